```python
import math
import jax, jax.numpy as jnp
from jax import lax
import numpy as np

D_MODEL = 1024
BATCH = 4
SEQ = 8192
DEPTH = 1

MEM_TOKENS = 256
D_FF = 2816
NORM_EPS = 1e-6

GDN_HEADS = 4
GDN_HEAD_DIM = 128
GDN_WIDTH = GDN_HEADS * GDN_HEAD_DIM
GDN_CONV = 4
GDN_CHUNK = 64

MOBA_HEADS = 4
MOBA_HEAD_DIM = 64
MOBA_WIDTH = MOBA_HEADS * MOBA_HEAD_DIM
MOBA_BLOCK = 256
MOBA_TOPK = 3
MOBA_Q_BLOCK = 128

MEM_HEADS = 4
MEM_HEAD_DIM = 64
MEM_WIDTH = MEM_HEADS * MEM_HEAD_DIM

D_MIX = GDN_WIDTH + MOBA_WIDTH + MEM_WIDTH
IN_SPLITS = (GDN_WIDTH, GDN_WIDTH, GDN_WIDTH, GDN_WIDTH, GDN_HEADS, GDN_HEADS,
             MOBA_WIDTH, MOBA_WIDTH, MOBA_WIDTH, MEM_WIDTH)

kernel_name = "hymba_gdn_moba_macaron_layer"


def rms_norm(x, gain, eps=NORM_EPS):
    xf = x.astype(jnp.float32)
    y = xf * lax.rsqrt(jnp.mean(xf * xf, axis=-1, keepdims=True) + eps)
    return (y * gain.astype(jnp.float32)).astype(x.dtype)


def l2_normalize(x, eps=NORM_EPS):
    return x * lax.rsqrt(jnp.sum(x * x, axis=-1, keepdims=True) + eps)


def swiglu_ffn(x, gain, w_gate, w_up, w_down):
    h = rms_norm(x, gain)
    return (jax.nn.silu(h @ w_gate) * (h @ w_up)) @ w_down


def causal_depthwise_conv(x, w):
    k_len, ch = w.shape
    return lax.conv_general_dilated(x, w[:, None, :], window_strides=(1,), padding=[(k_len - 1, 0)],
                                    dimension_numbers=("NWC", "WIO", "NWC"), feature_group_count=ch)


def alibi_slopes(n_heads):
    return 2.0 ** (-8.0 * jnp.arange(1, n_heads + 1, dtype=jnp.float32) / n_heads)


def gated_deltanet(q, k, v, z, a, b, conv_w, a_log, dt_bias, out_gain):
    f32 = jnp.float32
    bsz, seq, _ = q.shape
    H, Dh, C = GDN_HEADS, GDN_HEAD_DIM, GDN_CHUNK
    n_chunks = seq // C
    qkv = jax.nn.silu(causal_depthwise_conv(jnp.concatenate([q, k, v], axis=-1), conv_w))
    qkv = qkv.astype(f32).reshape(bsz, seq, 3, H, Dh)
    q = l2_normalize(qkv[:, :, 0]) * Dh ** -0.5
    k = l2_normalize(qkv[:, :, 1])
    v = qkv[:, :, 2]
    beta = jax.nn.sigmoid(b.astype(f32))
    g = -jnp.exp(a_log.astype(f32)) * jax.nn.softplus(a.astype(f32) + dt_bias.astype(f32))

    def chunks(t):
        t = t.reshape((bsz, n_chunks, C) + t.shape[2:])
        return jnp.moveaxis(jnp.moveaxis(t, 1, 0), 3, 2)

    qc, kc, vc = chunks(q), chunks(k), chunks(v)
    bc = chunks(beta)
    gc = jnp.cumsum(chunks(g), axis=-1)
    lower = jnp.tril(jnp.ones((C, C), dtype=bool))
    strict = jnp.tril(jnp.ones((C, C), dtype=bool), -1)
    diff = gc[..., :, None] - gc[..., None, :]
    decay = jnp.where(lower, jnp.exp(jnp.where(lower, diff, 0.0)), 0.0)
    k_beta = kc * bc[..., None]
    a_mat = jnp.where(strict, jnp.einsum("nbhid,nbhjd->nbhij", k_beta, kc) * decay, 0.0)
    l_mat = a_mat + jnp.eye(C, dtype=f32)
    u = lax.linalg.triangular_solve(l_mat, vc * bc[..., None], left_side=True, lower=True,
                                    unit_diagonal=True)
    w = lax.linalg.triangular_solve(l_mat, k_beta * jnp.exp(gc)[..., None], left_side=True,
                                    lower=True, unit_diagonal=True)
    qk_intra = jnp.einsum("nbhid,nbhjd->nbhij", qc, kc) * decay

    def step(state, inp):
        q_i, k_i, u_i, w_i, g_i, a_i = inp
        v_new = u_i - jnp.einsum("bhcd,bhde->bhce", w_i, state)
        o_i = (jnp.einsum("bhcd,bhde->bhce", q_i * jnp.exp(g_i)[..., None], state)
               + jnp.einsum("bhij,bhje->bhie", a_i, v_new))
        g_last = g_i[..., -1]
        state = (state * jnp.exp(g_last)[..., None, None]
                 + jnp.einsum("bhcd,bhce->bhde", k_i * jnp.exp(g_last[..., None] - g_i)[..., None], v_new))
        return state, o_i

    state0 = jnp.zeros((bsz, H, Dh, Dh), f32)
    _, o = lax.scan(step, state0, (qc, kc, u, w, gc, qk_intra))
    o = jnp.moveaxis(jnp.moveaxis(o, 0, 1), 2, 3).reshape(bsz, seq, H, Dh)
    o = rms_norm(o, out_gain) * jax.nn.silu(z.astype(f32).reshape(bsz, seq, H, Dh))
    return o.reshape(bsz, seq, H * Dh)


def moba_attention(q, k, v, q_gain, k_gain):
    f32 = jnp.float32
    bsz, seq, _ = q.shape
    H, Dh, BLK, QB = MOBA_HEADS, MOBA_HEAD_DIM, MOBA_BLOCK, MOBA_Q_BLOCK
    t_pad = -(-seq // BLK) * BLK
    n_blocks = t_pad // BLK
    n_qblocks = t_pad // QB
    topk = min(MOBA_TOPK, n_blocks)
    bh = bsz * H

    def heads(t, gain=None):
        t = t.reshape(bsz, seq, H, Dh)
        if gain is not None:
            t = rms_norm(t, gain)
        t = t.astype(f32).transpose(0, 2, 1, 3).reshape(bh, seq, Dh)
        return jnp.pad(t, ((0, 0), (0, t_pad - seq), (0, 0)))

    qh, kh, vh = heads(q, q_gain), heads(k, k_gain), heads(v)
    kb = kh.reshape(bh, n_blocks, BLK, Dh)
    vb = vh.reshape(bh, n_blocks, BLK, Dh)
    k_mean = jnp.mean(kb, axis=2)
    slopes = jnp.broadcast_to(alibi_slopes(H), (bsz, H)).reshape(bh)
    scale = Dh ** -0.5
    key_offs = jnp.arange(BLK)
    q_blocks = qh.reshape(bh, n_qblocks, QB, Dh).transpose(1, 0, 2, 3)

    def one_query_block(args):
        q_blk, c = args
        t_pos = c * QB + jnp.arange(QB)
        own = (c * QB) // BLK
        gate = jnp.einsum("nqd,nbd->nqb", q_blk, k_mean)
        gate = jnp.where(jnp.arange(n_blocks) < own, gate, -jnp.inf)
        _, idx = lax.top_k(gate, topk)
        valid = idx < own
        k_sel = jax.vmap(lambda blocks, ids: blocks[ids])(kb, idx)
        v_sel = jax.vmap(lambda blocks, ids: blocks[ids])(vb, idx)
        dist_sel = (t_pos[None, :, None, None] - (idx[..., None] * BLK + key_offs)).astype(f32)
        s_sel = (jnp.einsum("nqd,nqrkd->nqrk", q_blk, k_sel) * scale
                 - slopes[:, None, None, None] * dist_sel)
        s_sel = jnp.where(valid[..., None], s_sel, -jnp.inf).reshape(bh, QB, topk * BLK)
        k_own = lax.dynamic_index_in_dim(kb, own, axis=1, keepdims=False)
        v_own = lax.dynamic_index_in_dim(vb, own, axis=1, keepdims=False)
        dist_own = (t_pos[:, None] - (own * BLK + key_offs)[None, :]).astype(f32)
        s_own = jnp.einsum("nqd,nkd->nqk", q_blk, k_own) * scale - slopes[:, None, None] * dist_own
        s_own = jnp.where(dist_own >= 0, s_own, -jnp.inf)
        p = jax.nn.softmax(jnp.concatenate([s_sel, s_own], axis=-1), axis=-1)
        p_sel = p[..., :topk * BLK].reshape(bh, QB, topk, BLK)
        p_own = p[..., topk * BLK:]
        return (jnp.einsum("nqrk,nqrkd->nqd", p_sel, v_sel)
                + jnp.einsum("nqk,nkd->nqd", p_own, v_own))

    out = lax.map(one_query_block, (q_blocks, jnp.arange(n_qblocks)))
    out = out.transpose(1, 0, 2, 3).reshape(bsz, H, t_pad, Dh)[:, :, :seq]
    return out.transpose(0, 2, 1, 3).reshape(bsz, seq, H * Dh)


def memory_attention(q, mem_h, w_kv, q_gain, k_gain):
    f32 = jnp.float32
    bsz, seq, _ = q.shape
    n_mem = mem_h.shape[1]
    H, Dh = MEM_HEADS, MEM_HEAD_DIM
    k, v = jnp.split(mem_h @ w_kv, 2, axis=-1)
    qh = rms_norm(q.reshape(bsz, seq, H, Dh), q_gain).astype(f32)
    kh = rms_norm(k.reshape(bsz, n_mem, H, Dh), k_gain).astype(f32)
    vh = v.reshape(bsz, n_mem, H, Dh).astype(f32)
    p = jax.nn.softmax(jnp.einsum("bthd,bmhd->bhtm", qh, kh) * Dh ** -0.5, axis=-1)
    return jnp.einsum("bhtm,bmhd->bthd", p, vh).reshape(bsz, seq, H * Dh)


def setup_inputs(seed: int = 0) -> dict:
    key = jax.random.key(seed)
    ks = jax.random.split(key, 24)
    f32 = jnp.float32
    L = DEPTH
    d_in = sum(IN_SPLITS)

    def dense(k, fan_in, fan_out):
        return jax.random.normal(k, (L, fan_in, fan_out), f32) * fan_in ** -0.5

    def gain(k, n):
        return 1.0 + 0.02 * jax.random.normal(k, (L, n), f32)

    dt = jnp.exp(jax.random.uniform(ks[8], (L, GDN_HEADS), f32, math.log(1e-3), math.log(1e-1)))
    return {
        "x": jax.random.normal(ks[0], (BATCH, SEQ, D_MODEL), f32),
        "mem": jax.random.normal(ks[1], (BATCH, MEM_TOKENS, D_MODEL), f32),
        "ffn1_norm": gain(ks[2], D_MODEL),
        "ffn1_w_gate": dense(ks[3], D_MODEL, D_FF),
        "ffn1_w_up": dense(ks[4], D_MODEL, D_FF),
        "ffn1_w_down": dense(ks[5], D_FF, D_MODEL),
        "mix_norm": gain(ks[6], D_MODEL),
        "w_in": dense(ks[7], D_MODEL, d_in),
        "gdn_conv_w": jax.random.normal(ks[9], (L, GDN_CONV, 3 * GDN_WIDTH), f32) * GDN_CONV ** -0.5,
        "gdn_a_log": jnp.log(jax.random.uniform(ks[10], (L, GDN_HEADS), f32, 1.0, 16.0)),
        "gdn_dt_bias": dt + jnp.log(-jnp.expm1(-dt)),
        "gdn_out_norm": gain(ks[11], GDN_HEAD_DIM),
        "moba_q_norm": gain(ks[12], MOBA_HEAD_DIM),
        "moba_k_norm": gain(ks[13], MOBA_HEAD_DIM),
        "mem_norm": gain(ks[14], D_MODEL),
        "w_mem_kv": dense(ks[15], D_MODEL, 2 * MEM_WIDTH),
        "mem_q_norm": gain(ks[16], MEM_HEAD_DIM),
        "mem_k_norm": gain(ks[17], MEM_HEAD_DIM),
        "w_out": dense(ks[18], D_MIX, D_MODEL),
        "ffn2_norm": gain(ks[19], D_MODEL),
        "ffn2_w_gate": dense(ks[20], D_MODEL, D_FF),
        "ffn2_w_up": dense(ks[21], D_MODEL, D_FF),
        "ffn2_w_down": dense(ks[22], D_FF, D_MODEL),
    }


def reference(x, mem, ffn1_norm, ffn1_w_gate, ffn1_w_up, ffn1_w_down, mix_norm, w_in,
              gdn_conv_w, gdn_a_log, gdn_dt_bias, gdn_out_norm, moba_q_norm, moba_k_norm,
              mem_norm, w_mem_kv, mem_q_norm, mem_k_norm, w_out,
              ffn2_norm, ffn2_w_gate, ffn2_w_up, ffn2_w_down):
    split_at = np.cumsum(IN_SPLITS)[:-1].tolist()
    for l in range(DEPTH):
        x = x + 0.5 * swiglu_ffn(x, ffn1_norm[l], ffn1_w_gate[l], ffn1_w_up[l], ffn1_w_down[l])
        h = rms_norm(x, mix_norm[l])
        gq, gk, gv, gz, ga, gb, mq, mk, mv, cq = jnp.split(h @ w_in[l], split_at, axis=-1)
        y_gdn = gated_deltanet(gq, gk, gv, gz, ga, gb, gdn_conv_w[l], gdn_a_log[l], gdn_dt_bias[l],
                               gdn_out_norm[l])
        y_moba = moba_attention(mq, mk, mv, moba_q_norm[l], moba_k_norm[l])
        y_mem = memory_attention(cq, rms_norm(mem, mem_norm[l]), w_mem_kv[l], mem_q_norm[l], mem_k_norm[l])
        y = jnp.concatenate([y_gdn, y_moba, y_mem], axis=-1).astype(x.dtype)
        x = x + y @ w_out[l]
        x = x + 0.5 * swiglu_ffn(x, ffn2_norm[l], ffn2_w_gate[l], ffn2_w_up[l], ffn2_w_down[l])
    return x
```

```python
import functools

import jax
import jax.numpy as jnp
from jax import lax
from jax.experimental import pallas as pl
from jax.experimental.pallas import tpu as pltpu

F32 = jnp.float32
BF16 = jnp.bfloat16
HIGHEST = lax.Precision.HIGHEST

NORM_EPS = 1e-6
D_FF = 2816
GDN_HEADS = 4
GDN_HEAD_DIM = 128
GDN_WIDTH = GDN_HEADS * GDN_HEAD_DIM
GDN_CONV = 4
GDN_CHUNK = 64
ATT_HEADS = 4
ATT_HEAD_DIM = 64
ATT_WIDTH = ATT_HEADS * ATT_HEAD_DIM
MOBA_BLOCK = 256
MOBA_TOPK = 3
MASKED = -1e30

V7X_VMEM_LIMIT_BYTES = 56 * 1024 * 1024
FFN_ROWS = 256
GDN_ROWS = 256


def _dot(a, b, precision=None):
    return jnp.dot(a, b, preferred_element_type=F32, precision=precision)


def _dot_nt(a, b, precision=None):
    return lax.dot_general(a, b, (((1,), (1,)), ((), ())), preferred_element_type=F32,
                           precision=precision)


def _rms_rows(x, gain):
    return x * lax.rsqrt(jnp.mean(x * x, axis=-1, keepdims=True) + NORM_EPS) * gain


def _silu(x):
    return x * jax.nn.sigmoid(x)


def _swiglu(x, gain, wg_ref, wu_ref, wd_ref):
    h = _rms_rows(x, gain).astype(BF16)
    g = _dot(h, wg_ref[...])
    u = _dot(h, wu_ref[...])
    return _dot((_silu(g) * u).astype(BF16), wd_ref[...])


def _resident():
    return pl.BlockSpec(memory_space=pltpu.VMEM)


def _ffn1_inproj_kernel(x_ref, n1_ref, wg_ref, wu_ref, wd_ref, n2_ref, wgdn_ref, wab_ref, watt_ref,
                        x1_ref, pg_ref, pab_ref, patt_ref):
    x = x_ref[...]
    x1 = x + 0.5 * _swiglu(x, n1_ref[...], wg_ref, wu_ref, wd_ref)
    x1_ref[...] = x1
    h = _rms_rows(x1, n2_ref[...]).astype(BF16)
    pg_ref[...] = _dot(h, wgdn_ref[...])
    pab_ref[...] = _dot(h, wab_ref[...])
    patt_ref[...] = _dot(h, watt_ref[...])


def _ffn1_inproj(x2d, n1, wg, wu, wd, n2, wgdn, wab, watt):
    n, d = x2d.shape
    rows = FFN_ROWS
    row_spec = lambda w: pl.BlockSpec((rows, w), lambda i: (i, 0))
    return pl.pallas_call(
        _ffn1_inproj_kernel,
        grid=(n // rows,),
        in_specs=[row_spec(d)] + [_resident()] * 8,
        out_specs=[row_spec(d), row_spec(wgdn.shape[1]), row_spec(wab.shape[1]), row_spec(watt.shape[1])],
        out_shape=[jax.ShapeDtypeStruct((n, d), F32),
                   jax.ShapeDtypeStruct((n, wgdn.shape[1]), F32),
                   jax.ShapeDtypeStruct((n, wab.shape[1]), F32),
                   jax.ShapeDtypeStruct((n, watt.shape[1]), F32)],
        compiler_params=pltpu.CompilerParams(dimension_semantics=("arbitrary",),
                                             vmem_limit_bytes=V7X_VMEM_LIMIT_BYTES),
        name="ffn1_inproj",
    )(x2d, n1, wg, wu, wd, n2, wgdn, wab, watt)


def _outproj_ffn2_kernel(x1_ref, yg_ref, ya_ref, wog_ref, woa_ref, n_ref, wg_ref, wu_ref, wd_ref, o_ref):
    x2 = x1_ref[...] + _dot(yg_ref[...], wog_ref[...]) + _dot(ya_ref[...], woa_ref[...])
    o_ref[...] = x2 + 0.5 * _swiglu(x2, n_ref[...], wg_ref, wu_ref, wd_ref)


def _outproj_ffn2(x1, yg, ya, wog, woa, nrm, wg, wu, wd):
    n, d = x1.shape
    rows = FFN_ROWS
    row_spec = lambda w: pl.BlockSpec((rows, w), lambda i: (i, 0))
    return pl.pallas_call(
        _outproj_ffn2_kernel,
        grid=(n // rows,),
        in_specs=[row_spec(d), row_spec(yg.shape[1]), row_spec(ya.shape[1])] + [_resident()] * 6,
        out_specs=row_spec(d),
        out_shape=jax.ShapeDtypeStruct((n, d), F32),
        compiler_params=pltpu.CompilerParams(dimension_semantics=("arbitrary",),
                                             vmem_limit_bytes=V7X_VMEM_LIMIT_BYTES),
        name="outproj_ffn2",
    )(x1, yg, ya, wog, woa, nrm, wg, wu, wd)


def _head_block_ones(width, head_dim):
    r = lax.broadcasted_iota(jnp.int32, (width, width), 0) // head_dim
    c = lax.broadcasted_iota(jnp.int32, (width, width), 1) // head_dim
    return (r == c).astype(F32)


def _head_rms(x, gain, block_ones, head_dim):
    ss = _dot(x * x, block_ones, HIGHEST)
    return x * lax.rsqrt(ss * (1.0 / head_dim) + NORM_EPS) * gain


def _mem_kv_kernel(mem_ref, nrm_ref, wkv_ref, kg_ref, k_ref, vt_ref):
    h = _rms_rows(mem_ref[...], nrm_ref[...]).astype(BF16)
    kv = _dot(h, wkv_ref[...])
    k = _head_rms(kv[:, :ATT_WIDTH], kg_ref[...], _head_block_ones(ATT_WIDTH, ATT_HEAD_DIM), ATT_HEAD_DIM)
    k_ref[...] = k.astype(BF16)
    vt_ref[...] = kv[:, ATT_WIDTH:].T.astype(BF16)


def _mem_kv(mem, nrm, wkv, kgain):
    b, m, d = mem.shape
    return pl.pallas_call(
        _mem_kv_kernel,
        grid=(b,),
        in_specs=[pl.BlockSpec((None, m, d), lambda i: (i, 0, 0)), _resident(), _resident(), _resident()],
        out_specs=[pl.BlockSpec((None, m, ATT_WIDTH), lambda i: (i, 0, 0)),
                   pl.BlockSpec((None, ATT_WIDTH, m), lambda i: (i, 0, 0))],
        out_shape=[jax.ShapeDtypeStruct((b, m, ATT_WIDTH), BF16),
                   jax.ShapeDtypeStruct((b, ATT_WIDTH, m), BF16)],
        compiler_params=pltpu.CompilerParams(dimension_semantics=("arbitrary",)),
        name="mem_kv",
    )(mem, nrm, wkv, kgain)


def _softplus(x):
    return jnp.maximum(x, 0.0) + jnp.log(1.0 + jnp.exp(-jnp.abs(x)))


def _unit_lower_inverse(a_strict, eye):
    x = -a_strict
    t = eye + x
    for _ in range(5):
        x = _dot(x, x, HIGHEST)
        t = t + _dot(t, x, HIGHEST)
    return t


def _gdn_kernel(qkv_ref, z_ref, ab_ref, cw_ref, alog_ref, dtb_ref, og_ref, y_ref,
                xbuf_ref, q_ref, k_ref, v_ref, gb_ref, o_ref, s_ref):
    rows = z_ref.shape[0]
    H, Dh, C, W = GDN_HEADS, GDN_HEAD_DIM, GDN_CHUNK, GDN_WIDTH
    t = pl.program_id(1)

    @pl.when(t == 0)
    def _():
        xbuf_ref[0:8, :] = jnp.zeros((8, 3 * W), F32)
        s_ref[...] = jnp.zeros_like(s_ref)

    xbuf_ref[8:, :] = qkv_ref[...]
    conv = cw_ref[0:1, :] * xbuf_ref[5:5 + rows, :]
    for j in range(1, GDN_CONV):
        conv = conv + cw_ref[j:j + 1, :] * xbuf_ref[5 + j:5 + j + rows, :]
    xbuf_ref[0:8, :] = xbuf_ref[rows:rows + 8, :]
    act = _silu(conv)
    for h in range(H):
        qh = act[:, h * Dh:(h + 1) * Dh]
        kh = act[:, W + h * Dh:W + (h + 1) * Dh]
        q_ref[:, h * Dh:(h + 1) * Dh] = qh * (lax.rsqrt(jnp.sum(qh * qh, axis=-1, keepdims=True) + NORM_EPS)
                                               * Dh ** -0.5)
        k_ref[:, h * Dh:(h + 1) * Dh] = kh * lax.rsqrt(jnp.sum(kh * kh, axis=-1, keepdims=True) + NORM_EPS)
    v_ref[...] = act[:, 2 * W:]
    ab = ab_ref[...]
    lane = lax.broadcasted_iota(jnp.int32, ab.shape, 1)
    log_decay = -jnp.exp(alog_ref[...]) * _softplus(ab + dtb_ref[...])
    gb_ref[...] = jnp.where(lane < H, log_decay, jax.nn.sigmoid(ab))

    ri = lax.broadcasted_iota(jnp.int32, (C, C), 0)
    ci = lax.broadcasted_iota(jnp.int32, (C, C), 1)
    lower = ri >= ci
    strict = ri > ci
    lower_f = lower.astype(F32)
    upper_f = (ri <= ci).astype(F32)
    eye = (ri == ci).astype(F32)
    ones = jnp.ones((C, C), F32)

    def chunk_step(c, carry):
        r0 = pl.multiple_of(c * C, C)
        gb = gb_ref[pl.ds(r0, C), :]
        for h in range(H):
            cols = slice(h * Dh, (h + 1) * Dh)
            q = q_ref[pl.ds(r0, C), cols]
            k = k_ref[pl.ds(r0, C), cols]
            v = v_ref[pl.ds(r0, C), cols]
            g_b = jnp.broadcast_to(gb[:, h:h + 1], (C, Dh))
            beta = jnp.broadcast_to(gb[:, H + h:H + h + 1], (C, Dh))
            gc = _dot(lower_f, g_b, HIGHEST)
            gr = _dot(ones, g_b[:, :C] * upper_f, HIGHEST)
            diff = gc[:, :C] - gr
            decay = jnp.where(lower, jnp.exp(jnp.where(lower, diff, 0.0)), 0.0)
            kb = k * beta
            egc = jnp.exp(gc)
            a_mat = jnp.where(strict, _dot_nt(kb.astype(BF16), k.astype(BF16)) * decay, 0.0)
            t_inv = _unit_lower_inverse(a_mat, eye).astype(BF16)
            u = _dot(t_inv, (v * beta).astype(BF16))
            w = _dot(t_inv, (kb * egc).astype(BF16))
            qk = _dot_nt(q.astype(BF16), k.astype(BF16)) * decay
            state = s_ref[h]
            state_b = state.astype(BF16)
            v_new = u - _dot(w.astype(BF16), state_b)
            v_new_b = v_new.astype(BF16)
            o = _dot((q * egc).astype(BF16), state_b) + _dot(qk.astype(BF16), v_new_b)
            g_last = gc[C - 1:C, :]
            k_dec_t = (k * jnp.exp(g_last - gc)).T.astype(BF16)
            s_ref[h] = state * jnp.exp(g_last) + _dot(k_dec_t, v_new_b)
            o_ref[pl.ds(r0, C), cols] = o
        return carry

    lax.fori_loop(0, rows // C, chunk_step, 0)

    z = z_ref[...]
    o_all = o_ref[...]
    for h in range(H):
        cols = slice(h * Dh, (h + 1) * Dh)
        y_ref[:, cols] = (_rms_rows(o_all[:, cols], og_ref[...]) * _silu(z[:, cols])).astype(BF16)


def _gdn(pg, pab, conv_w, alog_vec, dtb_vec, out_gain, batch, seq):
    n = pg.shape[0]
    rows = GDN_ROWS
    tiles = seq // rows
    W = GDN_WIDTH
    return pl.pallas_call(
        _gdn_kernel,
        grid=(batch, tiles),
        in_specs=[pl.BlockSpec((rows, 3 * W), lambda b, t: (b * tiles + t, 0)),
                  pl.BlockSpec((rows, W), lambda b, t: (b * tiles + t, 3)),
                  pl.BlockSpec((rows, pab.shape[1]), lambda b, t: (b * tiles + t, 0)),
                  _resident(), _resident(), _resident(), _resident()],
        out_specs=pl.BlockSpec((rows, W), lambda b, t: (b * tiles + t, 0)),
        out_shape=jax.ShapeDtypeStruct((n, W), BF16),
        scratch_shapes=[pltpu.VMEM((rows + 8, 3 * W), F32),
                        pltpu.VMEM((rows, W), F32), pltpu.VMEM((rows, W), F32), pltpu.VMEM((rows, W), F32),
                        pltpu.VMEM((rows, pab.shape[1]), F32),
                        pltpu.VMEM((rows, W), F32),
                        pltpu.VMEM((GDN_HEADS, GDN_HEAD_DIM, GDN_HEAD_DIM), F32)],
        compiler_params=pltpu.CompilerParams(dimension_semantics=("arbitrary", "arbitrary"),
                                             vmem_limit_bytes=V7X_VMEM_LIMIT_BYTES),
        name="gdn",
    )(pg, pg, pab, conv_w, alog_vec, dtb_vec, out_gain)


def _moba_prep_kernel(p_ref, qg_ref, kg_ref, cg_ref, qt_ref, kn_ref, vt_ref, cqt_ref, sel_ref, km_ref):
    H, Dh, Wd = ATT_HEADS, ATT_HEAD_DIM, ATT_WIDTH
    nb = km_ref.shape[0]
    i = pl.program_id(1)

    @pl.when(i == 0)
    def _():
        km_ref[...] = jnp.zeros_like(km_ref)

    block_ones = _head_block_ones(Wd, Dh)
    qn = _head_rms(p_ref[:, 0:Wd], qg_ref[...], block_ones, Dh)
    kn = _head_rms(p_ref[:, Wd:2 * Wd], kg_ref[...], block_ones, Dh)
    cqn = _head_rms(p_ref[:, 3 * Wd:4 * Wd], cg_ref[...], block_ones, Dh)
    scale = Dh ** -0.5
    qt_ref[...] = (qn * scale).T.astype(BF16)
    cqt_ref[...] = (cqn * scale).T.astype(BF16)
    kn_ref[...] = kn.astype(BF16)
    vt_ref[...] = p_ref[:, 2 * Wd:3 * Wd].T.astype(BF16)

    km = km_ref[...]
    blk = lax.broadcasted_iota(jnp.int32, (nb, Wd), 0)
    lane_head = lax.broadcasted_iota(jnp.int32, (nb, Wd), 1) // Dh
    jdx = lax.broadcasted_iota(jnp.int32, (nb, MOBA_BLOCK), 0)
    past = jdx < i
    for h in range(H):
        gate = _dot_nt(jnp.where(lane_head == h, km, 0.0), qn, HIGHEST)
        gate = jnp.where(past, gate, -jnp.inf)
        rank = jnp.zeros(gate.shape, F32)
        for jp in range(nb):
            row = gate[jp:jp + 1, :]
            ahead = (row > gate) | ((row == gate) & (jp < jdx))
            rank = rank + ahead.astype(F32)
        sel_ref[h] = jnp.where((rank < MOBA_TOPK) & past, 0.0, MASKED)
    km_ref[...] = jnp.where(blk == i, jnp.mean(kn, axis=0, keepdims=True), km)


def _moba_prep(patt, qg, kg, cg, batch, seq):
    n = patt.shape[0]
    nb = seq // MOBA_BLOCK
    Wd, H, BLK = ATT_WIDTH, ATT_HEADS, MOBA_BLOCK
    return pl.pallas_call(
        _moba_prep_kernel,
        grid=(batch, nb),
        in_specs=[pl.BlockSpec((BLK, 4 * Wd), lambda b, i: (b * nb + i, 0)),
                  _resident(), _resident(), _resident()],
        out_specs=[pl.BlockSpec((None, Wd, BLK), lambda b, i: (b, 0, i)),
                   pl.BlockSpec((BLK, Wd), lambda b, i: (b * nb + i, 0)),
                   pl.BlockSpec((None, None, Wd, BLK), lambda b, i: (b, i, 0, 0)),
                   pl.BlockSpec((None, Wd, BLK), lambda b, i: (b, 0, i)),
                   pl.BlockSpec((None, H, nb, BLK), lambda b, i: (b, 0, 0, i))],
        out_shape=[jax.ShapeDtypeStruct((batch, Wd, seq), BF16),
                   jax.ShapeDtypeStruct((n, Wd), BF16),
                   jax.ShapeDtypeStruct((batch, nb, Wd, BLK), BF16),
                   jax.ShapeDtypeStruct((batch, Wd, seq), BF16),
                   jax.ShapeDtypeStruct((batch, H, nb, seq), F32)],
        scratch_shapes=[pltpu.VMEM((nb, Wd), F32)],
        compiler_params=pltpu.CompilerParams(dimension_semantics=("arbitrary", "arbitrary")),
        name="moba_prep",
    )(patt, qg, kg, cg)


def _alibi_slope(h):
    return 2.0 ** (-8.0 * (h + 1) / ATT_HEADS)


def _moba_attn_kernel(qt_ref, cqt_ref, k_ref, vt_ref, sel_ref, km_ref, vmt_ref, y_ref,
                      m_ref, l_ref, acc_ref):
    H, Dh, BLK = ATT_HEADS, ATT_HEAD_DIM, MOBA_BLOCK
    PAIR = 2 * Dh
    i = pl.program_id(1)
    key_minus_query = (lax.broadcasted_iota(jnp.int32, (BLK, BLK), 0)
                       - lax.broadcasted_iota(jnp.int32, (BLK, BLK), 1)).astype(F32)
    pair_row_head = lax.broadcasted_iota(jnp.int32, (PAIR, BLK), 0) // Dh

    def head_queries(src_ref, h):
        qp = src_ref[(h // 2) * PAIR:(h // 2 + 1) * PAIR, :]
        return jnp.where(pair_row_head == h % 2, qp, jnp.zeros_like(qp))

    def online_step(h, s_t, v_pair_t):
        m_old = m_ref[h]
        m_new = jnp.maximum(m_old, jnp.max(s_t, axis=0, keepdims=True))
        p = jnp.exp(s_t - m_new)
        alpha = jnp.exp(m_old - m_new)
        l_ref[h] = alpha * l_ref[h] + jnp.sum(p, axis=0, keepdims=True)
        acc_ref[h] = alpha * acc_ref[h] + _dot(v_pair_t, p.astype(BF16))
        m_ref[h] = m_new

    def assemble(h_out):
        parts = []
        for h in range(H):
            rows = slice((h % 2) * Dh, (h % 2 + 1) * Dh)
            parts.append(h_out(h)[rows, :])
        return jnp.concatenate(parts, axis=0).T

    m_ref[...] = jnp.full(m_ref.shape, MASKED, F32)
    l_ref[...] = jnp.zeros_like(l_ref)
    acc_ref[...] = jnp.zeros_like(acc_ref)

    own0 = pl.multiple_of(i * BLK, BLK)
    for h in range(H):
        lanes = slice((h // 2) * PAIR, (h // 2 + 1) * PAIR)
        s_t = _dot(k_ref[pl.ds(own0, BLK), lanes], head_queries(qt_ref, h))
        s_t = jnp.where(key_minus_query <= 0.0, s_t + _alibi_slope(h) * key_minus_query, MASKED)
        online_step(h, s_t, vt_ref[i, lanes, :])

    def past_block(j, carry):
        r0 = pl.multiple_of(j * BLK, BLK)
        blocks_back = (i - j).astype(F32) * BLK
        for h in range(H):
            lanes = slice((h // 2) * PAIR, (h // 2 + 1) * PAIR)
            s_t = _dot(k_ref[pl.ds(r0, BLK), lanes], head_queries(qt_ref, h))
            bias_row = sel_ref[h, pl.ds(j, 1), :] - _alibi_slope(h) * blocks_back
            s_t = s_t + _alibi_slope(h) * key_minus_query + bias_row
            online_step(h, s_t, vt_ref[j, lanes, :])
        return carry

    lax.fori_loop(0, i, past_block, 0)
    y_ref[:, 0:H * Dh] = assemble(lambda h: acc_ref[h] / l_ref[h]).astype(BF16)

    def mem_head(h):
        lanes = slice((h // 2) * PAIR, (h // 2 + 1) * PAIR)
        s_t = _dot(km_ref[:, lanes], head_queries(cqt_ref, h))
        p = jnp.exp(s_t - jnp.max(s_t, axis=0, keepdims=True))
        return _dot(vmt_ref[lanes, :], p.astype(BF16)) / jnp.sum(p, axis=0, keepdims=True)

    y_ref[:, H * Dh:2 * H * Dh] = assemble(mem_head).astype(BF16)


def _moba_attn(qt, cqt, kn, vt, sel, kmem, vmem_t, batch, seq):
    nb = seq // MOBA_BLOCK
    Wd, H, BLK = ATT_WIDTH, ATT_HEADS, MOBA_BLOCK
    n_mem = kmem.shape[1]
    return pl.pallas_call(
        _moba_attn_kernel,
        grid=(batch, nb),
        in_specs=[pl.BlockSpec((None, Wd, BLK), lambda b, i: (b, 0, i)),
                  pl.BlockSpec((None, Wd, BLK), lambda b, i: (b, 0, i)),
                  pl.BlockSpec((seq, Wd), lambda b, i: (b, 0)),
                  pl.BlockSpec((None, nb, Wd, BLK), lambda b, i: (b, 0, 0, 0)),
                  pl.BlockSpec((None, H, nb, BLK), lambda b, i: (b, 0, 0, i)),
                  pl.BlockSpec((None, n_mem, Wd), lambda b, i: (b, 0, 0)),
                  pl.BlockSpec((None, Wd, n_mem), lambda b, i: (b, 0, 0))],
        out_specs=pl.BlockSpec((BLK, 2 * Wd), lambda b, i: (b * nb + i, 0)),
        out_shape=jax.ShapeDtypeStruct((batch * seq, 2 * Wd), BF16),
        scratch_shapes=[pltpu.VMEM((H, 1, BLK), F32), pltpu.VMEM((H, 1, BLK), F32),
                        pltpu.VMEM((H, 2 * ATT_HEAD_DIM, BLK), F32)],
        compiler_params=pltpu.CompilerParams(dimension_semantics=("arbitrary", "arbitrary"),
                                             vmem_limit_bytes=V7X_VMEM_LIMIT_BYTES),
        name="moba_attn",
    )(qt, cqt, kn, vt, sel, kmem, vmem_t)


def _lane_vector(values, width):
    return jnp.zeros((1, width), F32).at[0, :values.shape[0]].set(values.astype(F32))


def _layer(x2d, mem, batch, seq, ffn1_norm, ffn1_w_gate, ffn1_w_up, ffn1_w_down, mix_norm, w_in,
           gdn_conv_w, gdn_a_log, gdn_dt_bias, gdn_out_norm, moba_q_norm, moba_k_norm,
           mem_norm, w_mem_kv, mem_q_norm, mem_k_norm, w_out, ffn2_norm, ffn2_w_gate, ffn2_w_up, ffn2_w_down):
    W, H = GDN_WIDTH, GDN_HEADS
    row = lambda v: v.reshape(1, -1).astype(F32)
    tile_heads = lambda v: jnp.tile(v.astype(F32), ATT_HEADS).reshape(1, -1)
    ab0 = 4 * W
    att0 = ab0 + 2 * H
    w_gdn = w_in[:, :ab0].astype(BF16)
    ab_width = 128
    w_ab = jnp.zeros((w_in.shape[0], ab_width), F32).at[:, :2 * H].set(w_in[:, ab0:att0]).astype(BF16)
    w_att = w_in[:, att0:].astype(BF16)

    x1, pg, pab, patt = _ffn1_inproj(
        x2d, row(ffn1_norm), ffn1_w_gate.astype(BF16), ffn1_w_up.astype(BF16), ffn1_w_down.astype(BF16),
        row(mix_norm), w_gdn, w_ab, w_att)

    y_gdn = _gdn(pg, pab, gdn_conv_w.astype(F32), _lane_vector(gdn_a_log, ab_width),
                 _lane_vector(gdn_dt_bias, ab_width), row(gdn_out_norm), batch, seq)

    kmem, vmem_t = _mem_kv(mem, row(mem_norm), w_mem_kv.astype(BF16), tile_heads(mem_k_norm))
    qt, kn, vt, cqt, sel = _moba_prep(patt, tile_heads(moba_q_norm), tile_heads(moba_k_norm),
                                      tile_heads(mem_q_norm), batch, seq)
    y_att = _moba_attn(qt, cqt, kn, vt, sel, kmem, vmem_t, batch, seq)

    w_out_b = w_out.astype(BF16)
    return _outproj_ffn2(x1, y_gdn, y_att, w_out_b[:W], w_out_b[W:], row(ffn2_norm),
                         ffn2_w_gate.astype(BF16), ffn2_w_up.astype(BF16), ffn2_w_down.astype(BF16))


def kernel(x, mem, ffn1_norm, ffn1_w_gate, ffn1_w_up, ffn1_w_down, mix_norm, w_in, gdn_conv_w, gdn_a_log,
           gdn_dt_bias, gdn_out_norm, moba_q_norm, moba_k_norm, mem_norm, w_mem_kv, mem_q_norm, mem_k_norm,
           w_out, ffn2_norm, ffn2_w_gate, ffn2_w_up, ffn2_w_down):
    batch, seq, d = x.shape
    assert seq % MOBA_BLOCK == 0 and seq % GDN_ROWS == 0 and (batch * seq) % FFN_ROWS == 0
    depth = w_in.shape[0]
    x2d = x.reshape(batch * seq, d)
    for l in range(depth):
        x2d = _layer(x2d, mem, batch, seq, ffn1_norm[l], ffn1_w_gate[l], ffn1_w_up[l], ffn1_w_down[l],
                     mix_norm[l], w_in[l], gdn_conv_w[l], gdn_a_log[l], gdn_dt_bias[l], gdn_out_norm[l],
                     moba_q_norm[l], moba_k_norm[l], mem_norm[l], w_mem_kv[l], mem_q_norm[l], mem_k_norm[l],
                     w_out[l], ffn2_norm[l], ffn2_w_gate[l], ffn2_w_up[l], ffn2_w_down[l])
    return x2d.reshape(batch, seq, d)
```

```python
import functools

import jax
import jax.numpy as jnp
from jax import lax
from jax.experimental import pallas as pl
from jax.experimental.pallas import tpu as pltpu

F32 = jnp.float32
BF16 = jnp.bfloat16
HIGHEST = lax.Precision.HIGHEST

NORM_EPS = 1e-6
D_FF = 2816
GDN_HEADS = 4
GDN_HEAD_DIM = 128
GDN_WIDTH = GDN_HEADS * GDN_HEAD_DIM
GDN_CONV = 4
GDN_CHUNK = 64
ATT_HEADS = 4
ATT_HEAD_DIM = 64
ATT_WIDTH = ATT_HEADS * ATT_HEAD_DIM
MOBA_BLOCK = 256
MOBA_TOPK = 3
MASKED = -1e30
LOG2_E = 1.4426950408889634

V7X_VMEM_LIMIT_BYTES = 56 * 1024 * 1024
FFN_ROWS = 256
GDN_ROWS = 256


def _dot(a, b, precision=None):
    return jnp.dot(a, b, preferred_element_type=F32, precision=precision)


def _dot_nt(a, b, precision=None):
    return lax.dot_general(a, b, (((1,), (1,)), ((), ())), preferred_element_type=F32,
                           precision=precision)


def _rms_rows(x, gain):
    return x * lax.rsqrt(jnp.mean(x * x, axis=-1, keepdims=True) + NORM_EPS) * gain


def _silu(x):
    return x * jax.nn.sigmoid(x)


def _swiglu(x, gain, wg_ref, wu_ref, wd_ref):
    h = _rms_rows(x, gain).astype(BF16)
    g = _dot(h, wg_ref[...])
    u = _dot(h, wu_ref[...])
    return _dot((_silu(g) * u).astype(BF16), wd_ref[...])


def _resident():
    return pl.BlockSpec(memory_space=pltpu.VMEM)


def _ffn1_inproj_kernel(x_ref, n1_ref, wg_ref, wu_ref, wd_ref, n2_ref, wgdn_ref, wab_ref, watt_ref,
                        x1_ref, pg_ref, pab_ref, patt_ref):
    x = x_ref[...]
    x1 = x + 0.5 * _swiglu(x, n1_ref[...], wg_ref, wu_ref, wd_ref)
    x1_ref[...] = x1
    h = _rms_rows(x1, n2_ref[...]).astype(BF16)
    pg_ref[...] = _dot(h, wgdn_ref[...])
    pab_ref[...] = _dot(h, wab_ref[...])
    patt_ref[...] = _dot(h, watt_ref[...])


def _ffn1_inproj(x2d, n1, wg, wu, wd, n2, wgdn, wab, watt):
    n, d = x2d.shape
    rows = FFN_ROWS
    row_spec = lambda w: pl.BlockSpec((rows, w), lambda i: (i, 0))
    return pl.pallas_call(
        _ffn1_inproj_kernel,
        grid=(n // rows,),
        in_specs=[row_spec(d)] + [_resident()] * 8,
        out_specs=[row_spec(d), row_spec(wgdn.shape[1]), row_spec(wab.shape[1]), row_spec(watt.shape[1])],
        out_shape=[jax.ShapeDtypeStruct((n, d), F32),
                   jax.ShapeDtypeStruct((n, wgdn.shape[1]), F32),
                   jax.ShapeDtypeStruct((n, wab.shape[1]), F32),
                   jax.ShapeDtypeStruct((n, watt.shape[1]), F32)],
        compiler_params=pltpu.CompilerParams(dimension_semantics=("arbitrary",),
                                             vmem_limit_bytes=V7X_VMEM_LIMIT_BYTES),
        name="ffn1_inproj",
    )(x2d, n1, wg, wu, wd, n2, wgdn, wab, watt)


def _outproj_ffn2_kernel(x1_ref, yg_ref, ya_ref, wog_ref, woa_ref, n_ref, wg_ref, wu_ref, wd_ref, o_ref):
    x2 = x1_ref[...] + _dot(yg_ref[...], wog_ref[...]) + _dot(ya_ref[...], woa_ref[...])
    o_ref[...] = x2 + 0.5 * _swiglu(x2, n_ref[...], wg_ref, wu_ref, wd_ref)


def _outproj_ffn2(x1, yg, ya, wog, woa, nrm, wg, wu, wd):
    n, d = x1.shape
    rows = FFN_ROWS
    row_spec = lambda w: pl.BlockSpec((rows, w), lambda i: (i, 0))
    return pl.pallas_call(
        _outproj_ffn2_kernel,
        grid=(n // rows,),
        in_specs=[row_spec(d), row_spec(yg.shape[1]), row_spec(ya.shape[1])] + [_resident()] * 6,
        out_specs=row_spec(d),
        out_shape=jax.ShapeDtypeStruct((n, d), F32),
        compiler_params=pltpu.CompilerParams(dimension_semantics=("arbitrary",),
                                             vmem_limit_bytes=V7X_VMEM_LIMIT_BYTES),
        name="outproj_ffn2",
    )(x1, yg, ya, wog, woa, nrm, wg, wu, wd)


def _head_block_ones(width, head_dim):
    r = lax.broadcasted_iota(jnp.int32, (width, width), 0) // head_dim
    c = lax.broadcasted_iota(jnp.int32, (width, width), 1) // head_dim
    return (r == c).astype(F32)


def _head_rms(x, gain, block_ones, head_dim):
    ss = _dot(x * x, block_ones, HIGHEST)
    return x * lax.rsqrt(ss * (1.0 / head_dim) + NORM_EPS) * gain


def _mem_kv_kernel(mem_ref, nrm_ref, wkv_ref, kg_ref, k_ref, vt_ref):
    h = _rms_rows(mem_ref[...], nrm_ref[...]).astype(BF16)
    kv = _dot(h, wkv_ref[...])
    k = _head_rms(kv[:, :ATT_WIDTH], kg_ref[...], _head_block_ones(ATT_WIDTH, ATT_HEAD_DIM), ATT_HEAD_DIM)
    k_ref[...] = k.astype(BF16)
    vt_ref[...] = kv[:, ATT_WIDTH:].T.astype(BF16)


def _mem_kv(mem, nrm, wkv, kgain):
    b, m, d = mem.shape
    return pl.pallas_call(
        _mem_kv_kernel,
        grid=(b,),
        in_specs=[pl.BlockSpec((None, m, d), lambda i: (i, 0, 0)), _resident(), _resident(), _resident()],
        out_specs=[pl.BlockSpec((None, m, ATT_WIDTH), lambda i: (i, 0, 0)),
                   pl.BlockSpec((None, ATT_WIDTH, m), lambda i: (i, 0, 0))],
        out_shape=[jax.ShapeDtypeStruct((b, m, ATT_WIDTH), BF16),
                   jax.ShapeDtypeStruct((b, ATT_WIDTH, m), BF16)],
        compiler_params=pltpu.CompilerParams(dimension_semantics=("arbitrary",)),
        name="mem_kv",
    )(mem, nrm, wkv, kgain)


def _softplus(x):
    return jnp.maximum(x, 0.0) + jnp.log(1.0 + jnp.exp(-jnp.abs(x)))


def _unit_lower_inverse(a_strict, eye):
    x = -a_strict
    t = eye + x
    for _ in range(5):
        xb = x.astype(BF16)
        x = _dot(xb, xb)
        t = t + _dot(t.astype(BF16), x.astype(BF16))
    tb = t.astype(BF16)
    t0 = tb.astype(F32)
    a_hi = a_strict.astype(BF16)
    a_lo = (a_strict - a_hi.astype(F32)).astype(BF16)
    resid = eye - t0 - _dot(a_hi, tb) - _dot(a_lo, tb)
    return t0 + _dot(tb, resid.astype(BF16))


def _gdn_kernel(qkv_ref, z_ref, ab_ref, cw_ref, alog_ref, dtb_ref, og_ref, y_ref, xbuf_ref, s_ref):
    rows = z_ref.shape[0]
    H, Dh, C, W = GDN_HEADS, GDN_HEAD_DIM, GDN_CHUNK, GDN_WIDTH
    n_chunks = rows // C
    t = pl.program_id(1)

    @pl.when(t == 0)
    def _():
        xbuf_ref[0:8, :] = jnp.zeros((8, 3 * W), F32)
        s_ref[...] = jnp.zeros_like(s_ref)

    xbuf_ref[8:, :] = qkv_ref[...]
    conv = cw_ref[0:1, :] * xbuf_ref[5:5 + rows, :]
    for j in range(1, GDN_CONV):
        conv = conv + cw_ref[j:j + 1, :] * xbuf_ref[5 + j:5 + j + rows, :]
    xbuf_ref[0:8, :] = xbuf_ref[rows:rows + 8, :]
    act = _silu(conv)

    ri = lax.broadcasted_iota(jnp.int32, (rows, rows), 0)
    ci = lax.broadcasted_iota(jnp.int32, (rows, rows), 1)
    same_chunk = (ri // C) == (ci // C)
    lower = same_chunk & (ri >= ci)
    strict = same_chunk & (ri > ci)
    eye = (ri == ci).astype(F32)

    ab = ab_ref[...]
    log_decay = -jnp.exp(alog_ref[...]) * _softplus(ab + dtb_ref[...])
    beta_all = jax.nn.sigmoid(ab)
    gc_all = _dot(lower.astype(F32), log_decay, HIGHEST)
    gc_rows = gc_all.T

    u_l, wq_l, qk_l, kdt_l, eg_last_l = [], [], [], [], []
    for h in range(H):
        qh = act[:, h * Dh:(h + 1) * Dh]
        kh = act[:, W + h * Dh:W + (h + 1) * Dh]
        v = act[:, 2 * W + h * Dh:2 * W + (h + 1) * Dh]
        q = qh * (lax.rsqrt(jnp.sum(qh * qh, axis=-1, keepdims=True) + NORM_EPS) * Dh ** -0.5)
        k = kh * lax.rsqrt(jnp.sum(kh * kh, axis=-1, keepdims=True) + NORM_EPS)
        gcol = jnp.broadcast_to(gc_all[:, h:h + 1], (rows, Dh))
        beta = jnp.broadcast_to(beta_all[:, H + h:H + h + 1], (rows, Dh))
        diff = jnp.concatenate([gcol] * (rows // Dh), axis=1) - gc_rows[h:h + 1, :]
        decay = jnp.where(lower, jnp.exp(jnp.where(lower, diff, 0.0)), 0.0)
        kb = k * beta
        k16 = k.astype(BF16)
        a_mat = jnp.where(strict, _dot_nt(kb.astype(BF16), k16) * decay, 0.0)
        qk = (_dot_nt(q.astype(BF16), k16) * decay).astype(BF16)
        t_inv = _unit_lower_inverse(a_mat, eye).astype(BF16)
        egc = jnp.exp(gcol)
        uw = _dot(t_inv, jnp.concatenate([v * beta, kb * egc], axis=1).astype(BF16))
        u_l.append(uw[:, :Dh])
        w16 = uw[:, Dh:].astype(BF16)
        qe16 = (q * egc).astype(BF16)
        wq_c, qk_c, kdt_c, egl_c = [], [], [], []
        for c in range(n_chunks):
            r = slice(c * C, (c + 1) * C)
            g_last = gcol[(c + 1) * C - 1:(c + 1) * C, :]
            wq_c.append(jnp.concatenate([w16[r], qe16[r]], axis=0))
            qk_c.append(qk[r, r])
            kdt_c.append((k[r] * jnp.exp(g_last - gcol[r])).T.astype(BF16))
            egl_c.append(jnp.exp(g_last))
        wq_l.append(wq_c)
        qk_l.append(qk_c)
        kdt_l.append(kdt_c)
        eg_last_l.append(egl_c)

    state = [s_ref[h] for h in range(H)]
    o_l = [[] for _ in range(H)]
    for c in range(n_chunks):
        r = slice(c * C, (c + 1) * C)
        ws = [_dot(wq_l[h][c], state[h].astype(BF16)) for h in range(H)]
        v_new = [(u_l[h][r] - ws[h][:C]).astype(BF16) for h in range(H)]
        state = [state[h] * eg_last_l[h][c] + _dot(kdt_l[h][c], v_new[h]) for h in range(H)]
        for h in range(H):
            o_l[h].append(ws[h][C:] + _dot(qk_l[h][c], v_new[h]))
    for h in range(H):
        s_ref[h] = state[h]

    z = z_ref[...]
    for h in range(H):
        cols = slice(h * Dh, (h + 1) * Dh)
        o = jnp.concatenate(o_l[h], axis=0)
        y_ref[:, cols] = (_rms_rows(o, og_ref[...]) * _silu(z[:, cols])).astype(BF16)


def _gdn(pg, pab, conv_w, alog_vec, dtb_vec, out_gain, batch, seq):
    n = pg.shape[0]
    rows = GDN_ROWS
    tiles = seq // rows
    W = GDN_WIDTH
    return pl.pallas_call(
        _gdn_kernel,
        grid=(batch, tiles),
        in_specs=[pl.BlockSpec((rows, 3 * W), lambda b, t: (b * tiles + t, 0)),
                  pl.BlockSpec((rows, W), lambda b, t: (b * tiles + t, 3)),
                  pl.BlockSpec((rows, pab.shape[1]), lambda b, t: (b * tiles + t, 0)),
                  _resident(), _resident(), _resident(), _resident()],
        out_specs=pl.BlockSpec((rows, W), lambda b, t: (b * tiles + t, 0)),
        out_shape=jax.ShapeDtypeStruct((n, W), BF16),
        scratch_shapes=[pltpu.VMEM((rows + 8, 3 * W), F32),
                        pltpu.VMEM((GDN_HEADS, GDN_HEAD_DIM, GDN_HEAD_DIM), F32)],
        compiler_params=pltpu.CompilerParams(dimension_semantics=("arbitrary", "arbitrary"),
                                             vmem_limit_bytes=V7X_VMEM_LIMIT_BYTES),
        name="gdn",
    )(pg, pg, pab, conv_w, alog_vec, dtb_vec, out_gain)


def _moba_prep_kernel(p_ref, qg_ref, kg_ref, cg_ref, qt_ref, kn_ref, vt_ref, cqt_ref, sel_ref, km_ref):
    H, Dh, Wd = ATT_HEADS, ATT_HEAD_DIM, ATT_WIDTH
    nb = km_ref.shape[0]
    i = pl.program_id(1)

    @pl.when(i == 0)
    def _():
        km_ref[...] = jnp.zeros_like(km_ref)

    block_ones = _head_block_ones(Wd, Dh)
    qn = _head_rms(p_ref[:, 0:Wd], qg_ref[...], block_ones, Dh)
    kn = _head_rms(p_ref[:, Wd:2 * Wd], kg_ref[...], block_ones, Dh)
    cqn = _head_rms(p_ref[:, 3 * Wd:4 * Wd], cg_ref[...], block_ones, Dh)
    scale = Dh ** -0.5 * LOG2_E
    qt_ref[...] = (qn * scale).T.astype(BF16)
    cqt_ref[...] = (cqn * scale).T.astype(BF16)
    kn_ref[...] = kn.astype(BF16)
    vt_ref[...] = p_ref[:, 2 * Wd:3 * Wd].T.astype(BF16)

    km = km_ref[...]
    blk = lax.broadcasted_iota(jnp.int32, (nb, Wd), 0)
    lane_head = lax.broadcasted_iota(jnp.int32, (nb, Wd), 1) // Dh
    jdx = lax.broadcasted_iota(jnp.int32, (nb, MOBA_BLOCK), 0)
    past = jdx < i
    for h in range(H):
        gate = _dot_nt(jnp.where(lane_head == h, km, 0.0), qn, HIGHEST)
        gate = jnp.where(past, gate, -jnp.inf)
        rank = jnp.zeros(gate.shape, F32)
        for jp in range(nb):
            row = gate[jp:jp + 1, :]
            ahead = (row > gate) | ((row == gate) & (jp < jdx))
            rank = rank + ahead.astype(F32)
        sel_ref[h] = jnp.where((rank < MOBA_TOPK) & past, 0.0, MASKED)
    km_ref[...] = jnp.where(blk == i, jnp.mean(kn, axis=0, keepdims=True), km)


def _moba_prep(patt, qg, kg, cg, batch, seq):
    n = patt.shape[0]
    nb = seq // MOBA_BLOCK
    Wd, H, BLK = ATT_WIDTH, ATT_HEADS, MOBA_BLOCK
    return pl.pallas_call(
        _moba_prep_kernel,
        grid=(batch, nb),
        in_specs=[pl.BlockSpec((BLK, 4 * Wd), lambda b, i: (b * nb + i, 0)),
                  _resident(), _resident(), _resident()],
        out_specs=[pl.BlockSpec((None, Wd, BLK), lambda b, i: (b, 0, i)),
                   pl.BlockSpec((BLK, Wd), lambda b, i: (b * nb + i, 0)),
                   pl.BlockSpec((None, None, Wd, BLK), lambda b, i: (b, i, 0, 0)),
                   pl.BlockSpec((None, Wd, BLK), lambda b, i: (b, 0, i)),
                   pl.BlockSpec((None, H, nb, BLK), lambda b, i: (b, 0, 0, i))],
        out_shape=[jax.ShapeDtypeStruct((batch, Wd, seq), BF16),
                   jax.ShapeDtypeStruct((n, Wd), BF16),
                   jax.ShapeDtypeStruct((batch, nb, Wd, BLK), BF16),
                   jax.ShapeDtypeStruct((batch, Wd, seq), BF16),
                   jax.ShapeDtypeStruct((batch, H, nb, seq), F32)],
        scratch_shapes=[pltpu.VMEM((nb, Wd), F32)],
        compiler_params=pltpu.CompilerParams(dimension_semantics=("arbitrary", "arbitrary")),
        name="moba_prep",
    )(patt, qg, kg, cg)


def _alibi_slope(h):
    return 2.0 ** (-8.0 * (h + 1) / ATT_HEADS)


def _moba_attn_kernel(qt_ref, cqt_ref, k_ref, vt_ref, sel_ref, km_ref, vmt_ref, y_ref,
                      m_ref, l_ref, acc_ref, s_ref, alibi_ref):
    H, Dh, BLK = ATT_HEADS, ATT_HEAD_DIM, MOBA_BLOCK
    PAIR = 2 * Dh
    i = pl.program_id(1)
    pair_row_head = lax.broadcasted_iota(jnp.int32, (PAIR, BLK), 0) // Dh

    @pl.when((pl.program_id(0) == 0) & (i == 0))
    def _():
        key_minus_query = (lax.broadcasted_iota(jnp.int32, (BLK, BLK), 0)
                           - lax.broadcasted_iota(jnp.int32, (BLK, BLK), 1)).astype(F32)
        for h in range(H):
            alibi_ref[h] = (_alibi_slope(h) * LOG2_E) * key_minus_query

    def pair_lanes(h):
        return slice((h // 2) * PAIR, (h // 2 + 1) * PAIR)

    def head_queries(src_ref, h):
        qp = src_ref[pair_lanes(h), :]
        return jnp.where(pair_row_head == h % 2, qp, jnp.zeros_like(qp))

    q_heads = [head_queries(qt_ref, h) for h in range(H)]

    def raw_scores(blk):
        r0 = pl.multiple_of(blk * BLK, BLK)
        return [_dot(k_ref[pl.ds(r0, BLK), pair_lanes(h)], q_heads[h]) for h in range(H)]

    def assemble(h_out):
        parts = []
        for h in range(H):
            rows = slice((h % 2) * Dh, (h % 2 + 1) * Dh)
            parts.append(h_out(h)[rows, :])
        return jnp.concatenate(parts, axis=0).T

    own_scores = raw_scores(i)
    first_past = raw_scores(0)
    for h in range(H):
        alibi = alibi_ref[h]
        s_t = jnp.where(alibi <= 0.0, own_scores[h] + alibi, MASKED)
        m = jnp.max(s_t, axis=0, keepdims=True)
        p = jnp.exp2(s_t - m)
        m_ref[h] = m
        l_ref[h] = jnp.sum(p, axis=0, keepdims=True)
        acc_ref[h] = _dot(vt_ref[i, pair_lanes(h), :], p.astype(BF16))
        s_ref[0, h] = first_past[h]

    def past_block(j, carry):
        cur = j % 2
        nxt = jnp.minimum(j + 1, jnp.maximum(i - 1, 0))
        next_scores = raw_scores(nxt)
        blocks_back = (i - j).astype(F32) * BLK
        for h in range(H):
            bias_row = sel_ref[h, pl.ds(j, 1), :] - (_alibi_slope(h) * LOG2_E) * blocks_back
            s_t = s_ref[cur, h] + alibi_ref[h] + bias_row
            m_old = m_ref[h]
            m_new = jnp.maximum(m_old, jnp.max(s_t, axis=0, keepdims=True))
            p = jnp.exp2(s_t - m_new)
            alpha = jnp.exp2(m_old - m_new)
            l_ref[h] = alpha * l_ref[h] + jnp.sum(p, axis=0, keepdims=True)
            acc_ref[h] = alpha * acc_ref[h] + _dot(vt_ref[j, pair_lanes(h), :], p.astype(BF16))
            m_ref[h] = m_new
        for h in range(H):
            s_ref[1 - cur, h] = next_scores[h]
        return carry

    lax.fori_loop(0, i, past_block, 0)
    y_ref[:, 0:H * Dh] = assemble(lambda h: acc_ref[h] / l_ref[h]).astype(BF16)

    def mem_head(h):
        s_t = _dot(km_ref[:, pair_lanes(h)], head_queries(cqt_ref, h))
        p = jnp.exp2(s_t - jnp.max(s_t, axis=0, keepdims=True))
        return _dot(vmt_ref[pair_lanes(h), :], p.astype(BF16)) / jnp.sum(p, axis=0, keepdims=True)

    y_ref[:, H * Dh:2 * H * Dh] = assemble(mem_head).astype(BF16)


def _moba_attn(qt, cqt, kn, vt, sel, kmem, vmem_t, batch, seq):
    nb = seq // MOBA_BLOCK
    Wd, H, BLK = ATT_WIDTH, ATT_HEADS, MOBA_BLOCK
    n_mem = kmem.shape[1]
    return pl.pallas_call(
        _moba_attn_kernel,
        grid=(batch, nb),
        in_specs=[pl.BlockSpec((None, Wd, BLK), lambda b, i: (b, 0, i)),
                  pl.BlockSpec((None, Wd, BLK), lambda b, i: (b, 0, i)),
                  pl.BlockSpec((seq, Wd), lambda b, i: (b, 0)),
                  pl.BlockSpec((None, nb, Wd, BLK), lambda b, i: (b, 0, 0, 0)),
                  pl.BlockSpec((None, H, nb, BLK), lambda b, i: (b, 0, 0, i)),
                  pl.BlockSpec((None, n_mem, Wd), lambda b, i: (b, 0, 0)),
                  pl.BlockSpec((None, Wd, n_mem), lambda b, i: (b, 0, 0))],
        out_specs=pl.BlockSpec((BLK, 2 * Wd), lambda b, i: (b * nb + i, 0)),
        out_shape=jax.ShapeDtypeStruct((batch * seq, 2 * Wd), BF16),
        scratch_shapes=[pltpu.VMEM((H, 1, BLK), F32), pltpu.VMEM((H, 1, BLK), F32),
                        pltpu.VMEM((H, 2 * ATT_HEAD_DIM, BLK), F32),
                        pltpu.VMEM((2, H, BLK, BLK), F32),
                        pltpu.VMEM((H, BLK, BLK), F32)],
        compiler_params=pltpu.CompilerParams(dimension_semantics=("arbitrary", "arbitrary"),
                                             vmem_limit_bytes=V7X_VMEM_LIMIT_BYTES),
        name="moba_attn",
    )(qt, cqt, kn, vt, sel, kmem, vmem_t)


def _lane_vector(values, width):
    return jnp.zeros((1, width), F32).at[0, :values.shape[0]].set(values.astype(F32))


def _layer(x2d, mem, batch, seq, ffn1_norm, ffn1_w_gate, ffn1_w_up, ffn1_w_down, mix_norm, w_in,
           gdn_conv_w, gdn_a_log, gdn_dt_bias, gdn_out_norm, moba_q_norm, moba_k_norm,
           mem_norm, w_mem_kv, mem_q_norm, mem_k_norm, w_out, ffn2_norm, ffn2_w_gate, ffn2_w_up, ffn2_w_down):
    W, H = GDN_WIDTH, GDN_HEADS
    row = lambda v: v.reshape(1, -1).astype(F32)
    tile_heads = lambda v: jnp.tile(v.astype(F32), ATT_HEADS).reshape(1, -1)
    ab0 = 4 * W
    att0 = ab0 + 2 * H
    w_gdn = w_in[:, :ab0].astype(BF16)
    ab_width = 128
    w_ab = jnp.zeros((w_in.shape[0], ab_width), F32).at[:, :2 * H].set(w_in[:, ab0:att0]).astype(BF16)
    w_att = w_in[:, att0:].astype(BF16)

    x1, pg, pab, patt = _ffn1_inproj(
        x2d, row(ffn1_norm), ffn1_w_gate.astype(BF16), ffn1_w_up.astype(BF16), ffn1_w_down.astype(BF16),
        row(mix_norm), w_gdn, w_ab, w_att)

    y_gdn = _gdn(pg, pab, gdn_conv_w.astype(F32), _lane_vector(gdn_a_log, ab_width),
                 _lane_vector(gdn_dt_bias, ab_width), row(gdn_out_norm), batch, seq)

    kmem, vmem_t = _mem_kv(mem, row(mem_norm), w_mem_kv.astype(BF16), tile_heads(mem_k_norm))
    qt, kn, vt, cqt, sel = _moba_prep(patt, tile_heads(moba_q_norm), tile_heads(moba_k_norm),
                                      tile_heads(mem_q_norm), batch, seq)
    y_att = _moba_attn(qt, cqt, kn, vt, sel, kmem, vmem_t, batch, seq)

    w_out_b = w_out.astype(BF16)
    return _outproj_ffn2(x1, y_gdn, y_att, w_out_b[:W], w_out_b[W:], row(ffn2_norm),
                         ffn2_w_gate.astype(BF16), ffn2_w_up.astype(BF16), ffn2_w_down.astype(BF16))


def kernel(x, mem, ffn1_norm, ffn1_w_gate, ffn1_w_up, ffn1_w_down, mix_norm, w_in, gdn_conv_w, gdn_a_log,
           gdn_dt_bias, gdn_out_norm, moba_q_norm, moba_k_norm, mem_norm, w_mem_kv, mem_q_norm, mem_k_norm,
           w_out, ffn2_norm, ffn2_w_gate, ffn2_w_up, ffn2_w_down):
    batch, seq, d = x.shape
    assert seq % MOBA_BLOCK == 0 and seq % GDN_ROWS == 0 and (batch * seq) % FFN_ROWS == 0
    depth = w_in.shape[0]
    x2d = x.reshape(batch * seq, d)
    for l in range(depth):
        x2d = _layer(x2d, mem, batch, seq, ffn1_norm[l], ffn1_w_gate[l], ffn1_w_up[l], ffn1_w_down[l],
                     mix_norm[l], w_in[l], gdn_conv_w[l], gdn_a_log[l], gdn_dt_bias[l], gdn_out_norm[l],
                     moba_q_norm[l], moba_k_norm[l], mem_norm[l], w_mem_kv[l], mem_q_norm[l], mem_k_norm[l],
                     w_out[l], ffn2_norm[l], ffn2_w_gate[l], ffn2_w_up[l], ffn2_w_down[l])
    return x2d.reshape(batch, seq, d)
```

```python
import functools

import jax
import jax.numpy as jnp
from jax import lax
from jax.experimental import pallas as pl
from jax.experimental.pallas import tpu as pltpu

F32 = jnp.float32
BF16 = jnp.bfloat16
HIGHEST = lax.Precision.HIGHEST

NORM_EPS = 1e-6
D_FF = 2816
GDN_HEADS = 4
GDN_HEAD_DIM = 128
GDN_WIDTH = GDN_HEADS * GDN_HEAD_DIM
GDN_CONV = 4
GDN_CHUNK = 64
GDN_GROUP = 2 * GDN_CHUNK
ATT_HEADS = 4
ATT_HEAD_DIM = 64
ATT_WIDTH = ATT_HEADS * ATT_HEAD_DIM
MOBA_BLOCK = 256
MOBA_TOPK = 3
MASKED = -1e30
LOG2_E = 1.4426950408889634

V7X_VMEM_LIMIT_BYTES = 56 * 1024 * 1024
FFN_ROWS = 256
GDN_ROWS = 256


def _dot(a, b, precision=None):
    return jnp.dot(a, b, preferred_element_type=F32, precision=precision)


def _dot_nt(a, b, precision=None):
    return lax.dot_general(a, b, (((1,), (1,)), ((), ())), preferred_element_type=F32,
                           precision=precision)


def _rms_rows(x, gain):
    return x * lax.rsqrt(jnp.mean(x * x, axis=-1, keepdims=True) + NORM_EPS) * gain


def _silu(x):
    return x * jax.nn.sigmoid(x)


def _swiglu(x, gain, wg_ref, wu_ref, wd_ref):
    h = _rms_rows(x, gain).astype(BF16)
    g = _dot(h, wg_ref[...])
    u = _dot(h, wu_ref[...])
    return _dot((_silu(g) * u).astype(BF16), wd_ref[...])


def _resident():
    return pl.BlockSpec(memory_space=pltpu.VMEM)


def _ffn1_inproj_kernel(x_ref, n1_ref, wg_ref, wu_ref, wd_ref, n2_ref, wgdn_ref, wab_ref, watt_ref,
                        x1_ref, pg_ref, pab_ref, patt_ref):
    x = x_ref[...]
    x1 = x + 0.5 * _swiglu(x, n1_ref[...], wg_ref, wu_ref, wd_ref)
    x1_ref[...] = x1
    h = _rms_rows(x1, n2_ref[...]).astype(BF16)
    pg_ref[...] = _dot(h, wgdn_ref[...])
    pab_ref[...] = _dot(h, wab_ref[...])
    patt_ref[...] = _dot(h, watt_ref[...])


def _ffn1_inproj(x2d, n1, wg, wu, wd, n2, wgdn, wab, watt):
    n, d = x2d.shape
    rows = FFN_ROWS
    row_spec = lambda w: pl.BlockSpec((rows, w), lambda i: (i, 0))
    return pl.pallas_call(
        _ffn1_inproj_kernel,
        grid=(n // rows,),
        in_specs=[row_spec(d)] + [_resident()] * 8,
        out_specs=[row_spec(d), row_spec(wgdn.shape[1]), row_spec(wab.shape[1]), row_spec(watt.shape[1])],
        out_shape=[jax.ShapeDtypeStruct((n, d), F32),
                   jax.ShapeDtypeStruct((n, wgdn.shape[1]), F32),
                   jax.ShapeDtypeStruct((n, wab.shape[1]), F32),
                   jax.ShapeDtypeStruct((n, watt.shape[1]), F32)],
        compiler_params=pltpu.CompilerParams(dimension_semantics=("arbitrary",),
                                             vmem_limit_bytes=V7X_VMEM_LIMIT_BYTES),
        name="ffn1_inproj",
    )(x2d, n1, wg, wu, wd, n2, wgdn, wab, watt)


def _outproj_ffn2_kernel(x1_ref, yg_ref, ya_ref, wog_ref, woa_ref, n_ref, wg_ref, wu_ref, wd_ref, o_ref):
    x2 = x1_ref[...] + _dot(yg_ref[...], wog_ref[...]) + _dot(ya_ref[...], woa_ref[...])
    o_ref[...] = x2 + 0.5 * _swiglu(x2, n_ref[...], wg_ref, wu_ref, wd_ref)


def _outproj_ffn2(x1, yg, ya, wog, woa, nrm, wg, wu, wd):
    n, d = x1.shape
    rows = FFN_ROWS
    row_spec = lambda w: pl.BlockSpec((rows, w), lambda i: (i, 0))
    return pl.pallas_call(
        _outproj_ffn2_kernel,
        grid=(n // rows,),
        in_specs=[row_spec(d), row_spec(yg.shape[1]), row_spec(ya.shape[1])] + [_resident()] * 6,
        out_specs=row_spec(d),
        out_shape=jax.ShapeDtypeStruct((n, d), F32),
        compiler_params=pltpu.CompilerParams(dimension_semantics=("arbitrary",),
                                             vmem_limit_bytes=V7X_VMEM_LIMIT_BYTES),
        name="outproj_ffn2",
    )(x1, yg, ya, wog, woa, nrm, wg, wu, wd)


def _head_block_ones(width, head_dim):
    r = lax.broadcasted_iota(jnp.int32, (width, width), 0) // head_dim
    c = lax.broadcasted_iota(jnp.int32, (width, width), 1) // head_dim
    return (r == c).astype(F32)


def _head_rms(x, gain, block_ones, head_dim):
    ss = _dot(x * x, block_ones, HIGHEST)
    return x * lax.rsqrt(ss * (1.0 / head_dim) + NORM_EPS) * gain


def _mem_kv_kernel(mem_ref, nrm_ref, wkv_ref, kg_ref, k_ref, vt_ref):
    h = _rms_rows(mem_ref[...], nrm_ref[...]).astype(BF16)
    kv = _dot(h, wkv_ref[...])
    k = _head_rms(kv[:, :ATT_WIDTH], kg_ref[...], _head_block_ones(ATT_WIDTH, ATT_HEAD_DIM), ATT_HEAD_DIM)
    k_ref[...] = k.astype(BF16)
    vt_ref[...] = kv[:, ATT_WIDTH:].T.astype(BF16)


def _mem_kv(mem, nrm, wkv, kgain):
    b, m, d = mem.shape
    return pl.pallas_call(
        _mem_kv_kernel,
        grid=(b,),
        in_specs=[pl.BlockSpec((None, m, d), lambda i: (i, 0, 0)), _resident(), _resident(), _resident()],
        out_specs=[pl.BlockSpec((None, m, ATT_WIDTH), lambda i: (i, 0, 0)),
                   pl.BlockSpec((None, ATT_WIDTH, m), lambda i: (i, 0, 0))],
        out_shape=[jax.ShapeDtypeStruct((b, m, ATT_WIDTH), BF16),
                   jax.ShapeDtypeStruct((b, ATT_WIDTH, m), BF16)],
        compiler_params=pltpu.CompilerParams(dimension_semantics=("arbitrary",)),
        name="mem_kv",
    )(mem, nrm, wkv, kgain)


def _softplus(x):
    return jnp.maximum(x, 0.0) + jnp.log(1.0 + jnp.exp(-jnp.abs(x)))


def _unit_lower_inverses(a_list, eye):
    x16 = [(-a).astype(BF16) for a in a_list]
    t = [eye - a for a in a_list]
    for _ in range(5):
        x16 = [_dot(xb, xb).astype(BF16) for xb in x16]
        t = [ti + _dot(ti.astype(BF16), xb) for ti, xb in zip(t, x16)]
    t16 = [ti.astype(BF16) for ti in t]
    a_hi = [a.astype(BF16) for a in a_list]
    a_lo = [(a - hi.astype(F32)).astype(BF16) for a, hi in zip(a_list, a_hi)]
    resid = [(eye - tb.astype(F32) - _dot(hi, tb) - _dot(lo, tb)).astype(BF16)
             for tb, hi, lo in zip(t16, a_hi, a_lo)]
    return [tb.astype(F32) + _dot(tb, r) for tb, r in zip(t16, resid)]


def _gdn_kernel(qkv_ref, z_ref, ab_ref, cw_ref, alog_ref, dtb_ref, og_ref, y_ref, xbuf_ref, s_ref):
    rows = z_ref.shape[0]
    H, Dh, C, W = GDN_HEADS, GDN_HEAD_DIM, GDN_CHUNK, GDN_WIDTH
    G = GDN_GROUP
    n_chunks = rows // C
    n_groups = rows // G
    t = pl.program_id(1)

    @pl.when(t == 0)
    def _():
        xbuf_ref[0:8, :] = jnp.zeros((8, 3 * W), F32)
        s_ref[...] = jnp.zeros_like(s_ref)

    xbuf_ref[8:, :] = qkv_ref[...]
    conv = cw_ref[0:1, :] * xbuf_ref[5:5 + rows, :]
    for j in range(1, GDN_CONV):
        conv = conv + cw_ref[j:j + 1, :] * xbuf_ref[5 + j:5 + j + rows, :]
    xbuf_ref[0:8, :] = xbuf_ref[rows:rows + 8, :]
    act = _silu(conv)

    ab = ab_ref[...]
    log_decay = -jnp.exp(alog_ref[...]) * _softplus(ab + dtb_ref[...])
    beta_all = jax.nn.sigmoid(ab)
    rt = lax.broadcasted_iota(jnp.int32, (rows, rows), 0)
    ct = lax.broadcasted_iota(jnp.int32, (rows, rows), 1)
    chunk_lower = (((rt // C) == (ct // C)) & (rt >= ct)).astype(F32)
    gc_all = _dot(chunk_lower, log_decay, HIGHEST)
    gc_rows = gc_all.T

    ri = lax.broadcasted_iota(jnp.int32, (G, G), 0)
    ci = lax.broadcasted_iota(jnp.int32, (G, G), 1)
    same_chunk = (ri // C) == (ci // C)
    lower = same_chunk & (ri >= ci)
    strict = same_chunk & (ri > ci)
    eye = (ri == ci).astype(F32)

    units = [(h, g) for h in range(H) for g in range(n_groups)]
    q_h, k_h, v_h, gcol_h, beta_h = [], [], [], [], []
    for h in range(H):
        qh = act[:, h * Dh:(h + 1) * Dh]
        kh = act[:, W + h * Dh:W + (h + 1) * Dh]
        v_h.append(act[:, 2 * W + h * Dh:2 * W + (h + 1) * Dh])
        q_h.append(qh * (lax.rsqrt(jnp.sum(qh * qh, axis=-1, keepdims=True) + NORM_EPS) * Dh ** -0.5))
        k_h.append(kh * lax.rsqrt(jnp.sum(kh * kh, axis=-1, keepdims=True) + NORM_EPS))
        gcol_h.append(jnp.broadcast_to(gc_all[:, h:h + 1], (rows, Dh)))
        beta_h.append(jnp.broadcast_to(beta_all[:, H + h:H + h + 1], (rows, Dh)))

    def unit_rows(x, g):
        return x[g * G:(g + 1) * G]

    k_u = [unit_rows(k_h[h], g) for h, g in units]
    gcol_u = [unit_rows(gcol_h[h], g) for h, g in units]
    kb_u = [k * unit_rows(beta_h[h], g) for k, (h, g) in zip(k_u, units)]
    k16_u = [k.astype(BF16) for k in k_u]
    q_u = [unit_rows(q_h[h], g) for h, g in units]
    kk_u = [_dot_nt(kb.astype(BF16), k16) for kb, k16 in zip(kb_u, k16_u)]
    qk_u = [_dot_nt(q.astype(BF16), k16) for q, k16 in zip(q_u, k16_u)]
    decay_u = []
    for gcol, (h, g) in zip(gcol_u, units):
        diff = gcol - gc_rows[h:h + 1, g * G:(g + 1) * G]
        decay_u.append(jnp.where(lower, jnp.exp(jnp.where(lower, diff, 0.0)), 0.0))
    a_u = [jnp.where(strict, kk * decay, 0.0) for kk, decay in zip(kk_u, decay_u)]
    qk16_u = [(qk * decay).astype(BF16) for qk, decay in zip(qk_u, decay_u)]
    tinv_u = _unit_lower_inverses(a_u, eye)
    egc_u = [jnp.exp(gcol) for gcol in gcol_u]
    rhs_u = [jnp.concatenate([unit_rows(v_h[h], g) * unit_rows(beta_h[h], g), kb * egc], axis=1).astype(BF16)
             for kb, egc, (h, g) in zip(kb_u, egc_u, units)]
    uw_u = [_dot(tinv.astype(BF16), rhs) for tinv, rhs in zip(tinv_u, rhs_u)]
    w16_u = [uw[:, Dh:].astype(BF16) for uw in uw_u]
    qe16_u = [(q * egc).astype(BF16) for q, egc in zip(q_u, egc_u)]

    def chunk_operands(h, c):
        u = h * n_groups + (c * C) // G
        r = slice((c * C) % G, (c * C) % G + C)
        g_last = gcol_u[u][r.stop - 1:r.stop, :]
        return (uw_u[u][r, :Dh],
                jnp.concatenate([w16_u[u][r], qe16_u[u][r]], axis=0),
                qk16_u[u][r, r],
                (k_u[u][r] * jnp.exp(g_last - gcol_u[u][r])).T.astype(BF16),
                jnp.exp(g_last))

    ops = [[chunk_operands(h, c) for c in range(n_chunks)] for h in range(H)]

    state = [s_ref[h] for h in range(H)]
    o_l = [[] for _ in range(H)]
    for c in range(n_chunks):
        ws = [_dot(ops[h][c][1], state[h].astype(BF16)) for h in range(H)]
        v_new = [(ops[h][c][0] - ws[h][:C]).astype(BF16) for h in range(H)]
        state = [state[h] * ops[h][c][4] + _dot(ops[h][c][3], v_new[h]) for h in range(H)]
        for h in range(H):
            o_l[h].append(ws[h][C:] + _dot(ops[h][c][2], v_new[h]))
    for h in range(H):
        s_ref[h] = state[h]

    z = z_ref[...]
    for h in range(H):
        cols = slice(h * Dh, (h + 1) * Dh)
        o = jnp.concatenate(o_l[h], axis=0)
        y_ref[:, cols] = (_rms_rows(o, og_ref[...]) * _silu(z[:, cols])).astype(BF16)


def _gdn(pg, pab, conv_w, alog_vec, dtb_vec, out_gain, batch, seq):
    n = pg.shape[0]
    rows = GDN_ROWS
    tiles = seq // rows
    W = GDN_WIDTH
    return pl.pallas_call(
        _gdn_kernel,
        grid=(batch, tiles),
        in_specs=[pl.BlockSpec((rows, 3 * W), lambda b, t: (b * tiles + t, 0)),
                  pl.BlockSpec((rows, W), lambda b, t: (b * tiles + t, 3)),
                  pl.BlockSpec((rows, pab.shape[1]), lambda b, t: (b * tiles + t, 0)),
                  _resident(), _resident(), _resident(), _resident()],
        out_specs=pl.BlockSpec((rows, W), lambda b, t: (b * tiles + t, 0)),
        out_shape=jax.ShapeDtypeStruct((n, W), BF16),
        scratch_shapes=[pltpu.VMEM((rows + 8, 3 * W), F32),
                        pltpu.VMEM((GDN_HEADS, GDN_HEAD_DIM, GDN_HEAD_DIM), F32)],
        compiler_params=pltpu.CompilerParams(dimension_semantics=("arbitrary", "arbitrary"),
                                             vmem_limit_bytes=V7X_VMEM_LIMIT_BYTES),
        name="gdn",
    )(pg, pg, pab, conv_w, alog_vec, dtb_vec, out_gain)


def _moba_prep_kernel(p_ref, qg_ref, kg_ref, cg_ref, qt_ref, kn_ref, vt_ref, cqt_ref, sel_ref, km_ref):
    H, Dh, Wd = ATT_HEADS, ATT_HEAD_DIM, ATT_WIDTH
    nb = km_ref.shape[0]
    i = pl.program_id(1)

    @pl.when(i == 0)
    def _():
        km_ref[...] = jnp.zeros_like(km_ref)

    block_ones = _head_block_ones(Wd, Dh)
    qn = _head_rms(p_ref[:, 0:Wd], qg_ref[...], block_ones, Dh)
    kn = _head_rms(p_ref[:, Wd:2 * Wd], kg_ref[...], block_ones, Dh)
    cqn = _head_rms(p_ref[:, 3 * Wd:4 * Wd], cg_ref[...], block_ones, Dh)
    scale = Dh ** -0.5 * LOG2_E
    qt_ref[...] = (qn * scale).T.astype(BF16)
    cqt_ref[...] = (cqn * scale).T.astype(BF16)
    kn_ref[...] = kn.astype(BF16)
    vt_ref[...] = p_ref[:, 2 * Wd:3 * Wd].T.astype(BF16)

    km = km_ref[...]
    blk = lax.broadcasted_iota(jnp.int32, (nb, Wd), 0)
    lane_head = lax.broadcasted_iota(jnp.int32, (nb, Wd), 1) // Dh
    jdx = lax.broadcasted_iota(jnp.int32, (nb, MOBA_BLOCK), 0)
    past = jdx < i
    for h in range(H):
        gate = _dot_nt(jnp.where(lane_head == h, km, 0.0), qn, HIGHEST)
        gate = jnp.where(past, gate, -jnp.inf)
        rank = jnp.zeros(gate.shape, F32)
        for jp in range(nb):
            row = gate[jp:jp + 1, :]
            ahead = (row > gate) | ((row == gate) & (jp < jdx))
            rank = rank + ahead.astype(F32)
        sel_ref[h] = jnp.where((rank < MOBA_TOPK) & past, 0.0, MASKED)
    km_ref[...] = jnp.where(blk == i, jnp.mean(kn, axis=0, keepdims=True), km)


def _moba_prep(patt, qg, kg, cg, batch, seq):
    n = patt.shape[0]
    nb = seq // MOBA_BLOCK
    Wd, H, BLK = ATT_WIDTH, ATT_HEADS, MOBA_BLOCK
    return pl.pallas_call(
        _moba_prep_kernel,
        grid=(batch, nb),
        in_specs=[pl.BlockSpec((BLK, 4 * Wd), lambda b, i: (b * nb + i, 0)),
                  _resident(), _resident(), _resident()],
        out_specs=[pl.BlockSpec((None, Wd, BLK), lambda b, i: (b, 0, i)),
                   pl.BlockSpec((BLK, Wd), lambda b, i: (b * nb + i, 0)),
                   pl.BlockSpec((None, None, Wd, BLK), lambda b, i: (b, i, 0, 0)),
                   pl.BlockSpec((None, Wd, BLK), lambda b, i: (b, 0, i)),
                   pl.BlockSpec((None, H, nb, BLK), lambda b, i: (b, 0, 0, i))],
        out_shape=[jax.ShapeDtypeStruct((batch, Wd, seq), BF16),
                   jax.ShapeDtypeStruct((n, Wd), BF16),
                   jax.ShapeDtypeStruct((batch, nb, Wd, BLK), BF16),
                   jax.ShapeDtypeStruct((batch, Wd, seq), BF16),
                   jax.ShapeDtypeStruct((batch, H, nb, seq), F32)],
        scratch_shapes=[pltpu.VMEM((nb, Wd), F32)],
        compiler_params=pltpu.CompilerParams(dimension_semantics=("arbitrary", "arbitrary")),
        name="moba_prep",
    )(patt, qg, kg, cg)


def _alibi_slope(h):
    return 2.0 ** (-8.0 * (h + 1) / ATT_HEADS)


def _moba_attn_kernel(qt_ref, cqt_ref, k_ref, vt_ref, sel_ref, km_ref, vmt_ref, y_ref,
                      m_ref, l_ref, acc_ref, s_ref, alibi_ref):
    H, Dh, BLK = ATT_HEADS, ATT_HEAD_DIM, MOBA_BLOCK
    PAIR = 2 * Dh
    i = pl.program_id(1)
    pair_row_head = lax.broadcasted_iota(jnp.int32, (PAIR, BLK), 0) // Dh

    @pl.when((pl.program_id(0) == 0) & (i == 0))
    def _():
        key_minus_query = (lax.broadcasted_iota(jnp.int32, (BLK, BLK), 0)
                           - lax.broadcasted_iota(jnp.int32, (BLK, BLK), 1)).astype(F32)
        for h in range(H):
            alibi_ref[h] = (_alibi_slope(h) * LOG2_E) * key_minus_query

    def pair_lanes(h):
        return slice((h // 2) * PAIR, (h // 2 + 1) * PAIR)

    def head_queries(src_ref, h):
        qp = src_ref[pair_lanes(h), :]
        return jnp.where(pair_row_head == h % 2, qp, jnp.zeros_like(qp))

    q_heads = [head_queries(qt_ref, h) for h in range(H)]

    def raw_scores(blk):
        r0 = pl.multiple_of(blk * BLK, BLK)
        return [_dot(k_ref[pl.ds(r0, BLK), pair_lanes(h)], q_heads[h]) for h in range(H)]

    def head_rows(h):
        return slice(h * Dh, (h + 1) * Dh)

    own_scores = raw_scores(i)
    mem_scores = [_dot(km_ref[:, pair_lanes(h)], head_queries(cqt_ref, h)) for h in range(H)]
    first_past = raw_scores(0)
    own_p, mem_p, mem_l = [], [], []
    for h in range(H):
        alibi = alibi_ref[h]
        s_t = jnp.where(alibi <= 0.0, own_scores[h] + alibi, MASKED)
        m = jnp.max(s_t, axis=0, keepdims=True)
        p = jnp.exp2(s_t - m)
        m_ref[h] = m
        l_ref[h] = jnp.sum(p, axis=0, keepdims=True)
        own_p.append(p.astype(BF16))
    for h in range(H):
        p = jnp.exp2(mem_scores[h] - jnp.max(mem_scores[h], axis=0, keepdims=True))
        mem_l.append(jnp.sum(p, axis=0, keepdims=True))
        mem_p.append(p.astype(BF16))
    for h in range(H):
        acc_ref[h] = _dot(vt_ref[i, head_rows(h), :], own_p[h])
        s_ref[0, h] = first_past[h]
    mem_out = [_dot(vmt_ref[head_rows(h), :], mem_p[h]) / mem_l[h] for h in range(H)]
    y_ref[:, H * Dh:2 * H * Dh] = jnp.concatenate(mem_out, axis=0).T.astype(BF16)

    def block_step(j, slot, nxt):
        next_scores = None if nxt is None else raw_scores(nxt)
        blocks_back = (i - j).astype(F32) * BLK
        for h in range(H):
            bias_row = sel_ref[h, pl.ds(j, 1), :] - (_alibi_slope(h) * LOG2_E) * blocks_back
            s_t = s_ref[slot, h] + alibi_ref[h] + bias_row
            m_old = m_ref[h]
            m_new = jnp.maximum(m_old, jnp.max(s_t, axis=0, keepdims=True))
            p = jnp.exp2(s_t - m_new)
            alpha = jnp.exp2(m_old - m_new)
            l_ref[h] = alpha * l_ref[h] + jnp.sum(p, axis=0, keepdims=True)
            acc_ref[h] = alpha * acc_ref[h] + _dot(vt_ref[j, head_rows(h), :], p.astype(BF16))
            m_ref[h] = m_new
        if next_scores is not None:
            for h in range(H):
                s_ref[1 - slot, h] = next_scores[h]

    last_past = jnp.maximum(i - 1, 0)

    def block_pair(jj, carry):
        j0 = 2 * jj
        block_step(j0, 0, jnp.minimum(j0 + 1, last_past))
        block_step(j0 + 1, 1, jnp.minimum(j0 + 2, last_past))
        return carry

    lax.fori_loop(0, i // 2, block_pair, 0)

    @pl.when(i % 2 == 1)
    def _():
        block_step(i - 1, 0, None)

    moba_out = [acc_ref[h] / l_ref[h] for h in range(H)]
    y_ref[:, 0:H * Dh] = jnp.concatenate(moba_out, axis=0).T.astype(BF16)


def _moba_attn(qt, cqt, kn, vt, sel, kmem, vmem_t, batch, seq):
    nb = seq // MOBA_BLOCK
    Wd, H, BLK = ATT_WIDTH, ATT_HEADS, MOBA_BLOCK
    n_mem = kmem.shape[1]
    return pl.pallas_call(
        _moba_attn_kernel,
        grid=(batch, nb),
        in_specs=[pl.BlockSpec((None, Wd, BLK), lambda b, i: (b, 0, i)),
                  pl.BlockSpec((None, Wd, BLK), lambda b, i: (b, 0, i)),
                  pl.BlockSpec((seq, Wd), lambda b, i: (b, 0)),
                  pl.BlockSpec((None, nb, Wd, BLK), lambda b, i: (b, 0, 0, 0)),
                  pl.BlockSpec((None, H, nb, BLK), lambda b, i: (b, 0, 0, i)),
                  pl.BlockSpec((None, n_mem, Wd), lambda b, i: (b, 0, 0)),
                  pl.BlockSpec((None, Wd, n_mem), lambda b, i: (b, 0, 0))],
        out_specs=pl.BlockSpec((BLK, 2 * Wd), lambda b, i: (b * nb + i, 0)),
        out_shape=jax.ShapeDtypeStruct((batch * seq, 2 * Wd), BF16),
        scratch_shapes=[pltpu.VMEM((H, 1, BLK), F32), pltpu.VMEM((H, 1, BLK), F32),
                        pltpu.VMEM((H, ATT_HEAD_DIM, BLK), F32),
                        pltpu.VMEM((2, H, BLK, BLK), F32),
                        pltpu.VMEM((H, BLK, BLK), F32)],
        compiler_params=pltpu.CompilerParams(dimension_semantics=("arbitrary", "arbitrary"),
                                             vmem_limit_bytes=V7X_VMEM_LIMIT_BYTES),
        name="moba_attn",
    )(qt, cqt, kn, vt, sel, kmem, vmem_t)


def _lane_vector(values, width):
    return jnp.zeros((1, width), F32).at[0, :values.shape[0]].set(values.astype(F32))


def _layer(x2d, mem, batch, seq, ffn1_norm, ffn1_w_gate, ffn1_w_up, ffn1_w_down, mix_norm, w_in,
           gdn_conv_w, gdn_a_log, gdn_dt_bias, gdn_out_norm, moba_q_norm, moba_k_norm,
           mem_norm, w_mem_kv, mem_q_norm, mem_k_norm, w_out, ffn2_norm, ffn2_w_gate, ffn2_w_up, ffn2_w_down):
    W, H = GDN_WIDTH, GDN_HEADS
    row = lambda v: v.reshape(1, -1).astype(F32)
    tile_heads = lambda v: jnp.tile(v.astype(F32), ATT_HEADS).reshape(1, -1)
    ab0 = 4 * W
    att0 = ab0 + 2 * H
    w_gdn = w_in[:, :ab0].astype(BF16)
    ab_width = 128
    w_ab = jnp.zeros((w_in.shape[0], ab_width), F32).at[:, :2 * H].set(w_in[:, ab0:att0]).astype(BF16)
    w_att = w_in[:, att0:].astype(BF16)

    x1, pg, pab, patt = _ffn1_inproj(
        x2d, row(ffn1_norm), ffn1_w_gate.astype(BF16), ffn1_w_up.astype(BF16), ffn1_w_down.astype(BF16),
        row(mix_norm), w_gdn, w_ab, w_att)

    y_gdn = _gdn(pg, pab, gdn_conv_w.astype(F32), _lane_vector(gdn_a_log, ab_width),
                 _lane_vector(gdn_dt_bias, ab_width), row(gdn_out_norm), batch, seq)

    kmem, vmem_t = _mem_kv(mem, row(mem_norm), w_mem_kv.astype(BF16), tile_heads(mem_k_norm))
    qt, kn, vt, cqt, sel = _moba_prep(patt, tile_heads(moba_q_norm), tile_heads(moba_k_norm),
                                      tile_heads(mem_q_norm), batch, seq)
    y_att = _moba_attn(qt, cqt, kn, vt, sel, kmem, vmem_t, batch, seq)

    w_out_b = w_out.astype(BF16)
    return _outproj_ffn2(x1, y_gdn, y_att, w_out_b[:W], w_out_b[W:], row(ffn2_norm),
                         ffn2_w_gate.astype(BF16), ffn2_w_up.astype(BF16), ffn2_w_down.astype(BF16))


def kernel(x, mem, ffn1_norm, ffn1_w_gate, ffn1_w_up, ffn1_w_down, mix_norm, w_in, gdn_conv_w, gdn_a_log,
           gdn_dt_bias, gdn_out_norm, moba_q_norm, moba_k_norm, mem_norm, w_mem_kv, mem_q_norm, mem_k_norm,
           w_out, ffn2_norm, ffn2_w_gate, ffn2_w_up, ffn2_w_down):
    batch, seq, d = x.shape
    assert seq % MOBA_BLOCK == 0 and seq % GDN_ROWS == 0 and (batch * seq) % FFN_ROWS == 0
    depth = w_in.shape[0]
    x2d = x.reshape(batch * seq, d)
    for l in range(depth):
        x2d = _layer(x2d, mem, batch, seq, ffn1_norm[l], ffn1_w_gate[l], ffn1_w_up[l], ffn1_w_down[l],
                     mix_norm[l], w_in[l], gdn_conv_w[l], gdn_a_log[l], gdn_dt_bias[l], gdn_out_norm[l],
                     moba_q_norm[l], moba_k_norm[l], mem_norm[l], w_mem_kv[l], mem_q_norm[l], mem_k_norm[l],
                     w_out[l], ffn2_norm[l], ffn2_w_gate[l], ffn2_w_up[l], ffn2_w_down[l])
    return x2d.reshape(batch, seq, d)
```

```python
import struct

import jax
import jax.numpy as jnp
from jax import lax
from jax.experimental import pallas as pl
from jax.experimental.pallas import tpu as pltpu

F32 = jnp.float32
BF16 = jnp.bfloat16
HIGHEST = lax.Precision.HIGHEST

NORM_EPS = 1e-6
D_FF = 2816
GDN_HEADS = 4
GDN_HEAD_DIM = 128
GDN_WIDTH = GDN_HEADS * GDN_HEAD_DIM
GDN_CONV = 4
GDN_CHUNK = 64
GDN_GROUP = 2 * GDN_CHUNK
ATT_HEADS = 4
ATT_HEAD_DIM = 64
ATT_WIDTH = ATT_HEADS * ATT_HEAD_DIM
MOBA_BLOCK = 256
MOBA_TOPK = 3
MASKED = -1e30
LOG2_E = 1.4426950408889634
BIAS_TERMS = 3

V7X_VMEM_LIMIT_BYTES = 56 * 1024 * 1024
V7X_LANES = 128
V7X_BF16_SUBLANES = 16
FFN_ROWS = 256
GDN_ROWS = 256


def _dot(a, b, precision=None):
    return jnp.dot(a, b, preferred_element_type=F32, precision=precision)


def _dot_nt(a, b, precision=None):
    return lax.dot_general(a, b, (((1,), (1,)), ((), ())), preferred_element_type=F32,
                           precision=precision)


def _rms_rows(x, gain):
    return x * lax.rsqrt(jnp.mean(x * x, axis=-1, keepdims=True) + NORM_EPS) * gain


def _silu(x):
    return x * jax.nn.sigmoid(x)


def _swiglu(x, gain, wg_ref, wu_ref, wd_ref):
    h = _rms_rows(x, gain).astype(BF16)
    g = _dot(h, wg_ref[...])
    u = _dot(h, wu_ref[...])
    return _dot((_silu(g) * u).astype(BF16), wd_ref[...])


def _resident():
    return pl.BlockSpec(memory_space=pltpu.VMEM)


def _bf16_terms(x):
    terms, rest = [], x
    for _ in range(BIAS_TERMS):
        t = rest.astype(BF16).astype(F32)
        terms.append(t)
        rest = rest - t
    return terms


def _bf16_terms_const(value):
    def f32(v):
        return struct.unpack("<f", struct.pack("<f", v))[0]

    def bf16_round(v):
        bits = struct.unpack("<I", struct.pack("<f", v))[0]
        bits = (bits + 0x7FFF + ((bits >> 16) & 1)) & 0xFFFF0000
        return struct.unpack("<f", struct.pack("<I", bits))[0]

    terms, rest = [], f32(value)
    for _ in range(BIAS_TERMS):
        t = bf16_round(rest)
        terms.append(t)
        rest = f32(rest - t)
    return terms


def _ffn1_inproj_kernel(x_ref, n1_ref, wg_ref, wu_ref, wd_ref, n2_ref, wgdn_ref, wab_ref, watt_ref,
                        x1_ref, pg_ref, pab_ref, patt_ref):
    x = x_ref[...]
    x1 = x + 0.5 * _swiglu(x, n1_ref[...], wg_ref, wu_ref, wd_ref)
    x1_ref[...] = x1
    h = _rms_rows(x1, n2_ref[...]).astype(BF16)
    pg_ref[...] = _dot(h, wgdn_ref[...])
    pab_ref[...] = _dot(h, wab_ref[...])
    patt_ref[...] = _dot(h, watt_ref[...])


def _ffn1_inproj(x2d, n1, wg, wu, wd, n2, wgdn, wab, watt):
    n, d = x2d.shape
    rows = FFN_ROWS
    row_spec = lambda w: pl.BlockSpec((rows, w), lambda i: (i, 0))
    return pl.pallas_call(
        _ffn1_inproj_kernel,
        grid=(n // rows,),
        in_specs=[row_spec(d)] + [_resident()] * 8,
        out_specs=[row_spec(d), row_spec(wgdn.shape[1]), row_spec(wab.shape[1]), row_spec(watt.shape[1])],
        out_shape=[jax.ShapeDtypeStruct((n, d), F32),
                   jax.ShapeDtypeStruct((n, wgdn.shape[1]), F32),
                   jax.ShapeDtypeStruct((n, wab.shape[1]), F32),
                   jax.ShapeDtypeStruct((n, watt.shape[1]), F32)],
        compiler_params=pltpu.CompilerParams(dimension_semantics=("arbitrary",),
                                             vmem_limit_bytes=V7X_VMEM_LIMIT_BYTES),
        name="ffn1_inproj",
    )(x2d, n1, wg, wu, wd, n2, wgdn, wab, watt)


def _outproj_ffn2_kernel(x1_ref, yg_ref, ya_ref, wog_ref, woa_ref, n_ref, wg_ref, wu_ref, wd_ref, o_ref):
    x2 = x1_ref[...] + _dot(yg_ref[...], wog_ref[...]) + _dot(ya_ref[...], woa_ref[...])
    o_ref[...] = x2 + 0.5 * _swiglu(x2, n_ref[...], wg_ref, wu_ref, wd_ref)


def _outproj_ffn2(x1, yg, ya, wog, woa, nrm, wg, wu, wd):
    n, d = x1.shape
    rows = FFN_ROWS
    row_spec = lambda w: pl.BlockSpec((rows, w), lambda i: (i, 0))
    return pl.pallas_call(
        _outproj_ffn2_kernel,
        grid=(n // rows,),
        in_specs=[row_spec(d), row_spec(yg.shape[1]), row_spec(ya.shape[1])] + [_resident()] * 6,
        out_specs=row_spec(d),
        out_shape=jax.ShapeDtypeStruct((n, d), F32),
        compiler_params=pltpu.CompilerParams(dimension_semantics=("arbitrary",),
                                             vmem_limit_bytes=V7X_VMEM_LIMIT_BYTES),
        name="outproj_ffn2",
    )(x1, yg, ya, wog, woa, nrm, wg, wu, wd)


def _head_block_ones(width, head_dim):
    r = lax.broadcasted_iota(jnp.int32, (width, width), 0) // head_dim
    c = lax.broadcasted_iota(jnp.int32, (width, width), 1) // head_dim
    return jnp.where(r == c, 1.0, 0.0).astype(BF16)


def _head_rms(x, gain, block_ones, head_dim):
    sq = x * x
    hi = sq.astype(BF16)
    lo = (sq - hi.astype(F32)).astype(BF16)
    ss = _dot(hi, block_ones) + _dot(lo, block_ones)
    return x * lax.rsqrt(ss * (1.0 / head_dim) + NORM_EPS) * gain


def _mem_kv_kernel(mem_ref, nrm_ref, wkv_ref, kg_ref, k_ref, vt_ref):
    h = _rms_rows(mem_ref[...], nrm_ref[...]).astype(BF16)
    kv = _dot(h, wkv_ref[...])
    k = _head_rms(kv[:, :ATT_WIDTH], kg_ref[...], _head_block_ones(ATT_WIDTH, ATT_HEAD_DIM), ATT_HEAD_DIM)
    k_ref[...] = k.astype(BF16)
    vt_ref[...] = kv[:, ATT_WIDTH:].T.astype(BF16)


def _mem_kv(mem, nrm, wkv, kgain):
    b, m, d = mem.shape
    return pl.pallas_call(
        _mem_kv_kernel,
        grid=(b,),
        in_specs=[pl.BlockSpec((None, m, d), lambda i: (i, 0, 0)), _resident(), _resident(), _resident()],
        out_specs=[pl.BlockSpec((None, m, ATT_WIDTH), lambda i: (i, 0, 0)),
                   pl.BlockSpec((None, ATT_WIDTH, m), lambda i: (i, 0, 0))],
        out_shape=[jax.ShapeDtypeStruct((b, m, ATT_WIDTH), BF16),
                   jax.ShapeDtypeStruct((b, ATT_WIDTH, m), BF16)],
        compiler_params=pltpu.CompilerParams(dimension_semantics=("arbitrary",)),
        name="mem_kv",
    )(mem, nrm, wkv, kgain)


def _softplus(x):
    return jnp.maximum(x, 0.0) + jnp.log(1.0 + jnp.exp(-jnp.abs(x)))


def _unit_lower_inverses(a_list, eye):
    x16 = [(-a).astype(BF16) for a in a_list]
    t = [eye - a for a in a_list]
    for _ in range(5):
        x16 = [_dot(xb, xb).astype(BF16) for xb in x16]
        t = [ti + _dot(ti.astype(BF16), xb) for ti, xb in zip(t, x16)]
    t16 = [ti.astype(BF16) for ti in t]
    a_hi = [a.astype(BF16) for a in a_list]
    a_lo = [(a - hi.astype(F32)).astype(BF16) for a, hi in zip(a_list, a_hi)]
    resid = [(eye - tb.astype(F32) - _dot(hi, tb) - _dot(lo, tb)).astype(BF16)
             for tb, hi, lo in zip(t16, a_hi, a_lo)]
    return [tb.astype(F32) + _dot(tb, r) for tb, r in zip(t16, resid)]


def _gdn_kernel(qkv_ref, z_ref, ab_ref, cw_ref, alog_ref, dtb_ref, og_ref, y_ref, xbuf_ref, s_ref):
    rows = z_ref.shape[0]
    H, Dh, C, W = GDN_HEADS, GDN_HEAD_DIM, GDN_CHUNK, GDN_WIDTH
    G = GDN_GROUP
    n_chunks = rows // C
    n_groups = rows // G
    t = pl.program_id(1)

    @pl.when(t == 0)
    def _():
        xbuf_ref[0:8, :] = jnp.zeros((8, 3 * W), F32)
        s_ref[...] = jnp.zeros_like(s_ref)

    xbuf_ref[8:, :] = qkv_ref[...]
    conv = cw_ref[0:1, :] * xbuf_ref[5:5 + rows, :]
    for j in range(1, GDN_CONV):
        conv = conv + cw_ref[j:j + 1, :] * xbuf_ref[5 + j:5 + j + rows, :]
    xbuf_ref[0:8, :] = xbuf_ref[rows:rows + 8, :]
    act = _silu(conv)

    ab = ab_ref[...]
    log_decay = -jnp.exp(alog_ref[...]) * _softplus(ab + dtb_ref[...])
    beta_all = jax.nn.sigmoid(ab)
    rt = lax.broadcasted_iota(jnp.int32, (rows, rows), 0)
    ct = lax.broadcasted_iota(jnp.int32, (rows, rows), 1)
    chunk_lower = (((rt // C) == (ct // C)) & (rt >= ct)).astype(F32)
    gc_all = _dot(chunk_lower, log_decay, HIGHEST)
    gc_rows = gc_all.T

    ri = lax.broadcasted_iota(jnp.int32, (G, G), 0)
    ci = lax.broadcasted_iota(jnp.int32, (G, G), 1)
    same_chunk = (ri // C) == (ci // C)
    lower = same_chunk & (ri >= ci)
    strict = same_chunk & (ri > ci)
    eye = (ri == ci).astype(F32)

    units = [(h, g) for h in range(H) for g in range(n_groups)]
    q_h, k_h, v_h, gcol_h, beta_h = [], [], [], [], []
    for h in range(H):
        qh = act[:, h * Dh:(h + 1) * Dh]
        kh = act[:, W + h * Dh:W + (h + 1) * Dh]
        v_h.append(act[:, 2 * W + h * Dh:2 * W + (h + 1) * Dh])
        q_h.append(qh * (lax.rsqrt(jnp.sum(qh * qh, axis=-1, keepdims=True) + NORM_EPS) * Dh ** -0.5))
        k_h.append(kh * lax.rsqrt(jnp.sum(kh * kh, axis=-1, keepdims=True) + NORM_EPS))
        gcol_h.append(jnp.broadcast_to(gc_all[:, h:h + 1], (rows, Dh)))
        beta_h.append(jnp.broadcast_to(beta_all[:, H + h:H + h + 1], (rows, Dh)))

    def unit_rows(x, g):
        return x[g * G:(g + 1) * G]

    k_u = [unit_rows(k_h[h], g) for h, g in units]
    gcol_u = [unit_rows(gcol_h[h], g) for h, g in units]
    kb_u = [k * unit_rows(beta_h[h], g) for k, (h, g) in zip(k_u, units)]
    k16_u = [k.astype(BF16) for k in k_u]
    q_u = [unit_rows(q_h[h], g) for h, g in units]
    kk_u = [_dot_nt(kb.astype(BF16), k16) for kb, k16 in zip(kb_u, k16_u)]
    qk_u = [_dot_nt(q.astype(BF16), k16) for q, k16 in zip(q_u, k16_u)]
    decay_u = []
    for gcol, (h, g) in zip(gcol_u, units):
        diff = gcol - gc_rows[h:h + 1, g * G:(g + 1) * G]
        decay_u.append(jnp.where(lower, jnp.exp(jnp.where(lower, diff, 0.0)), 0.0))
    a_u = [jnp.where(strict, kk * decay, 0.0) for kk, decay in zip(kk_u, decay_u)]
    qk16_u = [(qk * decay).astype(BF16) for qk, decay in zip(qk_u, decay_u)]
    tinv_u = _unit_lower_inverses(a_u, eye)
    egc_u = [jnp.exp(gcol) for gcol in gcol_u]
    rhs_u = [jnp.concatenate([unit_rows(v_h[h], g) * unit_rows(beta_h[h], g), kb * egc], axis=1).astype(BF16)
             for kb, egc, (h, g) in zip(kb_u, egc_u, units)]
    uw_u = [_dot(tinv.astype(BF16), rhs) for tinv, rhs in zip(tinv_u, rhs_u)]
    w16_u = [uw[:, Dh:].astype(BF16) for uw in uw_u]
    qe16_u = [(q * egc).astype(BF16) for q, egc in zip(q_u, egc_u)]

    def chunk_operands(h, c):
        u = h * n_groups + (c * C) // G
        r = slice((c * C) % G, (c * C) % G + C)
        g_last = gcol_u[u][r.stop - 1:r.stop, :]
        return (uw_u[u][r, :Dh],
                jnp.concatenate([w16_u[u][r], qe16_u[u][r]], axis=0),
                qk16_u[u][r, r],
                (k_u[u][r] * jnp.exp(g_last - gcol_u[u][r])).T.astype(BF16),
                jnp.exp(g_last))

    ops = [[chunk_operands(h, c) for c in range(n_chunks)] for h in range(H)]

    state = [s_ref[h] for h in range(H)]
    o_l = [[] for _ in range(H)]
    for c in range(n_chunks):
        ws = [_dot(ops[h][c][1], state[h].astype(BF16)) for h in range(H)]
        v_new = [(ops[h][c][0] - ws[h][:C]).astype(BF16) for h in range(H)]
        state = [state[h] * ops[h][c][4] + _dot(ops[h][c][3], v_new[h]) for h in range(H)]
        for h in range(H):
            o_l[h].append(ws[h][C:] + _dot(ops[h][c][2], v_new[h]))
    for h in range(H):
        s_ref[h] = state[h]

    z = z_ref[...]
    for h in range(H):
        cols = slice(h * Dh, (h + 1) * Dh)
        o = jnp.concatenate(o_l[h], axis=0)
        y_ref[:, cols] = (_rms_rows(o, og_ref[...]) * _silu(z[:, cols])).astype(BF16)


def _gdn(pg, pab, conv_w, alog_vec, dtb_vec, out_gain, batch, seq):
    n = pg.shape[0]
    rows = GDN_ROWS
    tiles = seq // rows
    W = GDN_WIDTH
    return pl.pallas_call(
        _gdn_kernel,
        grid=(batch, tiles),
        in_specs=[pl.BlockSpec((rows, 3 * W), lambda b, t: (b * tiles + t, 0)),
                  pl.BlockSpec((rows, W), lambda b, t: (b * tiles + t, 3)),
                  pl.BlockSpec((rows, pab.shape[1]), lambda b, t: (b * tiles + t, 0)),
                  _resident(), _resident(), _resident(), _resident()],
        out_specs=pl.BlockSpec((rows, W), lambda b, t: (b * tiles + t, 0)),
        out_shape=jax.ShapeDtypeStruct((n, W), BF16),
        scratch_shapes=[pltpu.VMEM((rows + 8, 3 * W), F32),
                        pltpu.VMEM((GDN_HEADS, GDN_HEAD_DIM, GDN_HEAD_DIM), F32)],
        compiler_params=pltpu.CompilerParams(dimension_semantics=("arbitrary", "arbitrary"),
                                             vmem_limit_bytes=V7X_VMEM_LIMIT_BYTES),
        name="gdn",
    )(pg, pg, pab, conv_w, alog_vec, dtb_vec, out_gain)


def _moba_prep_kernel(p_ref, qg_ref, kg_ref, cg_ref, qt_ref, kn_ref, vt_ref, cqt_ref, sel_ref, km_ref):
    H, Dh, Wd = ATT_HEADS, ATT_HEAD_DIM, ATT_WIDTH
    nb = km_ref.shape[0]
    i = pl.program_id(1)

    @pl.when(i == 0)
    def _():
        km_ref[...] = jnp.zeros_like(km_ref)

    block_ones = _head_block_ones(Wd, Dh)
    qn = _head_rms(p_ref[:, 0:Wd], qg_ref[...], block_ones, Dh)
    kn = _head_rms(p_ref[:, Wd:2 * Wd], kg_ref[...], block_ones, Dh)
    cqn = _head_rms(p_ref[:, 3 * Wd:4 * Wd], cg_ref[...], block_ones, Dh)
    scale = Dh ** -0.5 * LOG2_E
    qt_ref[...] = (qn * scale).T.astype(BF16)
    cqt_ref[...] = (cqn * scale).T.astype(BF16)
    kn_ref[...] = kn.astype(BF16)
    vt_ref[...] = p_ref[:, 2 * Wd:3 * Wd].T.astype(BF16)

    km = km_ref[...]
    blk = lax.broadcasted_iota(jnp.int32, (nb, Wd), 0)
    lane_head = lax.broadcasted_iota(jnp.int32, (nb, Wd), 1) // Dh
    km_heads = jnp.concatenate([jnp.where(lane_head == h, km, 0.0) for h in range(H)], axis=0)
    gate_all = _dot_nt(km_heads, qn, HIGHEST)
    jdx = lax.broadcasted_iota(jnp.int32, (nb, MOBA_BLOCK), 0)
    jdx_f = jdx.astype(F32)
    past = jdx < i
    for h in range(H):
        gate = jnp.where(past, gate_all[h * nb:(h + 1) * nb], -jnp.inf)
        chosen = jnp.zeros(gate.shape, jnp.bool_)
        for _ in range(MOBA_TOPK):
            top = jnp.max(gate, axis=0, keepdims=True)
            first = jnp.min(jnp.where(gate == top, jdx_f, float(nb)), axis=0, keepdims=True)
            pick = jdx_f == first
            chosen = chosen | pick
            gate = jnp.where(pick, -jnp.inf, gate)
        sel_ref[h] = jnp.where(chosen & past, 0.0, MASKED)
    km_ref[...] = jnp.where(blk == i, jnp.mean(kn, axis=0, keepdims=True), km)


def _moba_prep(patt, qg, kg, cg, batch, seq):
    n = patt.shape[0]
    nb = seq // MOBA_BLOCK
    Wd, H, BLK = ATT_WIDTH, ATT_HEADS, MOBA_BLOCK
    return pl.pallas_call(
        _moba_prep_kernel,
        grid=(batch, nb),
        in_specs=[pl.BlockSpec((BLK, 4 * Wd), lambda b, i: (b * nb + i, 0)),
                  _resident(), _resident(), _resident()],
        out_specs=[pl.BlockSpec((None, Wd, BLK), lambda b, i: (b, 0, i)),
                   pl.BlockSpec((BLK, Wd), lambda b, i: (b * nb + i, 0)),
                   pl.BlockSpec((None, None, Wd, BLK), lambda b, i: (b, i, 0, 0)),
                   pl.BlockSpec((None, Wd, BLK), lambda b, i: (b, 0, i)),
                   pl.BlockSpec((None, H, nb, BLK), lambda b, i: (b, 0, 0, i))],
        out_shape=[jax.ShapeDtypeStruct((batch, Wd, seq), BF16),
                   jax.ShapeDtypeStruct((n, Wd), BF16),
                   jax.ShapeDtypeStruct((batch, nb, Wd, BLK), BF16),
                   jax.ShapeDtypeStruct((batch, Wd, seq), BF16),
                   jax.ShapeDtypeStruct((batch, H, nb, seq), F32)],
        scratch_shapes=[pltpu.VMEM((nb, Wd), F32)],
        compiler_params=pltpu.CompilerParams(dimension_semantics=("arbitrary", "arbitrary")),
        name="moba_prep",
    )(patt, qg, kg, cg)


def _alibi_slope(h):
    return 2.0 ** (-8.0 * (h + 1) / ATT_HEADS)


def _moba_attn_kernel(qt_ref, cqt_ref, k_ref, vt_ref, sel_ref, km_ref, vmt_ref, y_ref,
                      m_ref, l_ref, acc_ref, s_ref, kaug_ref, rhs_ref):
    H, Dh, BLK = ATT_HEADS, ATT_HEAD_DIM, MOBA_BLOCK
    PAIR = 2 * Dh
    AUG = V7X_BF16_SUBLANES
    i = pl.program_id(1)
    pair_row_head = lax.broadcasted_iota(jnp.int32, (PAIR, BLK), 0) // Dh

    @pl.when((pl.program_id(0) == 0) & (i == 0))
    def _():
        lane = lax.broadcasted_iota(jnp.int32, (BLK, V7X_LANES), 1)
        key_pos = lax.broadcasted_iota(jnp.int32, (BLK, V7X_LANES), 0).astype(F32)
        kaug_ref[...] = jnp.where(lane < BIAS_TERMS, key_pos,
                                  jnp.where(lane < 2 * BIAS_TERMS, 1.0, 0.0)).astype(BF16)
        rhs_ref[...] = jnp.zeros_like(rhs_ref)

    def pair_lanes(h):
        return slice((h // 2) * PAIR, (h // 2 + 1) * PAIR)

    def head_rows(h):
        return slice(h * Dh, (h + 1) * Dh)

    def head_queries(src_ref, h):
        qp = src_ref[pair_lanes(h), :]
        return jnp.where(pair_row_head == h % 2, qp, jnp.zeros_like(qp))

    for h in range(H):
        rhs_ref[h, 0:PAIR, :] = head_queries(qt_ref, h)

    aug_row = lax.broadcasted_iota(jnp.int32, (AUG, BLK), 0)
    query_pos = lax.broadcasted_iota(jnp.int32, (1, BLK), 1).astype(F32)

    def scores(blk, own):
        r0 = pl.multiple_of(blk * BLK, BLK)
        distance = (i - blk).astype(F32) * BLK
        keys_aug = kaug_ref[...]
        out = []
        for h in range(H):
            c = _alibi_slope(h) * LOG2_E
            row_bias = -c * (query_pos + distance)
            if not own:
                row_bias = row_bias + sel_ref[h, pl.ds(blk, 1), :]
            rows = _bf16_terms_const(c) + _bf16_terms(row_bias)
            aug = jnp.zeros((AUG, BLK), F32)
            for n, val in enumerate(rows):
                aug = jnp.where(aug_row == n, val, aug)
            rhs_ref[h, PAIR:PAIR + AUG, :] = aug.astype(BF16)
            keys = jnp.concatenate([k_ref[pl.ds(r0, BLK), pair_lanes(h)], keys_aug], axis=1)
            out.append(_dot(keys, rhs_ref[h]))
        return out

    own_scores = scores(i, True)
    mem_scores = [_dot(km_ref[:, pair_lanes(h)], head_queries(cqt_ref, h)) for h in range(H)]
    first_past = scores(0, False)
    causal = (lax.broadcasted_iota(jnp.int32, (BLK, BLK), 0) <= lax.broadcasted_iota(jnp.int32, (BLK, BLK), 1))
    own_p, mem_p, mem_l = [], [], []
    for h in range(H):
        s_t = jnp.where(causal, own_scores[h], MASKED)
        m = jnp.max(s_t, axis=0, keepdims=True)
        p = jnp.exp2(s_t - m)
        m_ref[h] = m
        l_ref[h] = jnp.sum(p, axis=0, keepdims=True)
        own_p.append(p.astype(BF16))
    for h in range(H):
        p = jnp.exp2(mem_scores[h] - jnp.max(mem_scores[h], axis=0, keepdims=True))
        mem_l.append(jnp.sum(p, axis=0, keepdims=True))
        mem_p.append(p.astype(BF16))
    for h in range(H):
        acc_ref[h] = _dot(vt_ref[i, head_rows(h), :], own_p[h])
        s_ref[0, h] = first_past[h]
    mem_out = [_dot(vmt_ref[head_rows(h), :], mem_p[h]) / mem_l[h] for h in range(H)]
    y_ref[:, H * Dh:2 * H * Dh] = jnp.concatenate(mem_out, axis=0).T.astype(BF16)

    def block_step(j, slot, nxt):
        next_scores = None if nxt is None else scores(nxt, False)
        for h in range(H):
            s_t = s_ref[slot, h]
            m_old = m_ref[h]
            m_new = jnp.maximum(m_old, jnp.max(s_t, axis=0, keepdims=True))
            p = jnp.exp2(s_t - m_new)
            alpha = jnp.exp2(m_old - m_new)
            l_ref[h] = alpha * l_ref[h] + jnp.sum(p, axis=0, keepdims=True)
            acc_ref[h] = alpha * acc_ref[h] + _dot(vt_ref[j, head_rows(h), :], p.astype(BF16))
            m_ref[h] = m_new
        if next_scores is not None:
            for h in range(H):
                s_ref[1 - slot, h] = next_scores[h]

    last_past = jnp.maximum(i - 1, 0)

    def block_pair(jj, carry):
        j0 = 2 * jj
        block_step(j0, 0, jnp.minimum(j0 + 1, last_past))
        block_step(j0 + 1, 1, jnp.minimum(j0 + 2, last_past))
        return carry

    lax.fori_loop(0, i // 2, block_pair, 0)

    @pl.when(i % 2 == 1)
    def _():
        block_step(i - 1, 0, None)

    moba_out = [acc_ref[h] / l_ref[h] for h in range(H)]
    y_ref[:, 0:H * Dh] = jnp.concatenate(moba_out, axis=0).T.astype(BF16)


def _moba_attn(qt, cqt, kn, vt, sel, kmem, vmem_t, batch, seq):
    nb = seq // MOBA_BLOCK
    Wd, H, BLK = ATT_WIDTH, ATT_HEADS, MOBA_BLOCK
    n_mem = kmem.shape[1]
    return pl.pallas_call(
        _moba_attn_kernel,
        grid=(batch, nb),
        in_specs=[pl.BlockSpec((None, Wd, BLK), lambda b, i: (b, 0, i)),
                  pl.BlockSpec((None, Wd, BLK), lambda b, i: (b, 0, i)),
                  pl.BlockSpec((seq, Wd), lambda b, i: (b, 0)),
                  pl.BlockSpec((None, nb, Wd, BLK), lambda b, i: (b, 0, 0, 0)),
                  pl.BlockSpec((None, H, nb, BLK), lambda b, i: (b, 0, 0, i)),
                  pl.BlockSpec((None, n_mem, Wd), lambda b, i: (b, 0, 0)),
                  pl.BlockSpec((None, Wd, n_mem), lambda b, i: (b, 0, 0))],
        out_specs=pl.BlockSpec((BLK, 2 * Wd), lambda b, i: (b * nb + i, 0)),
        out_shape=jax.ShapeDtypeStruct((batch * seq, 2 * Wd), BF16),
        scratch_shapes=[pltpu.VMEM((H, 1, BLK), F32), pltpu.VMEM((H, 1, BLK), F32),
                        pltpu.VMEM((H, ATT_HEAD_DIM, BLK), F32),
                        pltpu.VMEM((2, H, BLK, BLK), F32),
                        pltpu.VMEM((BLK, V7X_LANES), BF16),
                        pltpu.VMEM((H, 2 * ATT_HEAD_DIM + V7X_LANES, BLK), BF16)],
        compiler_params=pltpu.CompilerParams(dimension_semantics=("arbitrary", "arbitrary"),
                                             vmem_limit_bytes=V7X_VMEM_LIMIT_BYTES),
        name="moba_attn",
    )(qt, cqt, kn, vt, sel, kmem, vmem_t)


def _lane_vector(values, width):
    return jnp.zeros((1, width), F32).at[0, :values.shape[0]].set(values.astype(F32))


def _layer(x2d, mem, batch, seq, ffn1_norm, ffn1_w_gate, ffn1_w_up, ffn1_w_down, mix_norm, w_in,
           gdn_conv_w, gdn_a_log, gdn_dt_bias, gdn_out_norm, moba_q_norm, moba_k_norm,
           mem_norm, w_mem_kv, mem_q_norm, mem_k_norm, w_out, ffn2_norm, ffn2_w_gate, ffn2_w_up, ffn2_w_down):
    W, H = GDN_WIDTH, GDN_HEADS
    row = lambda v: v.reshape(1, -1).astype(F32)
    tile_heads = lambda v: jnp.tile(v.astype(F32), ATT_HEADS).reshape(1, -1)
    ab0 = 4 * W
    att0 = ab0 + 2 * H
    w_gdn = w_in[:, :ab0].astype(BF16)
    ab_width = V7X_LANES
    w_ab = jnp.zeros((w_in.shape[0], ab_width), F32).at[:, :2 * H].set(w_in[:, ab0:att0]).astype(BF16)
    w_att = w_in[:, att0:].astype(BF16)

    x1, pg, pab, patt = _ffn1_inproj(
        x2d, row(ffn1_norm), ffn1_w_gate.astype(BF16), ffn1_w_up.astype(BF16), ffn1_w_down.astype(BF16),
        row(mix_norm), w_gdn, w_ab, w_att)

    y_gdn = _gdn(pg, pab, gdn_conv_w.astype(F32), _lane_vector(gdn_a_log, ab_width),
                 _lane_vector(gdn_dt_bias, ab_width), row(gdn_out_norm), batch, seq)

    kmem, vmem_t = _mem_kv(mem, row(mem_norm), w_mem_kv.astype(BF16), tile_heads(mem_k_norm))
    qt, kn, vt, cqt, sel = _moba_prep(patt, tile_heads(moba_q_norm), tile_heads(moba_k_norm),
                                      tile_heads(mem_q_norm), batch, seq)
    y_att = _moba_attn(qt, cqt, kn, vt, sel, kmem, vmem_t, batch, seq)

    w_out_b = w_out.astype(BF16)
    return _outproj_ffn2(x1, y_gdn, y_att, w_out_b[:W], w_out_b[W:], row(ffn2_norm),
                         ffn2_w_gate.astype(BF16), ffn2_w_up.astype(BF16), ffn2_w_down.astype(BF16))


def kernel(x, mem, ffn1_norm, ffn1_w_gate, ffn1_w_up, ffn1_w_down, mix_norm, w_in, gdn_conv_w, gdn_a_log,
           gdn_dt_bias, gdn_out_norm, moba_q_norm, moba_k_norm, mem_norm, w_mem_kv, mem_q_norm, mem_k_norm,
           w_out, ffn2_norm, ffn2_w_gate, ffn2_w_up, ffn2_w_down):
    batch, seq, d = x.shape
    assert seq % MOBA_BLOCK == 0 and seq % GDN_ROWS == 0 and (batch * seq) % FFN_ROWS == 0
    depth = w_in.shape[0]
    x2d = x.reshape(batch * seq, d)
    for l in range(depth):
        x2d = _layer(x2d, mem, batch, seq, ffn1_norm[l], ffn1_w_gate[l], ffn1_w_up[l], ffn1_w_down[l],
                     mix_norm[l], w_in[l], gdn_conv_w[l], gdn_a_log[l], gdn_dt_bias[l], gdn_out_norm[l],
                     moba_q_norm[l], moba_k_norm[l], mem_norm[l], w_mem_kv[l], mem_q_norm[l], mem_k_norm[l],
                     w_out[l], ffn2_norm[l], ffn2_w_gate[l], ffn2_w_up[l], ffn2_w_down[l])
    return x2d.reshape(batch, seq, d)
```

```python
import struct

import jax
import jax.numpy as jnp
from jax import lax
from jax.experimental import pallas as pl
from jax.experimental.pallas import tpu as pltpu

F32 = jnp.float32
BF16 = jnp.bfloat16
HIGHEST = lax.Precision.HIGHEST

NORM_EPS = 1e-6
D_FF = 2816
GDN_HEADS = 4
GDN_HEAD_DIM = 128
GDN_WIDTH = GDN_HEADS * GDN_HEAD_DIM
GDN_CONV = 4
GDN_CHUNK = 64
GDN_GROUP = 2 * GDN_CHUNK
ATT_HEADS = 4
ATT_HEAD_DIM = 64
ATT_WIDTH = ATT_HEADS * ATT_HEAD_DIM
MOBA_BLOCK = 256
MOBA_TOPK = 3
MASKED = -1e30
LOG2_E = 1.4426950408889634
BIAS_TERMS = 3

V7X_VMEM_LIMIT_BYTES = 56 * 1024 * 1024
V7X_LANES = 128
V7X_BF16_SUBLANES = 16
ATT_VROWS = ATT_HEAD_DIM + V7X_BF16_SUBLANES
FFN_ROWS = 256
GDN_ROWS = 512


def _dot(a, b, precision=None):
    return jnp.dot(a, b, preferred_element_type=F32, precision=precision)


def _dot_nt(a, b, precision=None):
    return lax.dot_general(a, b, (((1,), (1,)), ((), ())), preferred_element_type=F32,
                           precision=precision)


def _rms_rows(x, gain):
    return x * lax.rsqrt(jnp.mean(x * x, axis=-1, keepdims=True) + NORM_EPS) * gain


def _silu(x):
    return x * jax.nn.sigmoid(x)


def _swiglu(x, gain, wg_ref, wu_ref, wd_ref):
    h = _rms_rows(x, gain).astype(BF16)
    g = _dot(h, wg_ref[...])
    u = _dot(h, wu_ref[...])
    return _dot((_silu(g) * u).astype(BF16), wd_ref[...])


def _resident():
    return pl.BlockSpec(memory_space=pltpu.VMEM)


def _bf16_terms(x):
    terms, rest = [], x
    for _ in range(BIAS_TERMS):
        t = rest.astype(BF16).astype(F32)
        terms.append(t)
        rest = rest - t
    return terms


def _bf16_terms_const(value):
    def f32(v):
        return struct.unpack("<f", struct.pack("<f", v))[0]

    def bf16_round(v):
        bits = struct.unpack("<I", struct.pack("<f", v))[0]
        bits = (bits + 0x7FFF + ((bits >> 16) & 1)) & 0xFFFF0000
        return struct.unpack("<f", struct.pack("<I", bits))[0]

    terms, rest = [], f32(value)
    for _ in range(BIAS_TERMS):
        t = bf16_round(rest)
        terms.append(t)
        rest = f32(rest - t)
    return terms


def _ffn1_inproj_kernel(x_ref, n1_ref, wg_ref, wu_ref, wd_ref, n2_ref, wgdn_ref, wab_ref, watt_ref,
                        x1_ref, pg_ref, pab_ref, patt_ref):
    x = x_ref[...]
    x1 = x + 0.5 * _swiglu(x, n1_ref[...], wg_ref, wu_ref, wd_ref)
    x1_ref[...] = x1
    h = _rms_rows(x1, n2_ref[...]).astype(BF16)
    pg_ref[...] = _dot(h, wgdn_ref[...])
    pab_ref[...] = _dot(h, wab_ref[...])
    patt_ref[...] = _dot(h, watt_ref[...])


def _ffn1_inproj(x2d, n1, wg, wu, wd, n2, wgdn, wab, watt):
    n, d = x2d.shape
    rows = FFN_ROWS
    row_spec = lambda w: pl.BlockSpec((rows, w), lambda i: (i, 0))
    return pl.pallas_call(
        _ffn1_inproj_kernel,
        grid=(n // rows,),
        in_specs=[row_spec(d)] + [_resident()] * 8,
        out_specs=[row_spec(d), row_spec(wgdn.shape[1]), row_spec(wab.shape[1]), row_spec(watt.shape[1])],
        out_shape=[jax.ShapeDtypeStruct((n, d), F32),
                   jax.ShapeDtypeStruct((n, wgdn.shape[1]), F32),
                   jax.ShapeDtypeStruct((n, wab.shape[1]), F32),
                   jax.ShapeDtypeStruct((n, watt.shape[1]), F32)],
        compiler_params=pltpu.CompilerParams(dimension_semantics=("arbitrary",),
                                             vmem_limit_bytes=V7X_VMEM_LIMIT_BYTES),
        name="ffn1_inproj",
    )(x2d, n1, wg, wu, wd, n2, wgdn, wab, watt)


def _outproj_ffn2_kernel(x1_ref, yg_ref, ya_ref, wog_ref, woa_ref, n_ref, wg_ref, wu_ref, wd_ref, o_ref):
    x2 = x1_ref[...] + _dot(yg_ref[...], wog_ref[...]) + _dot(ya_ref[...], woa_ref[...])
    o_ref[...] = x2 + 0.5 * _swiglu(x2, n_ref[...], wg_ref, wu_ref, wd_ref)


def _outproj_ffn2(x1, yg, ya, wog, woa, nrm, wg, wu, wd):
    n, d = x1.shape
    rows = FFN_ROWS
    row_spec = lambda w: pl.BlockSpec((rows, w), lambda i: (i, 0))
    return pl.pallas_call(
        _outproj_ffn2_kernel,
        grid=(n // rows,),
        in_specs=[row_spec(d), row_spec(yg.shape[1]), row_spec(ya.shape[1])] + [_resident()] * 6,
        out_specs=row_spec(d),
        out_shape=jax.ShapeDtypeStruct((n, d), F32),
        compiler_params=pltpu.CompilerParams(dimension_semantics=("arbitrary",),
                                             vmem_limit_bytes=V7X_VMEM_LIMIT_BYTES),
        name="outproj_ffn2",
    )(x1, yg, ya, wog, woa, nrm, wg, wu, wd)


def _head_block_ones(width, head_dim):
    r = lax.broadcasted_iota(jnp.int32, (width, width), 0) // head_dim
    c = lax.broadcasted_iota(jnp.int32, (width, width), 1) // head_dim
    return jnp.where(r == c, 1.0, 0.0).astype(BF16)


def _head_rms(x, gain, block_ones, head_dim):
    sq = x * x
    hi = sq.astype(BF16)
    lo = (sq - hi.astype(F32)).astype(BF16)
    ss = _dot(hi, block_ones) + _dot(lo, block_ones)
    return x * lax.rsqrt(ss * (1.0 / head_dim) + NORM_EPS) * gain


def _values_with_ones(v_t):
    Dh = ATT_HEAD_DIM
    row = lax.broadcasted_iota(jnp.int32, (V7X_BF16_SUBLANES, v_t.shape[1]), 0)
    ones_block = jnp.where(row == 0, 1.0, 0.0)
    parts = []
    for h in range(ATT_HEADS):
        parts += [v_t[h * Dh:(h + 1) * Dh], ones_block]
    return jnp.concatenate(parts, axis=0).astype(BF16)


def _mem_kv_kernel(mem_ref, nrm_ref, wkv_ref, kg_ref, k_ref, vt_ref):
    h = _rms_rows(mem_ref[...], nrm_ref[...]).astype(BF16)
    kv = _dot(h, wkv_ref[...])
    k = _head_rms(kv[:, :ATT_WIDTH], kg_ref[...], _head_block_ones(ATT_WIDTH, ATT_HEAD_DIM), ATT_HEAD_DIM)
    k_ref[...] = k.astype(BF16)
    vt_ref[...] = _values_with_ones(kv[:, ATT_WIDTH:].T)


def _mem_kv(mem, nrm, wkv, kgain):
    b, m, d = mem.shape
    return pl.pallas_call(
        _mem_kv_kernel,
        grid=(b,),
        in_specs=[pl.BlockSpec((None, m, d), lambda i: (i, 0, 0)), _resident(), _resident(), _resident()],
        out_specs=[pl.BlockSpec((None, m, ATT_WIDTH), lambda i: (i, 0, 0)),
                   pl.BlockSpec((None, ATT_HEADS * ATT_VROWS, m), lambda i: (i, 0, 0))],
        out_shape=[jax.ShapeDtypeStruct((b, m, ATT_WIDTH), BF16),
                   jax.ShapeDtypeStruct((b, ATT_HEADS * ATT_VROWS, m), BF16)],
        compiler_params=pltpu.CompilerParams(dimension_semantics=("arbitrary",)),
        name="mem_kv",
    )(mem, nrm, wkv, kgain)


def _softplus(x):
    return jnp.maximum(x, 0.0) + jnp.log(1.0 + jnp.exp(-jnp.abs(x)))


def _unit_lower_inverses(a_list, eye):
    x16 = [(-a).astype(BF16) for a in a_list]
    t = [eye - a for a in a_list]
    for _ in range(5):
        x16 = [_dot(xb, xb).astype(BF16) for xb in x16]
        t = [ti + _dot(ti.astype(BF16), xb) for ti, xb in zip(t, x16)]
    t16 = [ti.astype(BF16) for ti in t]
    a_hi = [a.astype(BF16) for a in a_list]
    a_lo = [(a - hi.astype(F32)).astype(BF16) for a, hi in zip(a_list, a_hi)]
    resid = [(eye - tb.astype(F32) - _dot(hi, tb) - _dot(lo, tb)).astype(BF16)
             for tb, hi, lo in zip(t16, a_hi, a_lo)]
    return [tb.astype(F32) + _dot(tb, r) for tb, r in zip(t16, resid)]


def _gdn_kernel(qkv_ref, z_ref, ab_ref, cw_ref, alog_ref, dtb_ref, og_ref, y_ref, xbuf_ref, s_ref):
    rows = z_ref.shape[0]
    H, Dh, C, W = GDN_HEADS, GDN_HEAD_DIM, GDN_CHUNK, GDN_WIDTH
    G = GDN_GROUP
    n_chunks = rows // C
    n_groups = rows // G
    t = pl.program_id(1)

    @pl.when(t == 0)
    def _():
        xbuf_ref[0:8, :] = jnp.zeros((8, 3 * W), F32)
        s_ref[...] = jnp.zeros_like(s_ref)

    xbuf_ref[8:, :] = qkv_ref[...]
    conv = cw_ref[0:1, :] * xbuf_ref[5:5 + rows, :]
    for j in range(1, GDN_CONV):
        conv = conv + cw_ref[j:j + 1, :] * xbuf_ref[5 + j:5 + j + rows, :]
    xbuf_ref[0:8, :] = xbuf_ref[rows:rows + 8, :]
    act = _silu(conv)

    ab = ab_ref[...]
    log_decay = -jnp.exp(alog_ref[...]) * _softplus(ab + dtb_ref[...])
    beta_all = jax.nn.sigmoid(ab)
    rt = lax.broadcasted_iota(jnp.int32, (rows, rows), 0)
    ct = lax.broadcasted_iota(jnp.int32, (rows, rows), 1)
    chunk_lower = (((rt // C) == (ct // C)) & (rt >= ct)).astype(F32)
    gc_all = _dot(chunk_lower, log_decay, HIGHEST)
    gc_rows = gc_all.T

    ri = lax.broadcasted_iota(jnp.int32, (G, G), 0)
    ci = lax.broadcasted_iota(jnp.int32, (G, G), 1)
    same_chunk = (ri // C) == (ci // C)
    lower = same_chunk & (ri >= ci)
    strict = same_chunk & (ri > ci)
    eye = (ri == ci).astype(F32)

    units = [(h, g) for h in range(H) for g in range(n_groups)]
    q_h, k_h, v_h, gcol_h, beta_h = [], [], [], [], []
    for h in range(H):
        qh = act[:, h * Dh:(h + 1) * Dh]
        kh = act[:, W + h * Dh:W + (h + 1) * Dh]
        v_h.append(act[:, 2 * W + h * Dh:2 * W + (h + 1) * Dh])
        q_h.append(qh * (lax.rsqrt(jnp.sum(qh * qh, axis=-1, keepdims=True) + NORM_EPS) * Dh ** -0.5))
        k_h.append(kh * lax.rsqrt(jnp.sum(kh * kh, axis=-1, keepdims=True) + NORM_EPS))
        gcol_h.append(jnp.broadcast_to(gc_all[:, h:h + 1], (rows, Dh)))
        beta_h.append(jnp.broadcast_to(beta_all[:, H + h:H + h + 1], (rows, Dh)))

    def unit_rows(x, g):
        return x[g * G:(g + 1) * G]

    k_u = [unit_rows(k_h[h], g) for h, g in units]
    gcol_u = [unit_rows(gcol_h[h], g) for h, g in units]
    kb_u = [k * unit_rows(beta_h[h], g) for k, (h, g) in zip(k_u, units)]
    k16_u = [k.astype(BF16) for k in k_u]
    q_u = [unit_rows(q_h[h], g) for h, g in units]
    kk_u = [_dot_nt(kb.astype(BF16), k16) for kb, k16 in zip(kb_u, k16_u)]
    qk_u = [_dot_nt(q.astype(BF16), k16) for q, k16 in zip(q_u, k16_u)]
    decay_u = []
    for gcol, (h, g) in zip(gcol_u, units):
        diff = gcol - gc_rows[h:h + 1, g * G:(g + 1) * G]
        decay_u.append(jnp.where(lower, jnp.exp(jnp.where(lower, diff, 0.0)), 0.0))
    a_u = [jnp.where(strict, kk * decay, 0.0) for kk, decay in zip(kk_u, decay_u)]
    qk16_u = [(qk * decay).astype(BF16) for qk, decay in zip(qk_u, decay_u)]
    tinv_u = _unit_lower_inverses(a_u, eye)
    egc_u = [jnp.exp(gcol) for gcol in gcol_u]
    rhs_u = [jnp.concatenate([unit_rows(v_h[h], g) * unit_rows(beta_h[h], g), kb * egc], axis=1).astype(BF16)
             for kb, egc, (h, g) in zip(kb_u, egc_u, units)]
    uw_u = [_dot(tinv.astype(BF16), rhs) for tinv, rhs in zip(tinv_u, rhs_u)]
    w16_u = [uw[:, Dh:].astype(BF16) for uw in uw_u]
    qe16_u = [(q * egc).astype(BF16) for q, egc in zip(q_u, egc_u)]

    def chunk_operands(h, c):
        u = h * n_groups + (c * C) // G
        r = slice((c * C) % G, (c * C) % G + C)
        g_last = gcol_u[u][r.stop - 1:r.stop, :]
        return (uw_u[u][r, :Dh],
                jnp.concatenate([w16_u[u][r], qe16_u[u][r]], axis=0),
                qk16_u[u][r, r],
                (k_u[u][r] * jnp.exp(g_last - gcol_u[u][r])).T.astype(BF16),
                jnp.exp(g_last))

    ops = [[chunk_operands(h, c) for c in range(n_chunks)] for h in range(H)]

    state = [s_ref[h] for h in range(H)]
    o_l = [[] for _ in range(H)]
    for c in range(n_chunks):
        ws = [_dot(ops[h][c][1], state[h].astype(BF16)) for h in range(H)]
        v_new = [(ops[h][c][0] - ws[h][:C]).astype(BF16) for h in range(H)]
        state = [state[h] * ops[h][c][4] + _dot(ops[h][c][3], v_new[h]) for h in range(H)]
        for h in range(H):
            o_l[h].append(ws[h][C:] + _dot(ops[h][c][2], v_new[h]))
    for h in range(H):
        s_ref[h] = state[h]

    z = z_ref[...]
    for h in range(H):
        cols = slice(h * Dh, (h + 1) * Dh)
        o = jnp.concatenate(o_l[h], axis=0)
        y_ref[:, cols] = (_rms_rows(o, og_ref[...]) * _silu(z[:, cols])).astype(BF16)


def _gdn(pg, pab, conv_w, alog_vec, dtb_vec, out_gain, batch, seq):
    n = pg.shape[0]
    rows = GDN_ROWS
    tiles = seq // rows
    W = GDN_WIDTH
    return pl.pallas_call(
        _gdn_kernel,
        grid=(batch, tiles),
        in_specs=[pl.BlockSpec((rows, 3 * W), lambda b, t: (b * tiles + t, 0)),
                  pl.BlockSpec((rows, W), lambda b, t: (b * tiles + t, 3)),
                  pl.BlockSpec((rows, pab.shape[1]), lambda b, t: (b * tiles + t, 0)),
                  _resident(), _resident(), _resident(), _resident()],
        out_specs=pl.BlockSpec((rows, W), lambda b, t: (b * tiles + t, 0)),
        out_shape=jax.ShapeDtypeStruct((n, W), BF16),
        scratch_shapes=[pltpu.VMEM((rows + 8, 3 * W), F32),
                        pltpu.VMEM((GDN_HEADS, GDN_HEAD_DIM, GDN_HEAD_DIM), F32)],
        compiler_params=pltpu.CompilerParams(dimension_semantics=("arbitrary", "arbitrary"),
                                             vmem_limit_bytes=V7X_VMEM_LIMIT_BYTES),
        name="gdn",
    )(pg, pg, pab, conv_w, alog_vec, dtb_vec, out_gain)


def _moba_prep_kernel(p_ref, qg_ref, kg_ref, cg_ref, qt_ref, kn_ref, vt_ref, cqt_ref, sel_ref, km_ref):
    H, Dh, Wd = ATT_HEADS, ATT_HEAD_DIM, ATT_WIDTH
    nb = km_ref.shape[0]
    i = pl.program_id(1)

    @pl.when(i == 0)
    def _():
        km_ref[...] = jnp.zeros_like(km_ref)

    block_ones = _head_block_ones(Wd, Dh)
    qn = _head_rms(p_ref[:, 0:Wd], qg_ref[...], block_ones, Dh)
    kn = _head_rms(p_ref[:, Wd:2 * Wd], kg_ref[...], block_ones, Dh)
    cqn = _head_rms(p_ref[:, 3 * Wd:4 * Wd], cg_ref[...], block_ones, Dh)
    scale = Dh ** -0.5 * LOG2_E
    qt_ref[...] = (qn * scale).T.astype(BF16)
    cqt_ref[...] = (cqn * scale).T.astype(BF16)
    kn_ref[...] = kn.astype(BF16)
    vt_ref[...] = _values_with_ones(p_ref[:, 2 * Wd:3 * Wd].T)

    km = km_ref[...]
    blk = lax.broadcasted_iota(jnp.int32, (nb, Wd), 0)
    lane_head = lax.broadcasted_iota(jnp.int32, (nb, Wd), 1) // Dh
    km_heads = jnp.concatenate([jnp.where(lane_head == h, km, 0.0) for h in range(H)], axis=0)
    gate_all = _dot_nt(km_heads, qn, HIGHEST)
    jdx = lax.broadcasted_iota(jnp.int32, (nb, MOBA_BLOCK), 0)
    jdx_f = jdx.astype(F32)
    past = jdx < i
    for h in range(H):
        gate = jnp.where(past, gate_all[h * nb:(h + 1) * nb], -jnp.inf)
        chosen = jnp.zeros(gate.shape, jnp.bool_)
        for _ in range(MOBA_TOPK):
            top = jnp.max(gate, axis=0, keepdims=True)
            first = jnp.min(jnp.where(gate == top, jdx_f, float(nb)), axis=0, keepdims=True)
            pick = jdx_f == first
            chosen = chosen | pick
            gate = jnp.where(pick, -jnp.inf, gate)
        sel_ref[h] = jnp.where(chosen & past, 0.0, MASKED)
    km_ref[...] = jnp.where(blk == i, jnp.mean(kn, axis=0, keepdims=True), km)


def _moba_prep(patt, qg, kg, cg, batch, seq):
    n = patt.shape[0]
    nb = seq // MOBA_BLOCK
    Wd, H, BLK = ATT_WIDTH, ATT_HEADS, MOBA_BLOCK
    return pl.pallas_call(
        _moba_prep_kernel,
        grid=(batch, nb),
        in_specs=[pl.BlockSpec((BLK, 4 * Wd), lambda b, i: (b * nb + i, 0)),
                  _resident(), _resident(), _resident()],
        out_specs=[pl.BlockSpec((None, Wd, BLK), lambda b, i: (b, 0, i)),
                   pl.BlockSpec((BLK, Wd), lambda b, i: (b * nb + i, 0)),
                   pl.BlockSpec((None, None, H * ATT_VROWS, BLK), lambda b, i: (b, i, 0, 0)),
                   pl.BlockSpec((None, Wd, BLK), lambda b, i: (b, 0, i)),
                   pl.BlockSpec((None, H, nb, BLK), lambda b, i: (b, 0, 0, i))],
        out_shape=[jax.ShapeDtypeStruct((batch, Wd, seq), BF16),
                   jax.ShapeDtypeStruct((n, Wd), BF16),
                   jax.ShapeDtypeStruct((batch, nb, H * ATT_VROWS, BLK), BF16),
                   jax.ShapeDtypeStruct((batch, Wd, seq), BF16),
                   jax.ShapeDtypeStruct((batch, H, nb, seq), F32)],
        scratch_shapes=[pltpu.VMEM((nb, Wd), F32)],
        compiler_params=pltpu.CompilerParams(dimension_semantics=("arbitrary", "arbitrary")),
        name="moba_prep",
    )(patt, qg, kg, cg)


def _alibi_slope(h):
    return 2.0 ** (-8.0 * (h + 1) / ATT_HEADS)


def _moba_attn_kernel(qt_ref, cqt_ref, k_ref, vt_ref, sel_ref, km_ref, vmt_ref, y_ref,
                      m_ref, acc_ref, s_ref, kaug_ref, rhs_ref):
    H, Dh, BLK = ATT_HEADS, ATT_HEAD_DIM, MOBA_BLOCK
    PAIR = 2 * Dh
    AUG = V7X_BF16_SUBLANES
    i = pl.program_id(1)
    pair_row_head = lax.broadcasted_iota(jnp.int32, (PAIR, BLK), 0) // Dh

    @pl.when((pl.program_id(0) == 0) & (i == 0))
    def _():
        lane = lax.broadcasted_iota(jnp.int32, (BLK, V7X_LANES), 1)
        key_pos = lax.broadcasted_iota(jnp.int32, (BLK, V7X_LANES), 0).astype(F32)
        kaug_ref[...] = jnp.where(lane < BIAS_TERMS, key_pos,
                                  jnp.where(lane < 2 * BIAS_TERMS, 1.0, 0.0)).astype(BF16)
        rhs_ref[...] = jnp.zeros_like(rhs_ref)

    def pair_lanes(h):
        return slice((h // 2) * PAIR, (h // 2 + 1) * PAIR)

    def value_rows(h):
        return slice(h * ATT_VROWS, (h + 1) * ATT_VROWS)

    def normalised(acc):
        return acc[0:Dh] / acc[Dh:Dh + 1]

    def head_queries(src_ref, h):
        qp = src_ref[pair_lanes(h), :]
        return jnp.where(pair_row_head == h % 2, qp, jnp.zeros_like(qp))

    for h in range(H):
        rhs_ref[h, 0:PAIR, :] = head_queries(qt_ref, h)

    aug_row = lax.broadcasted_iota(jnp.int32, (AUG, BLK), 0)
    query_pos = lax.broadcasted_iota(jnp.int32, (1, BLK), 1).astype(F32)

    def scores(blk, own):
        r0 = pl.multiple_of(blk * BLK, BLK)
        distance = (i - blk).astype(F32) * BLK
        keys_aug = kaug_ref[...]
        out = []
        for h in range(H):
            c = _alibi_slope(h) * LOG2_E
            row_bias = -c * (query_pos + distance)
            if not own:
                row_bias = row_bias + sel_ref[h, pl.ds(blk, 1), :]
            rows = _bf16_terms_const(c) + _bf16_terms(row_bias)
            aug = jnp.zeros((AUG, BLK), F32)
            for n, val in enumerate(rows):
                aug = jnp.where(aug_row == n, val, aug)
            rhs_ref[h, PAIR:PAIR + AUG, :] = aug.astype(BF16)
            keys = jnp.concatenate([k_ref[pl.ds(r0, BLK), pair_lanes(h)], keys_aug], axis=1)
            out.append(_dot(keys, rhs_ref[h]))
        return out

    own_scores = scores(i, True)
    mem_scores = [_dot(km_ref[:, pair_lanes(h)], head_queries(cqt_ref, h)) for h in range(H)]
    first_past = scores(0, False)
    causal = (lax.broadcasted_iota(jnp.int32, (BLK, BLK), 0) <= lax.broadcasted_iota(jnp.int32, (BLK, BLK), 1))
    own_p, mem_p = [], []
    for h in range(H):
        s_t = jnp.where(causal, own_scores[h], MASKED)
        m = jnp.max(s_t, axis=0, keepdims=True)
        p = jnp.exp2(s_t - m)
        m_ref[h] = m
        own_p.append(p.astype(BF16))
    for h in range(H):
        p = jnp.exp2(mem_scores[h] - jnp.max(mem_scores[h], axis=0, keepdims=True))
        mem_p.append(p.astype(BF16))
    for h in range(H):
        acc_ref[h] = _dot(vt_ref[i, value_rows(h), :], own_p[h])
        s_ref[0, h] = first_past[h]
    mem_out = [normalised(_dot(vmt_ref[value_rows(h), :], mem_p[h])) for h in range(H)]
    y_ref[:, H * Dh:2 * H * Dh] = jnp.concatenate(mem_out, axis=0).T.astype(BF16)

    def block_step(j, slot, nxt):
        next_scores = None if nxt is None else scores(nxt, False)
        for h in range(H):
            s_t = s_ref[slot, h]
            m_old = m_ref[h]
            m_new = jnp.maximum(m_old, jnp.max(s_t, axis=0, keepdims=True))
            p = jnp.exp2(s_t - m_new)
            alpha = jnp.exp2(m_old - m_new)
            acc_ref[h] = alpha * acc_ref[h] + _dot(vt_ref[j, value_rows(h), :], p.astype(BF16))
            m_ref[h] = m_new
        if next_scores is not None:
            for h in range(H):
                s_ref[1 - slot, h] = next_scores[h]

    last_past = jnp.maximum(i - 1, 0)

    def block_run(j0, count):
        for n in range(count):
            block_step(j0 + n, n % 2, jnp.minimum(j0 + n + 1, last_past))

    def block_quad(jj, carry):
        block_run(4 * jj, 4)
        return carry

    lax.fori_loop(0, i // 4, block_quad, 0)

    @pl.when(i % 4 >= 2)
    def _():
        block_run(4 * (i // 4), 2)

    @pl.when(i % 2 == 1)
    def _():
        block_step(i - 1, 0, None)

    moba_out = [normalised(acc_ref[h]) for h in range(H)]
    y_ref[:, 0:H * Dh] = jnp.concatenate(moba_out, axis=0).T.astype(BF16)


def _moba_attn(qt, cqt, kn, vt, sel, kmem, vmem_t, batch, seq):
    nb = seq // MOBA_BLOCK
    Wd, H, BLK = ATT_WIDTH, ATT_HEADS, MOBA_BLOCK
    n_mem = kmem.shape[1]
    return pl.pallas_call(
        _moba_attn_kernel,
        grid=(batch, nb),
        in_specs=[pl.BlockSpec((None, Wd, BLK), lambda b, i: (b, 0, i)),
                  pl.BlockSpec((None, Wd, BLK), lambda b, i: (b, 0, i)),
                  pl.BlockSpec((seq, Wd), lambda b, i: (b, 0)),
                  pl.BlockSpec((None, nb, H * ATT_VROWS, BLK), lambda b, i: (b, 0, 0, 0)),
                  pl.BlockSpec((None, H, nb, BLK), lambda b, i: (b, 0, 0, i)),
                  pl.BlockSpec((None, n_mem, Wd), lambda b, i: (b, 0, 0)),
                  pl.BlockSpec((None, H * ATT_VROWS, n_mem), lambda b, i: (b, 0, 0))],
        out_specs=pl.BlockSpec((BLK, 2 * Wd), lambda b, i: (b * nb + i, 0)),
        out_shape=jax.ShapeDtypeStruct((batch * seq, 2 * Wd), BF16),
        scratch_shapes=[pltpu.VMEM((H, 1, BLK), F32),
                        pltpu.VMEM((H, ATT_VROWS, BLK), F32),
                        pltpu.VMEM((2, H, BLK, BLK), F32),
                        pltpu.VMEM((BLK, V7X_LANES), BF16),
                        pltpu.VMEM((H, 2 * ATT_HEAD_DIM + V7X_LANES, BLK), BF16)],
        compiler_params=pltpu.CompilerParams(dimension_semantics=("arbitrary", "arbitrary"),
                                             vmem_limit_bytes=V7X_VMEM_LIMIT_BYTES),
        name="moba_attn",
    )(qt, cqt, kn, vt, sel, kmem, vmem_t)


def _lane_vector(values, width):
    return jnp.zeros((1, width), F32).at[0, :values.shape[0]].set(values.astype(F32))


def _layer(x2d, mem, batch, seq, ffn1_norm, ffn1_w_gate, ffn1_w_up, ffn1_w_down, mix_norm, w_in,
           gdn_conv_w, gdn_a_log, gdn_dt_bias, gdn_out_norm, moba_q_norm, moba_k_norm,
           mem_norm, w_mem_kv, mem_q_norm, mem_k_norm, w_out, ffn2_norm, ffn2_w_gate, ffn2_w_up, ffn2_w_down):
    W, H = GDN_WIDTH, GDN_HEADS
    row = lambda v: v.reshape(1, -1).astype(F32)
    tile_heads = lambda v: jnp.tile(v.astype(F32), ATT_HEADS).reshape(1, -1)
    ab0 = 4 * W
    att0 = ab0 + 2 * H
    w_gdn = w_in[:, :ab0].astype(BF16)
    ab_width = V7X_LANES
    w_ab = jnp.zeros((w_in.shape[0], ab_width), F32).at[:, :2 * H].set(w_in[:, ab0:att0]).astype(BF16)
    w_att = w_in[:, att0:].astype(BF16)

    x1, pg, pab, patt = _ffn1_inproj(
        x2d, row(ffn1_norm), ffn1_w_gate.astype(BF16), ffn1_w_up.astype(BF16), ffn1_w_down.astype(BF16),
        row(mix_norm), w_gdn, w_ab, w_att)

    y_gdn = _gdn(pg, pab, gdn_conv_w.astype(F32), _lane_vector(gdn_a_log, ab_width),
                 _lane_vector(gdn_dt_bias, ab_width), row(gdn_out_norm), batch, seq)

    kmem, vmem_t = _mem_kv(mem, row(mem_norm), w_mem_kv.astype(BF16), tile_heads(mem_k_norm))
    qt, kn, vt, cqt, sel = _moba_prep(patt, tile_heads(moba_q_norm), tile_heads(moba_k_norm),
                                      tile_heads(mem_q_norm), batch, seq)
    y_att = _moba_attn(qt, cqt, kn, vt, sel, kmem, vmem_t, batch, seq)

    w_out_b = w_out.astype(BF16)
    return _outproj_ffn2(x1, y_gdn, y_att, w_out_b[:W], w_out_b[W:], row(ffn2_norm),
                         ffn2_w_gate.astype(BF16), ffn2_w_up.astype(BF16), ffn2_w_down.astype(BF16))


def kernel(x, mem, ffn1_norm, ffn1_w_gate, ffn1_w_up, ffn1_w_down, mix_norm, w_in, gdn_conv_w, gdn_a_log,
           gdn_dt_bias, gdn_out_norm, moba_q_norm, moba_k_norm, mem_norm, w_mem_kv, mem_q_norm, mem_k_norm,
           w_out, ffn2_norm, ffn2_w_gate, ffn2_w_up, ffn2_w_down):
    batch, seq, d = x.shape
    assert seq % MOBA_BLOCK == 0 and seq % GDN_ROWS == 0 and (batch * seq) % FFN_ROWS == 0
    depth = w_in.shape[0]
    x2d = x.reshape(batch * seq, d)
    for l in range(depth):
        x2d = _layer(x2d, mem, batch, seq, ffn1_norm[l], ffn1_w_gate[l], ffn1_w_up[l], ffn1_w_down[l],
                     mix_norm[l], w_in[l], gdn_conv_w[l], gdn_a_log[l], gdn_dt_bias[l], gdn_out_norm[l],
                     moba_q_norm[l], moba_k_norm[l], mem_norm[l], w_mem_kv[l], mem_q_norm[l], mem_k_norm[l],
                     w_out[l], ffn2_norm[l], ffn2_w_gate[l], ffn2_w_up[l], ffn2_w_down[l])
    return x2d.reshape(batch, seq, d)
```

```python
import struct

import jax
import jax.numpy as jnp
from jax import lax
from jax.experimental import pallas as pl
from jax.experimental.pallas import tpu as pltpu

F32 = jnp.float32
BF16 = jnp.bfloat16
HIGHEST = lax.Precision.HIGHEST

NORM_EPS = 1e-6
D_FF = 2816
GDN_HEADS = 4
GDN_HEAD_DIM = 128
GDN_WIDTH = GDN_HEADS * GDN_HEAD_DIM
GDN_CONV = 4
GDN_CHUNK = 64
GDN_GROUP = 2 * GDN_CHUNK
ATT_HEADS = 4
ATT_HEAD_DIM = 64
ATT_WIDTH = ATT_HEADS * ATT_HEAD_DIM
MOBA_BLOCK = 256
MOBA_TOPK = 3
MASKED = -1e30
LOG2_E = 1.4426950408889634
BIAS_TERMS = 3

V7X_VMEM_LIMIT_BYTES = 56 * 1024 * 1024
V7X_LANES = 128
V7X_BF16_SUBLANES = 16
ATT_VROWS = ATT_HEAD_DIM + V7X_BF16_SUBLANES
FFN_ROWS = 256
FFN1_SLABS = 2
FFN2_SLABS = 4
GDN_ROWS = 512


def _dot(a, b, precision=None):
    return jnp.dot(a, b, preferred_element_type=F32, precision=precision)


def _dot_nt(a, b, precision=None):
    return lax.dot_general(a, b, (((1,), (1,)), ((), ())), preferred_element_type=F32,
                           precision=precision)


def _rms_rows(x, gain):
    return x * lax.rsqrt(jnp.mean(x * x, axis=-1, keepdims=True) + NORM_EPS) * gain


def _silu(x):
    half = 0.5 * x
    return half + half * jnp.tanh(half)


def _swiglu(x, gain, wg_ref, wu_ref, wd_ref):
    h = _rms_rows(x, gain).astype(BF16)
    g = _dot(h, wg_ref[...])
    u = _dot(h, wu_ref[...])
    return _dot((_silu(g) * u).astype(BF16), wd_ref[...])


def _resident():
    return pl.BlockSpec(memory_space=pltpu.VMEM)


def _bf16_terms(x):
    terms, rest = [], x
    for _ in range(BIAS_TERMS):
        t = rest.astype(BF16).astype(F32)
        terms.append(t)
        rest = rest - t
    return terms


def _bf16_terms_const(value):
    def f32(v):
        return struct.unpack("<f", struct.pack("<f", v))[0]

    def bf16_round(v):
        bits = struct.unpack("<I", struct.pack("<f", v))[0]
        bits = (bits + 0x7FFF + ((bits >> 16) & 1)) & 0xFFFF0000
        return struct.unpack("<f", struct.pack("<I", bits))[0]

    terms, rest = [], f32(value)
    for _ in range(BIAS_TERMS):
        t = bf16_round(rest)
        terms.append(t)
        rest = f32(rest - t)
    return terms


def _ffn1_inproj_kernel(x_ref, n1_ref, wg_ref, wu_ref, wd_ref, n2_ref, wgdn_ref, wab_ref, watt_ref,
                        x1_ref, pg_ref, pab_ref, patt_ref):
    for r0 in range(0, x_ref.shape[0], FFN_ROWS):
        rows = slice(r0, r0 + FFN_ROWS)
        x = x_ref[rows, :]
        x1 = x + 0.5 * _swiglu(x, n1_ref[...], wg_ref, wu_ref, wd_ref)
        x1_ref[rows, :] = x1
        h = _rms_rows(x1, n2_ref[...]).astype(BF16)
        pg_ref[rows, :] = _dot(h, wgdn_ref[...])
        pab_ref[rows, :] = _dot(h, wab_ref[...])
        patt_ref[rows, :] = _dot(h, watt_ref[...])


def _ffn1_inproj(x2d, n1, wg, wu, wd, n2, wgdn, wab, watt):
    n, d = x2d.shape
    rows = FFN1_SLABS * FFN_ROWS
    row_spec = lambda w: pl.BlockSpec((rows, w), lambda i: (i, 0))
    return pl.pallas_call(
        _ffn1_inproj_kernel,
        grid=(n // rows,),
        in_specs=[row_spec(d)] + [_resident()] * 8,
        out_specs=[row_spec(d), row_spec(wgdn.shape[1]), row_spec(wab.shape[1]), row_spec(watt.shape[1])],
        out_shape=[jax.ShapeDtypeStruct((n, d), F32),
                   jax.ShapeDtypeStruct((n, wgdn.shape[1]), F32),
                   jax.ShapeDtypeStruct((n, wab.shape[1]), F32),
                   jax.ShapeDtypeStruct((n, watt.shape[1]), F32)],
        compiler_params=pltpu.CompilerParams(dimension_semantics=("arbitrary",),
                                             vmem_limit_bytes=V7X_VMEM_LIMIT_BYTES),
        name="ffn1_inproj",
    )(x2d, n1, wg, wu, wd, n2, wgdn, wab, watt)


def _outproj_ffn2_kernel(x1_ref, yg_ref, ya_ref, wog_ref, woa_ref, n_ref, wg_ref, wu_ref, wd_ref, o_ref):
    for r0 in range(0, x1_ref.shape[0], FFN_ROWS):
        rows = slice(r0, r0 + FFN_ROWS)
        x2 = x1_ref[rows, :] + _dot(yg_ref[rows, :], wog_ref[...]) + _dot(ya_ref[rows, :], woa_ref[...])
        o_ref[rows, :] = x2 + 0.5 * _swiglu(x2, n_ref[...], wg_ref, wu_ref, wd_ref)


def _outproj_ffn2(x1, yg, ya, wog, woa, nrm, wg, wu, wd):
    n, d = x1.shape
    rows = FFN2_SLABS * FFN_ROWS
    row_spec = lambda w: pl.BlockSpec((rows, w), lambda i: (i, 0))
    return pl.pallas_call(
        _outproj_ffn2_kernel,
        grid=(n // rows,),
        in_specs=[row_spec(d), row_spec(yg.shape[1]), row_spec(ya.shape[1])] + [_resident()] * 6,
        out_specs=row_spec(d),
        out_shape=jax.ShapeDtypeStruct((n, d), F32),
        compiler_params=pltpu.CompilerParams(dimension_semantics=("arbitrary",),
                                             vmem_limit_bytes=V7X_VMEM_LIMIT_BYTES),
        name="outproj_ffn2",
    )(x1, yg, ya, wog, woa, nrm, wg, wu, wd)


def _head_block_ones(width, head_dim):
    r = lax.broadcasted_iota(jnp.int32, (width, width), 0) // head_dim
    c = lax.broadcasted_iota(jnp.int32, (width, width), 1) // head_dim
    return jnp.where(r == c, 1.0, 0.0).astype(BF16)


def _head_rms(x, gain, block_ones, head_dim):
    sq = x * x
    hi = sq.astype(BF16)
    lo = (sq - hi.astype(F32)).astype(BF16)
    ss = _dot(hi, block_ones) + _dot(lo, block_ones)
    return x * lax.rsqrt(ss * (1.0 / head_dim) + NORM_EPS) * gain


def _values_with_ones(v_t):
    Dh = ATT_HEAD_DIM
    row = lax.broadcasted_iota(jnp.int32, (V7X_BF16_SUBLANES, v_t.shape[1]), 0)
    ones_block = jnp.where(row == 0, 1.0, 0.0)
    parts = []
    for h in range(ATT_HEADS):
        parts += [v_t[h * Dh:(h + 1) * Dh], ones_block]
    return jnp.concatenate(parts, axis=0).astype(BF16)


def _mem_kv_kernel(mem_ref, nrm_ref, wkv_ref, kg_ref, k_ref, vt_ref):
    h = _rms_rows(mem_ref[...], nrm_ref[...]).astype(BF16)
    kv = _dot(h, wkv_ref[...])
    k = _head_rms(kv[:, :ATT_WIDTH], kg_ref[...], _head_block_ones(ATT_WIDTH, ATT_HEAD_DIM), ATT_HEAD_DIM)
    k_ref[...] = k.astype(BF16)
    vt_ref[...] = _values_with_ones(kv[:, ATT_WIDTH:].T)


def _mem_kv(mem, nrm, wkv, kgain):
    b, m, d = mem.shape
    return pl.pallas_call(
        _mem_kv_kernel,
        grid=(b,),
        in_specs=[pl.BlockSpec((None, m, d), lambda i: (i, 0, 0)), _resident(), _resident(), _resident()],
        out_specs=[pl.BlockSpec((None, m, ATT_WIDTH), lambda i: (i, 0, 0)),
                   pl.BlockSpec((None, ATT_HEADS * ATT_VROWS, m), lambda i: (i, 0, 0))],
        out_shape=[jax.ShapeDtypeStruct((b, m, ATT_WIDTH), BF16),
                   jax.ShapeDtypeStruct((b, ATT_HEADS * ATT_VROWS, m), BF16)],
        compiler_params=pltpu.CompilerParams(dimension_semantics=("arbitrary",)),
        name="mem_kv",
    )(mem, nrm, wkv, kgain)


def _softplus(x):
    return jnp.maximum(x, 0.0) + jnp.log(1.0 + jnp.exp(-jnp.abs(x)))


def _unit_lower_inverses(a_list, eye):
    x16 = [(-a).astype(BF16) for a in a_list]
    t = [eye - a for a in a_list]
    for _ in range(5):
        x16 = [_dot(xb, xb).astype(BF16) for xb in x16]
        t = [ti + _dot(ti.astype(BF16), xb) for ti, xb in zip(t, x16)]
    t16 = [ti.astype(BF16) for ti in t]
    a_hi = [a.astype(BF16) for a in a_list]
    a_lo = [(a - hi.astype(F32)).astype(BF16) for a, hi in zip(a_list, a_hi)]
    resid = [(eye - tb.astype(F32) - _dot(hi, tb) - _dot(lo, tb)).astype(BF16)
             for tb, hi, lo in zip(t16, a_hi, a_lo)]
    return [tb.astype(F32) + _dot(tb, r) for tb, r in zip(t16, resid)]


def _gdn_kernel(qkv_ref, z_ref, ab_ref, cw_ref, alog_ref, dtb_ref, og_ref, y_ref, xbuf_ref, s_ref):
    rows = z_ref.shape[0]
    H, Dh, C, W = GDN_HEADS, GDN_HEAD_DIM, GDN_CHUNK, GDN_WIDTH
    G = GDN_GROUP
    n_chunks = rows // C
    n_groups = rows // G
    t = pl.program_id(1)

    @pl.when(t == 0)
    def _():
        xbuf_ref[0:8, :] = jnp.zeros((8, 3 * W), F32)
        s_ref[...] = jnp.zeros_like(s_ref)

    xbuf_ref[8:, :] = qkv_ref[...]
    conv = cw_ref[0:1, :] * xbuf_ref[5:5 + rows, :]
    for j in range(1, GDN_CONV):
        conv = conv + cw_ref[j:j + 1, :] * xbuf_ref[5 + j:5 + j + rows, :]
    xbuf_ref[0:8, :] = xbuf_ref[rows:rows + 8, :]
    act = _silu(conv)

    ab = ab_ref[...]
    log_decay = -jnp.exp(alog_ref[...]) * _softplus(ab + dtb_ref[...])
    beta_all = jax.nn.sigmoid(ab)
    rt = lax.broadcasted_iota(jnp.int32, (rows, rows), 0)
    ct = lax.broadcasted_iota(jnp.int32, (rows, rows), 1)
    chunk_lower = (((rt // C) == (ct // C)) & (rt >= ct)).astype(F32)
    gc_all = _dot(chunk_lower, log_decay, HIGHEST)
    gc_rows = gc_all.T

    ri = lax.broadcasted_iota(jnp.int32, (G, G), 0)
    ci = lax.broadcasted_iota(jnp.int32, (G, G), 1)
    same_chunk = (ri // C) == (ci // C)
    lower = same_chunk & (ri >= ci)
    strict = same_chunk & (ri > ci)
    eye = (ri == ci).astype(F32)

    units = [(h, g) for h in range(H) for g in range(n_groups)]
    q_h, k_h, v_h, gcol_h, beta_h = [], [], [], [], []
    for h in range(H):
        qh = act[:, h * Dh:(h + 1) * Dh]
        kh = act[:, W + h * Dh:W + (h + 1) * Dh]
        v_h.append(act[:, 2 * W + h * Dh:2 * W + (h + 1) * Dh])
        q_h.append(qh * (lax.rsqrt(jnp.sum(qh * qh, axis=-1, keepdims=True) + NORM_EPS) * Dh ** -0.5))
        k_h.append(kh * lax.rsqrt(jnp.sum(kh * kh, axis=-1, keepdims=True) + NORM_EPS))
        gcol_h.append(jnp.broadcast_to(gc_all[:, h:h + 1], (rows, Dh)))
        beta_h.append(jnp.broadcast_to(beta_all[:, H + h:H + h + 1], (rows, Dh)))

    def unit_rows(x, g):
        return x[g * G:(g + 1) * G]

    k_u = [unit_rows(k_h[h], g) for h, g in units]
    gcol_u = [unit_rows(gcol_h[h], g) for h, g in units]
    kb_u = [k * unit_rows(beta_h[h], g) for k, (h, g) in zip(k_u, units)]
    k16_u = [k.astype(BF16) for k in k_u]
    q_u = [unit_rows(q_h[h], g) for h, g in units]
    kk_u = [_dot_nt(kb.astype(BF16), k16) for kb, k16 in zip(kb_u, k16_u)]
    qk_u = [_dot_nt(q.astype(BF16), k16) for q, k16 in zip(q_u, k16_u)]
    decay_u = []
    for gcol, (h, g) in zip(gcol_u, units):
        diff = gcol - gc_rows[h:h + 1, g * G:(g + 1) * G]
        decay_u.append(jnp.where(lower, jnp.exp(jnp.where(lower, diff, 0.0)), 0.0))
    a_u = [jnp.where(strict, kk * decay, 0.0) for kk, decay in zip(kk_u, decay_u)]
    qk16_u = [(qk * decay).astype(BF16) for qk, decay in zip(qk_u, decay_u)]
    tinv_u = _unit_lower_inverses(a_u, eye)
    egc_u = [jnp.exp(gcol) for gcol in gcol_u]
    rhs_u = [jnp.concatenate([unit_rows(v_h[h], g) * unit_rows(beta_h[h], g), kb * egc], axis=1).astype(BF16)
             for kb, egc, (h, g) in zip(kb_u, egc_u, units)]
    uw_u = [_dot(tinv.astype(BF16), rhs) for tinv, rhs in zip(tinv_u, rhs_u)]
    w16_u = [uw[:, Dh:].astype(BF16) for uw in uw_u]
    qe16_u = [(q * egc).astype(BF16) for q, egc in zip(q_u, egc_u)]

    def chunk_operands(h, c):
        u = h * n_groups + (c * C) // G
        r = slice((c * C) % G, (c * C) % G + C)
        g_last = gcol_u[u][r.stop - 1:r.stop, :]
        return (uw_u[u][r, :Dh],
                jnp.concatenate([w16_u[u][r], qe16_u[u][r]], axis=0),
                qk16_u[u][r, r],
                (k_u[u][r] * jnp.exp(g_last - gcol_u[u][r])).T.astype(BF16),
                jnp.exp(g_last))

    ops = [[chunk_operands(h, c) for c in range(n_chunks)] for h in range(H)]

    state = [s_ref[h] for h in range(H)]
    o_l = [[] for _ in range(H)]
    for c in range(n_chunks):
        ws = [_dot(ops[h][c][1], state[h].astype(BF16)) for h in range(H)]
        v_new = [(ops[h][c][0] - ws[h][:C]).astype(BF16) for h in range(H)]
        state = [state[h] * ops[h][c][4] + _dot(ops[h][c][3], v_new[h]) for h in range(H)]
        for h in range(H):
            o_l[h].append(ws[h][C:] + _dot(ops[h][c][2], v_new[h]))
    for h in range(H):
        s_ref[h] = state[h]

    z = z_ref[...]
    for h in range(H):
        cols = slice(h * Dh, (h + 1) * Dh)
        o = jnp.concatenate(o_l[h], axis=0)
        y_ref[:, cols] = (_rms_rows(o, og_ref[...]) * _silu(z[:, cols])).astype(BF16)


def _gdn(pg, pab, conv_w, alog_vec, dtb_vec, out_gain, batch, seq):
    n = pg.shape[0]
    rows = GDN_ROWS
    tiles = seq // rows
    W = GDN_WIDTH
    return pl.pallas_call(
        _gdn_kernel,
        grid=(batch, tiles),
        in_specs=[pl.BlockSpec((rows, 3 * W), lambda b, t: (b * tiles + t, 0)),
                  pl.BlockSpec((rows, W), lambda b, t: (b * tiles + t, 3)),
                  pl.BlockSpec((rows, pab.shape[1]), lambda b, t: (b * tiles + t, 0)),
                  _resident(), _resident(), _resident(), _resident()],
        out_specs=pl.BlockSpec((rows, W), lambda b, t: (b * tiles + t, 0)),
        out_shape=jax.ShapeDtypeStruct((n, W), BF16),
        scratch_shapes=[pltpu.VMEM((rows + 8, 3 * W), F32),
                        pltpu.VMEM((GDN_HEADS, GDN_HEAD_DIM, GDN_HEAD_DIM), F32)],
        compiler_params=pltpu.CompilerParams(dimension_semantics=("arbitrary", "arbitrary"),
                                             vmem_limit_bytes=V7X_VMEM_LIMIT_BYTES),
        name="gdn",
    )(pg, pg, pab, conv_w, alog_vec, dtb_vec, out_gain)


def _moba_prep_kernel(p_ref, qg_ref, kg_ref, cg_ref, qt_ref, kn_ref, vt_ref, cqt_ref, sel_ref, km_ref):
    H, Dh, Wd = ATT_HEADS, ATT_HEAD_DIM, ATT_WIDTH
    nb = km_ref.shape[0]
    i = pl.program_id(1)

    @pl.when(i == 0)
    def _():
        km_ref[...] = jnp.zeros_like(km_ref)

    block_ones = _head_block_ones(Wd, Dh)
    qn = _head_rms(p_ref[:, 0:Wd], qg_ref[...], block_ones, Dh)
    kn = _head_rms(p_ref[:, Wd:2 * Wd], kg_ref[...], block_ones, Dh)
    cqn = _head_rms(p_ref[:, 3 * Wd:4 * Wd], cg_ref[...], block_ones, Dh)
    scale = Dh ** -0.5 * LOG2_E
    qt_ref[...] = (qn * scale).T.astype(BF16)
    cqt_ref[...] = (cqn * scale).T.astype(BF16)
    kn_ref[...] = kn.astype(BF16)
    vt_ref[...] = _values_with_ones(p_ref[:, 2 * Wd:3 * Wd].T)

    km = km_ref[...]
    blk = lax.broadcasted_iota(jnp.int32, (nb, Wd), 0)
    lane_head = lax.broadcasted_iota(jnp.int32, (nb, Wd), 1) // Dh
    km_heads = jnp.concatenate([jnp.where(lane_head == h, km, 0.0) for h in range(H)], axis=0)
    gate_all = _dot_nt(km_heads, qn, HIGHEST)
    jdx = lax.broadcasted_iota(jnp.int32, (nb, MOBA_BLOCK), 0)
    jdx_f = jdx.astype(F32)
    past = jdx < i
    for h in range(H):
        gate = jnp.where(past, gate_all[h * nb:(h + 1) * nb], -jnp.inf)
        chosen = jnp.zeros(gate.shape, jnp.bool_)
        for _ in range(MOBA_TOPK):
            top = jnp.max(gate, axis=0, keepdims=True)
            first = jnp.min(jnp.where(gate == top, jdx_f, float(nb)), axis=0, keepdims=True)
            pick = jdx_f == first
            chosen = chosen | pick
            gate = jnp.where(pick, -jnp.inf, gate)
        sel_ref[h] = jnp.where(chosen & past, 0.0, MASKED)
    km_ref[...] = jnp.where(blk == i, jnp.mean(kn, axis=0, keepdims=True), km)


def _moba_prep(patt, qg, kg, cg, batch, seq):
    n = patt.shape[0]
    nb = seq // MOBA_BLOCK
    Wd, H, BLK = ATT_WIDTH, ATT_HEADS, MOBA_BLOCK
    return pl.pallas_call(
        _moba_prep_kernel,
        grid=(batch, nb),
        in_specs=[pl.BlockSpec((BLK, 4 * Wd), lambda b, i: (b * nb + i, 0)),
                  _resident(), _resident(), _resident()],
        out_specs=[pl.BlockSpec((None, Wd, BLK), lambda b, i: (b, 0, i)),
                   pl.BlockSpec((BLK, Wd), lambda b, i: (b * nb + i, 0)),
                   pl.BlockSpec((None, None, H * ATT_VROWS, BLK), lambda b, i: (b, i, 0, 0)),
                   pl.BlockSpec((None, Wd, BLK), lambda b, i: (b, 0, i)),
                   pl.BlockSpec((None, H, nb, BLK), lambda b, i: (b, 0, 0, i))],
        out_shape=[jax.ShapeDtypeStruct((batch, Wd, seq), BF16),
                   jax.ShapeDtypeStruct((n, Wd), BF16),
                   jax.ShapeDtypeStruct((batch, nb, H * ATT_VROWS, BLK), BF16),
                   jax.ShapeDtypeStruct((batch, Wd, seq), BF16),
                   jax.ShapeDtypeStruct((batch, H, nb, seq), F32)],
        scratch_shapes=[pltpu.VMEM((nb, Wd), F32)],
        compiler_params=pltpu.CompilerParams(dimension_semantics=("arbitrary", "arbitrary")),
        name="moba_prep",
    )(patt, qg, kg, cg)


def _alibi_slope(h):
    return 2.0 ** (-8.0 * (h + 1) / ATT_HEADS)


def _moba_attn_kernel(qt_ref, cqt_ref, k_ref, vt_ref, sel_ref, km_ref, vmt_ref, y_ref,
                      m_ref, acc_ref, s_ref, kaug_ref, rhs_ref):
    H, Dh, BLK = ATT_HEADS, ATT_HEAD_DIM, MOBA_BLOCK
    PAIR = 2 * Dh
    AUG = V7X_BF16_SUBLANES
    i = pl.program_id(1)
    pair_row_head = lax.broadcasted_iota(jnp.int32, (PAIR, BLK), 0) // Dh

    @pl.when((pl.program_id(0) == 0) & (i == 0))
    def _():
        lane = lax.broadcasted_iota(jnp.int32, (BLK, V7X_LANES), 1)
        key_pos = lax.broadcasted_iota(jnp.int32, (BLK, V7X_LANES), 0).astype(F32)
        kaug_ref[...] = jnp.where(lane < BIAS_TERMS, key_pos,
                                  jnp.where(lane < 2 * BIAS_TERMS, 1.0, 0.0)).astype(BF16)
        rhs_ref[...] = jnp.zeros_like(rhs_ref)

    def pair_lanes(h):
        return slice((h // 2) * PAIR, (h // 2 + 1) * PAIR)

    def value_rows(h):
        return slice(h * ATT_VROWS, (h + 1) * ATT_VROWS)

    def normalised(acc):
        return acc[0:Dh] / acc[Dh:Dh + 1]

    def head_queries(src_ref, h):
        qp = src_ref[pair_lanes(h), :]
        return jnp.where(pair_row_head == h % 2, qp, jnp.zeros_like(qp))

    for h in range(H):
        rhs_ref[h, 0:PAIR, :] = head_queries(qt_ref, h)

    aug_row = lax.broadcasted_iota(jnp.int32, (AUG, BLK), 0)
    query_pos = lax.broadcasted_iota(jnp.int32, (1, BLK), 1).astype(F32)

    def scores(blk, own):
        r0 = pl.multiple_of(blk * BLK, BLK)
        distance = (i - blk).astype(F32) * BLK
        keys_aug = kaug_ref[...]
        out = []
        for h in range(H):
            c = _alibi_slope(h) * LOG2_E
            row_bias = -c * (query_pos + distance)
            if not own:
                row_bias = row_bias + sel_ref[h, pl.ds(blk, 1), :]
            rows = _bf16_terms_const(c) + _bf16_terms(row_bias)
            aug = jnp.zeros((AUG, BLK), F32)
            for n, val in enumerate(rows):
                aug = jnp.where(aug_row == n, val, aug)
            rhs_ref[h, PAIR:PAIR + AUG, :] = aug.astype(BF16)
            keys = jnp.concatenate([k_ref[pl.ds(r0, BLK), pair_lanes(h)], keys_aug], axis=1)
            out.append(_dot(keys, rhs_ref[h]))
        return out

    own_scores = scores(i, True)
    mem_scores = [_dot(km_ref[:, pair_lanes(h)], head_queries(cqt_ref, h)) for h in range(H)]
    first_past = scores(0, False)
    causal = (lax.broadcasted_iota(jnp.int32, (BLK, BLK), 0) <= lax.broadcasted_iota(jnp.int32, (BLK, BLK), 1))
    own_p, mem_p = [], []
    for h in range(H):
        s_t = jnp.where(causal, own_scores[h], MASKED)
        m = jnp.max(s_t, axis=0, keepdims=True)
        p = jnp.exp2(s_t - m)
        m_ref[h] = m
        own_p.append(p.astype(BF16))
    for h in range(H):
        p = jnp.exp2(mem_scores[h] - jnp.max(mem_scores[h], axis=0, keepdims=True))
        mem_p.append(p.astype(BF16))
    for h in range(H):
        acc_ref[h] = _dot(vt_ref[i, value_rows(h), :], own_p[h])
        s_ref[0, h] = first_past[h]
    for h in range(H):
        acc_ref[H + h] = _dot(vmt_ref[value_rows(h), :], mem_p[h])

    def block_step(j, slot, nxt):
        next_scores = None if nxt is None else scores(nxt, False)
        for h in range(H):
            s_t = s_ref[slot, h]
            m_old = m_ref[h]
            m_new = jnp.maximum(m_old, jnp.max(s_t, axis=0, keepdims=True))
            p = jnp.exp2(s_t - m_new)
            alpha = jnp.exp2(m_old - m_new)
            acc_ref[h] = alpha * acc_ref[h] + _dot(vt_ref[j, value_rows(h), :], p.astype(BF16))
            m_ref[h] = m_new
        if next_scores is not None:
            for h in range(H):
                s_ref[1 - slot, h] = next_scores[h]

    last_past = jnp.maximum(i - 1, 0)

    def block_run(j0, count):
        for n in range(count):
            block_step(j0 + n, n % 2, jnp.minimum(j0 + n + 1, last_past))

    def block_quad(jj, carry):
        block_run(4 * jj, 4)
        return carry

    lax.fori_loop(0, i // 4, block_quad, 0)

    @pl.when(i % 4 >= 2)
    def _():
        block_run(4 * (i // 4), 2)

    @pl.when(i % 2 == 1)
    def _():
        block_step(i - 1, 0, None)

    out_t = jnp.concatenate([normalised(acc_ref[a]) for a in range(2 * H)], axis=0)
    y_ref[...] = out_t.T.astype(BF16)


def _moba_attn(qt, cqt, kn, vt, sel, kmem, vmem_t, batch, seq):
    nb = seq // MOBA_BLOCK
    Wd, H, BLK = ATT_WIDTH, ATT_HEADS, MOBA_BLOCK
    n_mem = kmem.shape[1]
    return pl.pallas_call(
        _moba_attn_kernel,
        grid=(batch, nb),
        in_specs=[pl.BlockSpec((None, Wd, BLK), lambda b, i: (b, 0, i)),
                  pl.BlockSpec((None, Wd, BLK), lambda b, i: (b, 0, i)),
                  pl.BlockSpec((seq, Wd), lambda b, i: (b, 0)),
                  pl.BlockSpec((None, nb, H * ATT_VROWS, BLK), lambda b, i: (b, 0, 0, 0)),
                  pl.BlockSpec((None, H, nb, BLK), lambda b, i: (b, 0, 0, i)),
                  pl.BlockSpec((None, n_mem, Wd), lambda b, i: (b, 0, 0)),
                  pl.BlockSpec((None, H * ATT_VROWS, n_mem), lambda b, i: (b, 0, 0))],
        out_specs=pl.BlockSpec((BLK, 2 * Wd), lambda b, i: (b * nb + i, 0)),
        out_shape=jax.ShapeDtypeStruct((batch * seq, 2 * Wd), BF16),
        scratch_shapes=[pltpu.VMEM((H, 1, BLK), F32),
                        pltpu.VMEM((2 * H, ATT_VROWS, BLK), F32),
                        pltpu.VMEM((2, H, BLK, BLK), F32),
                        pltpu.VMEM((BLK, V7X_LANES), BF16),
                        pltpu.VMEM((H, 2 * ATT_HEAD_DIM + V7X_LANES, BLK), BF16)],
        compiler_params=pltpu.CompilerParams(dimension_semantics=("arbitrary", "arbitrary"),
                                             vmem_limit_bytes=V7X_VMEM_LIMIT_BYTES),
        name="moba_attn",
    )(qt, cqt, kn, vt, sel, kmem, vmem_t)


def _lane_vector(values, width):
    return jnp.zeros((1, width), F32).at[0, :values.shape[0]].set(values.astype(F32))


def _layer(x2d, mem, batch, seq, ffn1_norm, ffn1_w_gate, ffn1_w_up, ffn1_w_down, mix_norm, w_in,
           gdn_conv_w, gdn_a_log, gdn_dt_bias, gdn_out_norm, moba_q_norm, moba_k_norm,
           mem_norm, w_mem_kv, mem_q_norm, mem_k_norm, w_out, ffn2_norm, ffn2_w_gate, ffn2_w_up, ffn2_w_down):
    W, H = GDN_WIDTH, GDN_HEADS
    row = lambda v: v.reshape(1, -1).astype(F32)
    tile_heads = lambda v: jnp.tile(v.astype(F32), ATT_HEADS).reshape(1, -1)
    ab0 = 4 * W
    att0 = ab0 + 2 * H
    w_gdn = w_in[:, :ab0].astype(BF16)
    ab_width = V7X_LANES
    w_ab = jnp.zeros((w_in.shape[0], ab_width), F32).at[:, :2 * H].set(w_in[:, ab0:att0]).astype(BF16)
    w_att = w_in[:, att0:].astype(BF16)

    x1, pg, pab, patt = _ffn1_inproj(
        x2d, row(ffn1_norm), ffn1_w_gate.astype(BF16), ffn1_w_up.astype(BF16), ffn1_w_down.astype(BF16),
        row(mix_norm), w_gdn, w_ab, w_att)

    y_gdn = _gdn(pg, pab, gdn_conv_w.astype(F32), _lane_vector(gdn_a_log, ab_width),
                 _lane_vector(gdn_dt_bias, ab_width), row(gdn_out_norm), batch, seq)

    kmem, vmem_t = _mem_kv(mem, row(mem_norm), w_mem_kv.astype(BF16), tile_heads(mem_k_norm))
    qt, kn, vt, cqt, sel = _moba_prep(patt, tile_heads(moba_q_norm), tile_heads(moba_k_norm),
                                      tile_heads(mem_q_norm), batch, seq)
    y_att = _moba_attn(qt, cqt, kn, vt, sel, kmem, vmem_t, batch, seq)

    w_out_b = w_out.astype(BF16)
    return _outproj_ffn2(x1, y_gdn, y_att, w_out_b[:W], w_out_b[W:], row(ffn2_norm),
                         ffn2_w_gate.astype(BF16), ffn2_w_up.astype(BF16), ffn2_w_down.astype(BF16))


def kernel(x, mem, ffn1_norm, ffn1_w_gate, ffn1_w_up, ffn1_w_down, mix_norm, w_in, gdn_conv_w, gdn_a_log,
           gdn_dt_bias, gdn_out_norm, moba_q_norm, moba_k_norm, mem_norm, w_mem_kv, mem_q_norm, mem_k_norm,
           w_out, ffn2_norm, ffn2_w_gate, ffn2_w_up, ffn2_w_down):
    batch, seq, d = x.shape
    assert seq % MOBA_BLOCK == 0 and seq % GDN_ROWS == 0 and (batch * seq) % FFN_ROWS == 0
    depth = w_in.shape[0]
    x2d = x.reshape(batch * seq, d)
    for l in range(depth):
        x2d = _layer(x2d, mem, batch, seq, ffn1_norm[l], ffn1_w_gate[l], ffn1_w_up[l], ffn1_w_down[l],
                     mix_norm[l], w_in[l], gdn_conv_w[l], gdn_a_log[l], gdn_dt_bias[l], gdn_out_norm[l],
                     moba_q_norm[l], moba_k_norm[l], mem_norm[l], w_mem_kv[l], mem_q_norm[l], mem_k_norm[l],
                     w_out[l], ffn2_norm[l], ffn2_w_gate[l], ffn2_w_up[l], ffn2_w_down[l])
    return x2d.reshape(batch, seq, d)
```

```python
import struct

import jax
import jax.numpy as jnp
from jax import lax
from jax.experimental import pallas as pl
from jax.experimental.pallas import tpu as pltpu

F32 = jnp.float32
BF16 = jnp.bfloat16
HIGHEST = lax.Precision.HIGHEST

NORM_EPS = 1e-6
D_FF = 2816
GDN_HEADS = 4
GDN_HEAD_DIM = 128
GDN_WIDTH = GDN_HEADS * GDN_HEAD_DIM
GDN_CONV = 4
GDN_CHUNK = 64
GDN_GROUP = 2 * GDN_CHUNK
ATT_HEADS = 4
ATT_HEAD_DIM = 64
ATT_WIDTH = ATT_HEADS * ATT_HEAD_DIM
MOBA_BLOCK = 256
MOBA_TOPK = 3
MASKED = -1e30
LOG2_E = 1.4426950408889634
BIAS_TERMS = 3

V7X_VMEM_LIMIT_BYTES = 56 * 1024 * 1024
V7X_LANES = 128
V7X_BF16_SUBLANES = 16
ATT_VROWS = ATT_HEAD_DIM + V7X_BF16_SUBLANES
FFN_ROWS = 256
FFN1_SLABS = 2
FFN2_SLABS = 4
GDN_ROWS = 512


def _dot(a, b, precision=None):
    return jnp.dot(a, b, preferred_element_type=F32, precision=precision)


def _dot_nt(a, b, precision=None):
    return lax.dot_general(a, b, (((1,), (1,)), ((), ())), preferred_element_type=F32,
                           precision=precision)


def _rms_rows(x, gain):
    return x * lax.rsqrt(jnp.mean(x * x, axis=-1, keepdims=True) + NORM_EPS) * gain


def _silu(x):
    half = 0.5 * x
    return half + half * jnp.tanh(half)


def _swiglu(x, gain, wg_ref, wu_ref, wd_ref):
    h = _rms_rows(x, gain).astype(BF16)
    g = _dot(h, wg_ref[...])
    u = _dot(h, wu_ref[...])
    return _dot((_silu(g) * u).astype(BF16), wd_ref[...])


def _resident():
    return pl.BlockSpec(memory_space=pltpu.VMEM)


def _bf16_terms(x):
    terms, rest = [], x
    for _ in range(BIAS_TERMS):
        t = rest.astype(BF16).astype(F32)
        terms.append(t)
        rest = rest - t
    return terms


def _bf16_terms_const(value):
    def f32(v):
        return struct.unpack("<f", struct.pack("<f", v))[0]

    def bf16_round(v):
        bits = struct.unpack("<I", struct.pack("<f", v))[0]
        bits = (bits + 0x7FFF + ((bits >> 16) & 1)) & 0xFFFF0000
        return struct.unpack("<f", struct.pack("<I", bits))[0]

    terms, rest = [], f32(value)
    for _ in range(BIAS_TERMS):
        t = bf16_round(rest)
        terms.append(t)
        rest = f32(rest - t)
    return terms


def _ffn1_inproj_kernel(x_ref, n1_ref, wg_ref, wu_ref, wd_ref, n2_ref, win_ref,
                        x1_ref, pg_ref, pab_ref, patt_ref):
    gdn_w, ab_w = pg_ref.shape[1], pab_ref.shape[1]
    for r0 in range(0, x_ref.shape[0], FFN_ROWS):
        rows = slice(r0, r0 + FFN_ROWS)
        x = x_ref[rows, :]
        x1 = x + 0.5 * _swiglu(x, n1_ref[...], wg_ref, wu_ref, wd_ref)
        x1_ref[rows, :] = x1
        h = _rms_rows(x1, n2_ref[...]).astype(BF16)
        pg_ref[rows, :] = _dot(h, win_ref[:, 0:gdn_w])
        pab_ref[rows, :] = _dot(h, win_ref[:, gdn_w:gdn_w + ab_w])
        patt_ref[rows, :] = _dot(h, win_ref[:, gdn_w + ab_w:])


def _ffn1_inproj(x2d, n1, wg, wu, wd, n2, win, gdn_w, ab_w):
    n, d = x2d.shape
    att_w = win.shape[1] - gdn_w - ab_w
    rows = FFN1_SLABS * FFN_ROWS
    row_spec = lambda w: pl.BlockSpec((rows, w), lambda i: (i, 0))
    return pl.pallas_call(
        _ffn1_inproj_kernel,
        grid=(n // rows,),
        in_specs=[row_spec(d)] + [_resident()] * 6,
        out_specs=[row_spec(d), row_spec(gdn_w), row_spec(ab_w), row_spec(att_w)],
        out_shape=[jax.ShapeDtypeStruct((n, d), F32),
                   jax.ShapeDtypeStruct((n, gdn_w), F32),
                   jax.ShapeDtypeStruct((n, ab_w), F32),
                   jax.ShapeDtypeStruct((n, att_w), F32)],
        compiler_params=pltpu.CompilerParams(dimension_semantics=("arbitrary",),
                                             vmem_limit_bytes=V7X_VMEM_LIMIT_BYTES),
        name="ffn1_inproj",
    )(x2d, n1, wg, wu, wd, n2, win)


def _outproj_ffn2_kernel(x1_ref, yg_ref, ya_ref, wo_ref, n_ref, wg_ref, wu_ref, wd_ref, o_ref):
    gdn_w = yg_ref.shape[1]
    for r0 in range(0, x1_ref.shape[0], FFN_ROWS):
        rows = slice(r0, r0 + FFN_ROWS)
        x2 = (x1_ref[rows, :] + _dot(yg_ref[rows, :], wo_ref[0:gdn_w, :])
              + _dot(ya_ref[rows, :], wo_ref[gdn_w:, :]))
        o_ref[rows, :] = x2 + 0.5 * _swiglu(x2, n_ref[...], wg_ref, wu_ref, wd_ref)


def _outproj_ffn2(x1, yg, ya, wo, nrm, wg, wu, wd):
    n, d = x1.shape
    rows = FFN2_SLABS * FFN_ROWS
    row_spec = lambda w: pl.BlockSpec((rows, w), lambda i: (i, 0))
    return pl.pallas_call(
        _outproj_ffn2_kernel,
        grid=(n // rows,),
        in_specs=[row_spec(d), row_spec(yg.shape[1]), row_spec(ya.shape[1])] + [_resident()] * 5,
        out_specs=row_spec(d),
        out_shape=jax.ShapeDtypeStruct((n, d), F32),
        compiler_params=pltpu.CompilerParams(dimension_semantics=("arbitrary",),
                                             vmem_limit_bytes=V7X_VMEM_LIMIT_BYTES),
        name="outproj_ffn2",
    )(x1, yg, ya, wo, nrm, wg, wu, wd)


def _head_block_ones(width, head_dim):
    r = lax.broadcasted_iota(jnp.int32, (width, width), 0) // head_dim
    c = lax.broadcasted_iota(jnp.int32, (width, width), 1) // head_dim
    return jnp.where(r == c, 1.0, 0.0).astype(BF16)


def _head_rms(x, gain, block_ones, head_dim):
    sq = x * x
    hi = sq.astype(BF16)
    lo = (sq - hi.astype(F32)).astype(BF16)
    ss = _dot(hi, block_ones) + _dot(lo, block_ones)
    return x * lax.rsqrt(ss * (1.0 / head_dim) + NORM_EPS) * gain


def _values_with_ones(v_t):
    Dh = ATT_HEAD_DIM
    row = lax.broadcasted_iota(jnp.int32, (V7X_BF16_SUBLANES, v_t.shape[1]), 0)
    ones_block = jnp.where(row == 0, 1.0, 0.0)
    parts = []
    for h in range(ATT_HEADS):
        parts += [v_t[h * Dh:(h + 1) * Dh], ones_block]
    return jnp.concatenate(parts, axis=0).astype(BF16)


def _mem_kv_kernel(mem_ref, nrm_ref, wkv_ref, kg_ref, k_ref, vt_ref):
    h = _rms_rows(mem_ref[...], nrm_ref[...]).astype(BF16)
    kv = _dot(h, wkv_ref[...])
    k = _head_rms(kv[:, :ATT_WIDTH], kg_ref[...], _head_block_ones(ATT_WIDTH, ATT_HEAD_DIM), ATT_HEAD_DIM)
    k_ref[...] = k.astype(BF16)
    vt_ref[...] = _values_with_ones(kv[:, ATT_WIDTH:].T)


def _mem_kv(mem, nrm, wkv, kgain):
    b, m, d = mem.shape
    return pl.pallas_call(
        _mem_kv_kernel,
        grid=(b,),
        in_specs=[pl.BlockSpec((None, m, d), lambda i: (i, 0, 0)), _resident(), _resident(), _resident()],
        out_specs=[pl.BlockSpec((None, m, ATT_WIDTH), lambda i: (i, 0, 0)),
                   pl.BlockSpec((None, ATT_HEADS * ATT_VROWS, m), lambda i: (i, 0, 0))],
        out_shape=[jax.ShapeDtypeStruct((b, m, ATT_WIDTH), BF16),
                   jax.ShapeDtypeStruct((b, ATT_HEADS * ATT_VROWS, m), BF16)],
        compiler_params=pltpu.CompilerParams(dimension_semantics=("arbitrary",)),
        name="mem_kv",
    )(mem, nrm, wkv, kgain)


def _softplus(x):
    return jnp.maximum(x, 0.0) + jnp.log(1.0 + jnp.exp(-jnp.abs(x)))


def _unit_lower_inverses(a_list, eye):
    x16 = [(-a).astype(BF16) for a in a_list]
    t = [eye - a for a in a_list]
    for _ in range(5):
        x16 = [_dot(xb, xb).astype(BF16) for xb in x16]
        t = [ti + _dot(ti.astype(BF16), xb) for ti, xb in zip(t, x16)]
    t16 = [ti.astype(BF16) for ti in t]
    a_hi = [a.astype(BF16) for a in a_list]
    a_lo = [(a - hi.astype(F32)).astype(BF16) for a, hi in zip(a_list, a_hi)]
    resid = [(eye - tb.astype(F32) - _dot(hi, tb) - _dot(lo, tb)).astype(BF16)
             for tb, hi, lo in zip(t16, a_hi, a_lo)]
    return [tb.astype(F32) + _dot(tb, r) for tb, r in zip(t16, resid)]


def _gdn_kernel(qkv_ref, z_ref, ab_ref, cw_ref, alog_ref, dtb_ref, og_ref, y_ref, xbuf_ref, s_ref):
    rows = z_ref.shape[0]
    H, Dh, C, W = GDN_HEADS, GDN_HEAD_DIM, GDN_CHUNK, GDN_WIDTH
    G = GDN_GROUP
    n_chunks = rows // C
    n_groups = rows // G
    t = pl.program_id(1)

    @pl.when(t == 0)
    def _():
        xbuf_ref[0:8, :] = jnp.zeros((8, 3 * W), F32)
        s_ref[...] = jnp.zeros_like(s_ref)

    xbuf_ref[8:, :] = qkv_ref[...]
    conv = cw_ref[0:1, :] * xbuf_ref[5:5 + rows, :]
    for j in range(1, GDN_CONV):
        conv = conv + cw_ref[j:j + 1, :] * xbuf_ref[5 + j:5 + j + rows, :]
    xbuf_ref[0:8, :] = xbuf_ref[rows:rows + 8, :]
    act = _silu(conv)

    ab = ab_ref[...]
    log_decay = -jnp.exp(alog_ref[...]) * _softplus(ab + dtb_ref[...])
    beta_all = jax.nn.sigmoid(ab)
    rt = lax.broadcasted_iota(jnp.int32, (rows, rows), 0)
    ct = lax.broadcasted_iota(jnp.int32, (rows, rows), 1)
    chunk_lower = (((rt // C) == (ct // C)) & (rt >= ct)).astype(F32)
    gc_all = _dot(chunk_lower, log_decay, HIGHEST)
    gc_rows = gc_all.T

    ri = lax.broadcasted_iota(jnp.int32, (G, G), 0)
    ci = lax.broadcasted_iota(jnp.int32, (G, G), 1)
    same_chunk = (ri // C) == (ci // C)
    lower = same_chunk & (ri >= ci)
    strict = same_chunk & (ri > ci)
    eye = (ri == ci).astype(F32)

    units = [(h, g) for h in range(H) for g in range(n_groups)]
    q_h, k_h, v_h, gcol_h, beta_h = [], [], [], [], []
    for h in range(H):
        qh = act[:, h * Dh:(h + 1) * Dh]
        kh = act[:, W + h * Dh:W + (h + 1) * Dh]
        v_h.append(act[:, 2 * W + h * Dh:2 * W + (h + 1) * Dh])
        q_h.append(qh * (lax.rsqrt(jnp.sum(qh * qh, axis=-1, keepdims=True) + NORM_EPS) * Dh ** -0.5))
        k_h.append(kh * lax.rsqrt(jnp.sum(kh * kh, axis=-1, keepdims=True) + NORM_EPS))
        gcol_h.append(jnp.broadcast_to(gc_all[:, h:h + 1], (rows, Dh)))
        beta_h.append(jnp.broadcast_to(beta_all[:, H + h:H + h + 1], (rows, Dh)))

    def unit_rows(x, g):
        return x[g * G:(g + 1) * G]

    k_u = [unit_rows(k_h[h], g) for h, g in units]
    gcol_u = [unit_rows(gcol_h[h], g) for h, g in units]
    kb_u = [k * unit_rows(beta_h[h], g) for k, (h, g) in zip(k_u, units)]
    k16_u = [k.astype(BF16) for k in k_u]
    q_u = [unit_rows(q_h[h], g) for h, g in units]
    kk_u = [_dot_nt(kb.astype(BF16), k16) for kb, k16 in zip(kb_u, k16_u)]
    qk_u = [_dot_nt(q.astype(BF16), k16) for q, k16 in zip(q_u, k16_u)]
    decay_u = []
    for gcol, (h, g) in zip(gcol_u, units):
        diff = gcol - gc_rows[h:h + 1, g * G:(g + 1) * G]
        decay_u.append(jnp.where(lower, jnp.exp(jnp.where(lower, diff, 0.0)), 0.0))
    a_u = [jnp.where(strict, kk * decay, 0.0) for kk, decay in zip(kk_u, decay_u)]
    qk16_u = [(qk * decay).astype(BF16) for qk, decay in zip(qk_u, decay_u)]
    tinv_u = _unit_lower_inverses(a_u, eye)
    egc_u = [jnp.exp(gcol) for gcol in gcol_u]
    rhs_u = [jnp.concatenate([unit_rows(v_h[h], g) * unit_rows(beta_h[h], g), kb * egc], axis=1).astype(BF16)
             for kb, egc, (h, g) in zip(kb_u, egc_u, units)]
    uw_u = [_dot(tinv.astype(BF16), rhs) for tinv, rhs in zip(tinv_u, rhs_u)]
    w16_u = [uw[:, Dh:].astype(BF16) for uw in uw_u]
    qe16_u = [(q * egc).astype(BF16) for q, egc in zip(q_u, egc_u)]

    def chunk_operands(h, c):
        u = h * n_groups + (c * C) // G
        r = slice((c * C) % G, (c * C) % G + C)
        g_last = gcol_u[u][r.stop - 1:r.stop, :]
        return (uw_u[u][r, :Dh],
                jnp.concatenate([w16_u[u][r], qe16_u[u][r]], axis=0),
                qk16_u[u][r, r],
                (k_u[u][r] * jnp.exp(g_last - gcol_u[u][r])).T.astype(BF16),
                jnp.exp(g_last))

    ops = [[chunk_operands(h, c) for c in range(n_chunks)] for h in range(H)]

    state = [s_ref[h] for h in range(H)]
    o_l = [[] for _ in range(H)]
    for c in range(n_chunks):
        ws = [_dot(ops[h][c][1], state[h].astype(BF16)) for h in range(H)]
        v_new = [(ops[h][c][0] - ws[h][:C]).astype(BF16) for h in range(H)]
        state = [state[h] * ops[h][c][4] + _dot(ops[h][c][3], v_new[h]) for h in range(H)]
        for h in range(H):
            o_l[h].append(ws[h][C:] + _dot(ops[h][c][2], v_new[h]))
    for h in range(H):
        s_ref[h] = state[h]

    z = z_ref[...]
    for h in range(H):
        cols = slice(h * Dh, (h + 1) * Dh)
        o = jnp.concatenate(o_l[h], axis=0)
        y_ref[:, cols] = (_rms_rows(o, og_ref[...]) * _silu(z[:, cols])).astype(BF16)


def _gdn(pg, pab, conv_w, alog_vec, dtb_vec, out_gain, batch, seq):
    n = pg.shape[0]
    rows = GDN_ROWS
    tiles = seq // rows
    W = GDN_WIDTH
    return pl.pallas_call(
        _gdn_kernel,
        grid=(batch, tiles),
        in_specs=[pl.BlockSpec((rows, 3 * W), lambda b, t: (b * tiles + t, 0)),
                  pl.BlockSpec((rows, W), lambda b, t: (b * tiles + t, 3)),
                  pl.BlockSpec((rows, pab.shape[1]), lambda b, t: (b * tiles + t, 0)),
                  _resident(), _resident(), _resident(), _resident()],
        out_specs=pl.BlockSpec((rows, W), lambda b, t: (b * tiles + t, 0)),
        out_shape=jax.ShapeDtypeStruct((n, W), BF16),
        scratch_shapes=[pltpu.VMEM((rows + 8, 3 * W), F32),
                        pltpu.VMEM((GDN_HEADS, GDN_HEAD_DIM, GDN_HEAD_DIM), F32)],
        compiler_params=pltpu.CompilerParams(dimension_semantics=("arbitrary", "arbitrary"),
                                             vmem_limit_bytes=V7X_VMEM_LIMIT_BYTES),
        name="gdn",
    )(pg, pg, pab, conv_w, alog_vec, dtb_vec, out_gain)


def _moba_prep_kernel(p_ref, qg_ref, kg_ref, cg_ref, qt_ref, kn_ref, vt_ref, cqt_ref, sel_ref, km_ref):
    H, Dh, Wd = ATT_HEADS, ATT_HEAD_DIM, ATT_WIDTH
    nb = km_ref.shape[0]
    i = pl.program_id(1)

    @pl.when(i == 0)
    def _():
        km_ref[...] = jnp.zeros_like(km_ref)

    block_ones = _head_block_ones(Wd, Dh)
    qn = _head_rms(p_ref[:, 0:Wd], qg_ref[...], block_ones, Dh)
    kn = _head_rms(p_ref[:, Wd:2 * Wd], kg_ref[...], block_ones, Dh)
    cqn = _head_rms(p_ref[:, 3 * Wd:4 * Wd], cg_ref[...], block_ones, Dh)
    scale = Dh ** -0.5 * LOG2_E
    qt_ref[...] = (qn * scale).T.astype(BF16)
    cqt_ref[...] = (cqn * scale).T.astype(BF16)
    kn_ref[...] = kn.astype(BF16)
    vt_ref[...] = _values_with_ones(p_ref[:, 2 * Wd:3 * Wd].T)

    km = km_ref[...]
    blk = lax.broadcasted_iota(jnp.int32, (nb, Wd), 0)
    lane_head = lax.broadcasted_iota(jnp.int32, (nb, Wd), 1) // Dh
    km_heads = jnp.concatenate([jnp.where(lane_head == h, km, 0.0) for h in range(H)], axis=0)
    gate_all = _dot_nt(km_heads, qn, HIGHEST)
    jdx = lax.broadcasted_iota(jnp.int32, (nb, MOBA_BLOCK), 0)
    jdx_f = jdx.astype(F32)
    past = jdx < i
    for h in range(H):
        gate = jnp.where(past, gate_all[h * nb:(h + 1) * nb], -jnp.inf)
        chosen = jnp.zeros(gate.shape, jnp.bool_)
        for _ in range(MOBA_TOPK):
            top = jnp.max(gate, axis=0, keepdims=True)
            first = jnp.min(jnp.where(gate == top, jdx_f, float(nb)), axis=0, keepdims=True)
            pick = jdx_f == first
            chosen = chosen | pick
            gate = jnp.where(pick, -jnp.inf, gate)
        sel_ref[h] = jnp.where(chosen & past, 0.0, MASKED)
    km_ref[...] = jnp.where(blk == i, jnp.mean(kn, axis=0, keepdims=True), km)


def _moba_prep(patt, qg, kg, cg, batch, seq):
    n = patt.shape[0]
    nb = seq // MOBA_BLOCK
    Wd, H, BLK = ATT_WIDTH, ATT_HEADS, MOBA_BLOCK
    return pl.pallas_call(
        _moba_prep_kernel,
        grid=(batch, nb),
        in_specs=[pl.BlockSpec((BLK, 4 * Wd), lambda b, i: (b * nb + i, 0)),
                  _resident(), _resident(), _resident()],
        out_specs=[pl.BlockSpec((None, Wd, BLK), lambda b, i: (b, 0, i)),
                   pl.BlockSpec((BLK, Wd), lambda b, i: (b * nb + i, 0)),
                   pl.BlockSpec((None, None, H * ATT_VROWS, BLK), lambda b, i: (b, i, 0, 0)),
                   pl.BlockSpec((None, Wd, BLK), lambda b, i: (b, 0, i)),
                   pl.BlockSpec((None, H, nb, BLK), lambda b, i: (b, 0, 0, i))],
        out_shape=[jax.ShapeDtypeStruct((batch, Wd, seq), BF16),
                   jax.ShapeDtypeStruct((n, Wd), BF16),
                   jax.ShapeDtypeStruct((batch, nb, H * ATT_VROWS, BLK), BF16),
                   jax.ShapeDtypeStruct((batch, Wd, seq), BF16),
                   jax.ShapeDtypeStruct((batch, H, nb, seq), F32)],
        scratch_shapes=[pltpu.VMEM((nb, Wd), F32)],
        compiler_params=pltpu.CompilerParams(dimension_semantics=("arbitrary", "arbitrary")),
        name="moba_prep",
    )(patt, qg, kg, cg)


def _alibi_slope(h):
    return 2.0 ** (-8.0 * (h + 1) / ATT_HEADS)


def _moba_attn_kernel(qt_ref, cqt_ref, k_ref, vt_ref, sel_ref, km_ref, vmt_ref, y_ref,
                      m_ref, acc_ref, s_ref, kaug_ref, rhs_ref):
    H, Dh, BLK = ATT_HEADS, ATT_HEAD_DIM, MOBA_BLOCK
    PAIR = 2 * Dh
    AUG = V7X_BF16_SUBLANES
    i = pl.program_id(1)
    pair_row_head = lax.broadcasted_iota(jnp.int32, (PAIR, BLK), 0) // Dh

    @pl.when((pl.program_id(0) == 0) & (i == 0))
    def _():
        lane = lax.broadcasted_iota(jnp.int32, (BLK, V7X_LANES), 1)
        key_pos = lax.broadcasted_iota(jnp.int32, (BLK, V7X_LANES), 0).astype(F32)
        kaug_ref[...] = jnp.where(lane < BIAS_TERMS, key_pos,
                                  jnp.where(lane < 2 * BIAS_TERMS, 1.0, 0.0)).astype(BF16)
        rhs_ref[...] = jnp.zeros_like(rhs_ref)

    def pair_lanes(h):
        return slice((h // 2) * PAIR, (h // 2 + 1) * PAIR)

    def value_rows(h):
        return slice(h * ATT_VROWS, (h + 1) * ATT_VROWS)

    def normalised(acc):
        return acc[0:Dh] / acc[Dh:Dh + 1]

    def head_queries(src_ref, h):
        qp = src_ref[pair_lanes(h), :]
        return jnp.where(pair_row_head == h % 2, qp, jnp.zeros_like(qp))

    for h in range(H):
        rhs_ref[h, 0:PAIR, :] = head_queries(qt_ref, h)

    aug_row = lax.broadcasted_iota(jnp.int32, (AUG, BLK), 0)
    query_pos = lax.broadcasted_iota(jnp.int32, (1, BLK), 1).astype(F32)

    def scores(blk, own):
        r0 = pl.multiple_of(blk * BLK, BLK)
        distance = (i - blk).astype(F32) * BLK
        keys_aug = kaug_ref[...]
        out = []
        for h in range(H):
            c = _alibi_slope(h) * LOG2_E
            row_bias = -c * (query_pos + distance)
            if not own:
                row_bias = row_bias + sel_ref[h, pl.ds(blk, 1), :]
            rows = _bf16_terms_const(c) + _bf16_terms(row_bias)
            aug = jnp.zeros((AUG, BLK), F32)
            for n, val in enumerate(rows):
                aug = jnp.where(aug_row == n, val, aug)
            rhs_ref[h, PAIR:PAIR + AUG, :] = aug.astype(BF16)
            keys = jnp.concatenate([k_ref[pl.ds(r0, BLK), pair_lanes(h)], keys_aug], axis=1)
            out.append(_dot(keys, rhs_ref[h]))
        return out

    own_scores = scores(i, True)
    mem_scores = [_dot(km_ref[:, pair_lanes(h)], head_queries(cqt_ref, h)) for h in range(H)]
    first_past = scores(0, False)
    causal = (lax.broadcasted_iota(jnp.int32, (BLK, BLK), 0) <= lax.broadcasted_iota(jnp.int32, (BLK, BLK), 1))
    own_p, mem_p = [], []
    for h in range(H):
        s_t = jnp.where(causal, own_scores[h], MASKED)
        m = jnp.max(s_t, axis=0, keepdims=True)
        p = jnp.exp2(s_t - m)
        m_ref[h] = m
        own_p.append(p.astype(BF16))
    for h in range(H):
        p = jnp.exp2(mem_scores[h] - jnp.max(mem_scores[h], axis=0, keepdims=True))
        mem_p.append(p.astype(BF16))
    for h in range(H):
        acc_ref[h] = _dot(vt_ref[i, value_rows(h), :], own_p[h])
        s_ref[0, h] = first_past[h]
    for h in range(H):
        acc_ref[H + h] = _dot(vmt_ref[value_rows(h), :], mem_p[h])

    def block_step(j, slot, nxt):
        next_scores = None if nxt is None else scores(nxt, False)
        for h in range(H):
            s_t = s_ref[slot, h]
            m_old = m_ref[h]
            m_new = jnp.maximum(m_old, jnp.max(s_t, axis=0, keepdims=True))
            p = jnp.exp2(s_t - m_new)
            alpha = jnp.exp2(m_old - m_new)
            acc_ref[h] = alpha * acc_ref[h] + _dot(vt_ref[j, value_rows(h), :], p.astype(BF16))
            m_ref[h] = m_new
        if next_scores is not None:
            for h in range(H):
                s_ref[1 - slot, h] = next_scores[h]

    last_past = jnp.maximum(i - 1, 0)

    def block_run(j0, count):
        for n in range(count):
            block_step(j0 + n, n % 2, jnp.minimum(j0 + n + 1, last_past))

    def block_octet(jj, carry):
        block_run(8 * jj, 8)
        return carry

    lax.fori_loop(0, i // 8, block_octet, 0)

    @pl.when(i % 8 >= 4)
    def _():
        block_run(8 * (i // 8), 4)

    @pl.when(i % 4 >= 2)
    def _():
        block_run(4 * (i // 4), 2)

    @pl.when(i % 2 == 1)
    def _():
        block_step(i - 1, 0, None)

    out_t = jnp.concatenate([normalised(acc_ref[a]) for a in range(2 * H)], axis=0)
    y_ref[...] = out_t.T.astype(BF16)


def _moba_attn(qt, cqt, kn, vt, sel, kmem, vmem_t, batch, seq):
    nb = seq // MOBA_BLOCK
    Wd, H, BLK = ATT_WIDTH, ATT_HEADS, MOBA_BLOCK
    n_mem = kmem.shape[1]
    return pl.pallas_call(
        _moba_attn_kernel,
        grid=(batch, nb),
        in_specs=[pl.BlockSpec((None, Wd, BLK), lambda b, i: (b, 0, i)),
                  pl.BlockSpec((None, Wd, BLK), lambda b, i: (b, 0, i)),
                  pl.BlockSpec((seq, Wd), lambda b, i: (b, 0)),
                  pl.BlockSpec((None, nb, H * ATT_VROWS, BLK), lambda b, i: (b, 0, 0, 0)),
                  pl.BlockSpec((None, H, nb, BLK), lambda b, i: (b, 0, 0, i)),
                  pl.BlockSpec((None, n_mem, Wd), lambda b, i: (b, 0, 0)),
                  pl.BlockSpec((None, H * ATT_VROWS, n_mem), lambda b, i: (b, 0, 0))],
        out_specs=pl.BlockSpec((BLK, 2 * Wd), lambda b, i: (b * nb + i, 0)),
        out_shape=jax.ShapeDtypeStruct((batch * seq, 2 * Wd), BF16),
        scratch_shapes=[pltpu.VMEM((H, 1, BLK), F32),
                        pltpu.VMEM((2 * H, ATT_VROWS, BLK), F32),
                        pltpu.VMEM((2, H, BLK, BLK), F32),
                        pltpu.VMEM((BLK, V7X_LANES), BF16),
                        pltpu.VMEM((H, 2 * ATT_HEAD_DIM + V7X_LANES, BLK), BF16)],
        compiler_params=pltpu.CompilerParams(dimension_semantics=("arbitrary", "arbitrary"),
                                             vmem_limit_bytes=V7X_VMEM_LIMIT_BYTES),
        name="moba_attn",
    )(qt, cqt, kn, vt, sel, kmem, vmem_t)


def _lane_vector(values, width):
    return jnp.zeros((1, width), F32).at[0, :values.shape[0]].set(values.astype(F32))


def _layer(x2d, mem, batch, seq, ffn1_norm, ffn1_w_gate, ffn1_w_up, ffn1_w_down, mix_norm, w_in,
           gdn_conv_w, gdn_a_log, gdn_dt_bias, gdn_out_norm, moba_q_norm, moba_k_norm,
           mem_norm, w_mem_kv, mem_q_norm, mem_k_norm, w_out, ffn2_norm, ffn2_w_gate, ffn2_w_up, ffn2_w_down):
    W, H = GDN_WIDTH, GDN_HEADS
    row = lambda v: v.reshape(1, -1).astype(F32)
    tile_heads = lambda v: jnp.tile(v.astype(F32), ATT_HEADS).reshape(1, -1)
    ab0 = 4 * W
    att0 = ab0 + 2 * H
    ab_width = V7X_LANES
    w_in_sections = jnp.concatenate(
        [w_in[:, :ab0], w_in[:, ab0:att0], jnp.zeros((w_in.shape[0], ab_width - 2 * H), w_in.dtype),
         w_in[:, att0:]], axis=1).astype(BF16)

    x1, pg, pab, patt = _ffn1_inproj(
        x2d, row(ffn1_norm), ffn1_w_gate.astype(BF16), ffn1_w_up.astype(BF16), ffn1_w_down.astype(BF16),
        row(mix_norm), w_in_sections, ab0, ab_width)

    y_gdn = _gdn(pg, pab, gdn_conv_w.astype(F32), _lane_vector(gdn_a_log, ab_width),
                 _lane_vector(gdn_dt_bias, ab_width), row(gdn_out_norm), batch, seq)

    kmem, vmem_t = _mem_kv(mem, row(mem_norm), w_mem_kv.astype(BF16), tile_heads(mem_k_norm))
    qt, kn, vt, cqt, sel = _moba_prep(patt, tile_heads(moba_q_norm), tile_heads(moba_k_norm),
                                      tile_heads(mem_q_norm), batch, seq)
    y_att = _moba_attn(qt, cqt, kn, vt, sel, kmem, vmem_t, batch, seq)

    return _outproj_ffn2(x1, y_gdn, y_att, w_out.astype(BF16), row(ffn2_norm),
                         ffn2_w_gate.astype(BF16), ffn2_w_up.astype(BF16), ffn2_w_down.astype(BF16))


def kernel(x, mem, ffn1_norm, ffn1_w_gate, ffn1_w_up, ffn1_w_down, mix_norm, w_in, gdn_conv_w, gdn_a_log,
           gdn_dt_bias, gdn_out_norm, moba_q_norm, moba_k_norm, mem_norm, w_mem_kv, mem_q_norm, mem_k_norm,
           w_out, ffn2_norm, ffn2_w_gate, ffn2_w_up, ffn2_w_down):
    batch, seq, d = x.shape
    assert seq % MOBA_BLOCK == 0 and seq % GDN_ROWS == 0 and (batch * seq) % FFN_ROWS == 0
    depth = w_in.shape[0]
    x2d = x.reshape(batch * seq, d)
    for l in range(depth):
        x2d = _layer(x2d, mem, batch, seq, ffn1_norm[l], ffn1_w_gate[l], ffn1_w_up[l], ffn1_w_down[l],
                     mix_norm[l], w_in[l], gdn_conv_w[l], gdn_a_log[l], gdn_dt_bias[l], gdn_out_norm[l],
                     moba_q_norm[l], moba_k_norm[l], mem_norm[l], w_mem_kv[l], mem_q_norm[l], mem_k_norm[l],
                     w_out[l], ffn2_norm[l], ffn2_w_gate[l], ffn2_w_up[l], ffn2_w_down[l])
    return x2d.reshape(batch, seq, d)
```

```python
import struct

import jax
import jax.numpy as jnp
from jax import lax
from jax.experimental import pallas as pl
from jax.experimental.pallas import tpu as pltpu

F32 = jnp.float32
BF16 = jnp.bfloat16
HIGHEST = lax.Precision.HIGHEST

NORM_EPS = 1e-6
D_FF = 2816
GDN_HEADS = 4
GDN_HEAD_DIM = 128
GDN_WIDTH = GDN_HEADS * GDN_HEAD_DIM
GDN_CONV = 4
GDN_CHUNK = 64
GDN_GROUP = 2 * GDN_CHUNK
ATT_HEADS = 4
ATT_HEAD_DIM = 64
ATT_WIDTH = ATT_HEADS * ATT_HEAD_DIM
MOBA_BLOCK = 256
MOBA_TOPK = 3
MASKED = -1e30
LOG2_E = 1.4426950408889634
BIAS_TERMS = 3

V7X_VMEM_LIMIT_BYTES = 56 * 1024 * 1024
V7X_LANES = 128
V7X_BF16_SUBLANES = 16
ATT_VROWS = ATT_HEAD_DIM + V7X_BF16_SUBLANES
FFN_ROWS = 256
FFN1_SLABS = 2
FFN2_SLABS = 4
GDN_ROWS = 512


def _dot(a, b, precision=None):
    return jnp.dot(a, b, preferred_element_type=F32, precision=precision)


def _dot_nt(a, b, precision=None):
    return lax.dot_general(a, b, (((1,), (1,)), ((), ())), preferred_element_type=F32,
                           precision=precision)


def _rms_rows(x, gain):
    return x * lax.rsqrt(jnp.mean(x * x, axis=-1, keepdims=True) + NORM_EPS) * gain


def _silu(x):
    half = 0.5 * x
    return half + half * jnp.tanh(half)


def _swiglu(x, gain, wg_ref, wu_ref, wd_ref):
    h = _rms_rows(x, gain).astype(BF16)
    g = _dot(h, wg_ref[...])
    u = _dot(h, wu_ref[...])
    return _dot((_silu(g) * u).astype(BF16), wd_ref[...])


def _resident():
    return pl.BlockSpec(memory_space=pltpu.VMEM)


def _bf16_terms(x):
    terms, rest = [], x
    for _ in range(BIAS_TERMS):
        t = rest.astype(BF16).astype(F32)
        terms.append(t)
        rest = rest - t
    return terms


def _bf16_terms_const(value):
    def f32(v):
        return struct.unpack("<f", struct.pack("<f", v))[0]

    def bf16_round(v):
        bits = struct.unpack("<I", struct.pack("<f", v))[0]
        bits = (bits + 0x7FFF + ((bits >> 16) & 1)) & 0xFFFF0000
        return struct.unpack("<f", struct.pack("<I", bits))[0]

    terms, rest = [], f32(value)
    for _ in range(BIAS_TERMS):
        t = bf16_round(rest)
        terms.append(t)
        rest = f32(rest - t)
    return terms


def _ffn1_inproj_kernel(x_ref, n1_ref, wg_ref, wu_ref, wd_ref, n2_ref, win_ref,
                        x1_ref, pg_ref, pab_ref, patt_ref):
    gdn_w, ab_w = pg_ref.shape[1], pab_ref.shape[1]
    for r0 in range(0, x_ref.shape[0], FFN_ROWS):
        rows = slice(r0, r0 + FFN_ROWS)
        x = x_ref[rows, :]
        x1 = x + 0.5 * _swiglu(x, n1_ref[...], wg_ref, wu_ref, wd_ref)
        x1_ref[rows, :] = x1
        h = _rms_rows(x1, n2_ref[...]).astype(BF16)
        pg_ref[rows, :] = _dot(h, win_ref[:, 0:gdn_w])
        pab_ref[rows, :] = _dot(h, win_ref[:, gdn_w:gdn_w + ab_w])
        patt_ref[rows, :] = _dot(h, win_ref[:, gdn_w + ab_w:])


def _ffn1_inproj(x2d, n1, wg, wu, wd, n2, win, gdn_w, ab_w):
    n, d = x2d.shape
    att_w = win.shape[1] - gdn_w - ab_w
    rows = FFN1_SLABS * FFN_ROWS
    row_spec = lambda w: pl.BlockSpec((rows, w), lambda i: (i, 0))
    return pl.pallas_call(
        _ffn1_inproj_kernel,
        grid=(n // rows,),
        in_specs=[row_spec(d)] + [_resident()] * 6,
        out_specs=[row_spec(d), row_spec(gdn_w), row_spec(ab_w), row_spec(att_w)],
        out_shape=[jax.ShapeDtypeStruct((n, d), F32),
                   jax.ShapeDtypeStruct((n, gdn_w), F32),
                   jax.ShapeDtypeStruct((n, ab_w), F32),
                   jax.ShapeDtypeStruct((n, att_w), F32)],
        compiler_params=pltpu.CompilerParams(dimension_semantics=("arbitrary",),
                                             vmem_limit_bytes=V7X_VMEM_LIMIT_BYTES),
        name="ffn1_inproj",
    )(x2d, n1, wg, wu, wd, n2, win)


def _outproj_ffn2_kernel(x1_ref, yg_ref, ya_ref, wo_ref, n_ref, wg_ref, wu_ref, wd_ref, o_ref):
    gdn_w = yg_ref.shape[1]
    for r0 in range(0, x1_ref.shape[0], FFN_ROWS):
        rows = slice(r0, r0 + FFN_ROWS)
        x2 = (x1_ref[rows, :] + _dot(yg_ref[rows, :], wo_ref[0:gdn_w, :])
              + _dot(ya_ref[rows, :], wo_ref[gdn_w:, :]))
        o_ref[rows, :] = x2 + 0.5 * _swiglu(x2, n_ref[...], wg_ref, wu_ref, wd_ref)


def _outproj_ffn2(x1, yg, ya, wo, nrm, wg, wu, wd):
    n, d = x1.shape
    rows = FFN2_SLABS * FFN_ROWS
    row_spec = lambda w: pl.BlockSpec((rows, w), lambda i: (i, 0))
    return pl.pallas_call(
        _outproj_ffn2_kernel,
        grid=(n // rows,),
        in_specs=[row_spec(d), row_spec(yg.shape[1]), row_spec(ya.shape[1])] + [_resident()] * 5,
        out_specs=row_spec(d),
        out_shape=jax.ShapeDtypeStruct((n, d), F32),
        compiler_params=pltpu.CompilerParams(dimension_semantics=("arbitrary",),
                                             vmem_limit_bytes=V7X_VMEM_LIMIT_BYTES),
        name="outproj_ffn2",
    )(x1, yg, ya, wo, nrm, wg, wu, wd)


def _head_block_ones(width, head_dim):
    r = lax.broadcasted_iota(jnp.int32, (width, width), 0) // head_dim
    c = lax.broadcasted_iota(jnp.int32, (width, width), 1) // head_dim
    return jnp.where(r == c, 1.0, 0.0).astype(BF16)


def _head_rms(x, gain, block_ones, head_dim):
    sq = x * x
    hi = sq.astype(BF16)
    lo = (sq - hi.astype(F32)).astype(BF16)
    ss = _dot(hi, block_ones) + _dot(lo, block_ones)
    return x * lax.rsqrt(ss * (1.0 / head_dim) + NORM_EPS) * gain


def _values_with_ones(v_t):
    Dh = ATT_HEAD_DIM
    row = lax.broadcasted_iota(jnp.int32, (V7X_BF16_SUBLANES, v_t.shape[1]), 0)
    ones_block = jnp.where(row == 0, 1.0, 0.0)
    parts = []
    for h in range(ATT_HEADS):
        parts += [v_t[h * Dh:(h + 1) * Dh], ones_block]
    return jnp.concatenate(parts, axis=0).astype(BF16)


def _mem_kv_kernel(mem_ref, nrm_ref, wkv_ref, kg_ref, k_ref, vt_ref):
    h = _rms_rows(mem_ref[...], nrm_ref[...]).astype(BF16)
    kv = _dot(h, wkv_ref[...])
    k = _head_rms(kv[:, :ATT_WIDTH], kg_ref[...], _head_block_ones(ATT_WIDTH, ATT_HEAD_DIM), ATT_HEAD_DIM)
    k_ref[...] = k.astype(BF16)
    vt_ref[...] = _values_with_ones(kv[:, ATT_WIDTH:].T)


def _mem_kv(mem, nrm, wkv, kgain):
    b, m, d = mem.shape
    return pl.pallas_call(
        _mem_kv_kernel,
        grid=(b,),
        in_specs=[pl.BlockSpec((None, m, d), lambda i: (i, 0, 0)), _resident(), _resident(), _resident()],
        out_specs=[pl.BlockSpec((None, m, ATT_WIDTH), lambda i: (i, 0, 0)),
                   pl.BlockSpec((None, ATT_HEADS * ATT_VROWS, m), lambda i: (i, 0, 0))],
        out_shape=[jax.ShapeDtypeStruct((b, m, ATT_WIDTH), BF16),
                   jax.ShapeDtypeStruct((b, ATT_HEADS * ATT_VROWS, m), BF16)],
        compiler_params=pltpu.CompilerParams(dimension_semantics=("arbitrary",)),
        name="mem_kv",
    )(mem, nrm, wkv, kgain)


def _softplus(x):
    return jnp.maximum(x, 0.0) + jnp.log(1.0 + jnp.exp(-jnp.abs(x)))


def _unit_lower_inverses(a_list, eye):
    x16 = [(-a).astype(BF16) for a in a_list]
    t = [eye - a for a in a_list]
    for _ in range(5):
        x16 = [_dot(xb, xb).astype(BF16) for xb in x16]
        t = [ti + _dot(ti.astype(BF16), xb) for ti, xb in zip(t, x16)]
    t16 = [ti.astype(BF16) for ti in t]
    a_hi = [a.astype(BF16) for a in a_list]
    a_lo = [(a - hi.astype(F32)).astype(BF16) for a, hi in zip(a_list, a_hi)]
    resid = [(eye - tb.astype(F32) - _dot(hi, tb) - _dot(lo, tb)).astype(BF16)
             for tb, hi, lo in zip(t16, a_hi, a_lo)]
    return [tb.astype(F32) + _dot(tb, r) for tb, r in zip(t16, resid)]


def _gdn_kernel(qkv_ref, z_ref, ab_ref, cw_ref, alog_ref, dtb_ref, og_ref, y_ref, xbuf_ref, s_ref):
    rows = z_ref.shape[0]
    H, Dh, C, W = GDN_HEADS, GDN_HEAD_DIM, GDN_CHUNK, GDN_WIDTH
    G = GDN_GROUP
    n_chunks = rows // C
    n_groups = rows // G
    t = pl.program_id(1)

    @pl.when(t == 0)
    def _():
        xbuf_ref[0:8, :] = jnp.zeros((8, 3 * W), F32)
        s_ref[...] = jnp.zeros_like(s_ref)

    xbuf_ref[8:, :] = qkv_ref[...]
    conv = cw_ref[0:1, :] * xbuf_ref[5:5 + rows, :]
    for j in range(1, GDN_CONV):
        conv = conv + cw_ref[j:j + 1, :] * xbuf_ref[5 + j:5 + j + rows, :]
    xbuf_ref[0:8, :] = xbuf_ref[rows:rows + 8, :]
    act = _silu(conv)

    ab = ab_ref[...]
    log_decay = -jnp.exp(alog_ref[...]) * _softplus(ab + dtb_ref[...])
    beta_all = jax.nn.sigmoid(ab)
    rt = lax.broadcasted_iota(jnp.int32, (rows, rows), 0)
    ct = lax.broadcasted_iota(jnp.int32, (rows, rows), 1)
    chunk_lower = (((rt // C) == (ct // C)) & (rt >= ct)).astype(F32)
    gc_all = _dot(chunk_lower, log_decay, HIGHEST)
    gc_rows = gc_all.T

    ri = lax.broadcasted_iota(jnp.int32, (G, G), 0)
    ci = lax.broadcasted_iota(jnp.int32, (G, G), 1)
    same_chunk = (ri // C) == (ci // C)
    lower = same_chunk & (ri >= ci)
    strict = same_chunk & (ri > ci)
    eye = (ri == ci).astype(F32)

    units = [(h, g) for h in range(H) for g in range(n_groups)]
    q_h, k_h, v_h, gcol_h, beta_h = [], [], [], [], []
    for h in range(H):
        qh = act[:, h * Dh:(h + 1) * Dh]
        kh = act[:, W + h * Dh:W + (h + 1) * Dh]
        v_h.append(act[:, 2 * W + h * Dh:2 * W + (h + 1) * Dh])
        q_h.append(qh * (lax.rsqrt(jnp.sum(qh * qh, axis=-1, keepdims=True) + NORM_EPS) * Dh ** -0.5))
        k_h.append(kh * lax.rsqrt(jnp.sum(kh * kh, axis=-1, keepdims=True) + NORM_EPS))
        gcol_h.append(jnp.broadcast_to(gc_all[:, h:h + 1], (rows, Dh)))
        beta_h.append(jnp.broadcast_to(beta_all[:, H + h:H + h + 1], (rows, Dh)))

    def unit_rows(x, g):
        return x[g * G:(g + 1) * G]

    k_u = [unit_rows(k_h[h], g) for h, g in units]
    gcol_u = [unit_rows(gcol_h[h], g) for h, g in units]
    kb_u = [k * unit_rows(beta_h[h], g) for k, (h, g) in zip(k_u, units)]
    k16_u = [k.astype(BF16) for k in k_u]
    q_u = [unit_rows(q_h[h], g) for h, g in units]
    kk_u = [_dot_nt(kb.astype(BF16), k16) for kb, k16 in zip(kb_u, k16_u)]
    qk_u = [_dot_nt(q.astype(BF16), k16) for q, k16 in zip(q_u, k16_u)]
    decay_u = []
    for gcol, (h, g) in zip(gcol_u, units):
        diff = gcol - gc_rows[h:h + 1, g * G:(g + 1) * G]
        decay_u.append(jnp.where(lower, jnp.exp(jnp.where(lower, diff, 0.0)), 0.0))
    a_u = [jnp.where(strict, kk * decay, 0.0) for kk, decay in zip(kk_u, decay_u)]
    qk16_u = [(qk * decay).astype(BF16) for qk, decay in zip(qk_u, decay_u)]
    tinv_u = _unit_lower_inverses(a_u, eye)
    egc_u = [jnp.exp(gcol) for gcol in gcol_u]
    rhs_u = [jnp.concatenate([unit_rows(v_h[h], g) * unit_rows(beta_h[h], g), kb * egc], axis=1).astype(BF16)
             for kb, egc, (h, g) in zip(kb_u, egc_u, units)]
    uw_u = [_dot(tinv.astype(BF16), rhs) for tinv, rhs in zip(tinv_u, rhs_u)]
    w16_u = [uw[:, Dh:].astype(BF16) for uw in uw_u]
    qe16_u = [(q * egc).astype(BF16) for q, egc in zip(q_u, egc_u)]

    def chunk_operands(h, c):
        u = h * n_groups + (c * C) // G
        r = slice((c * C) % G, (c * C) % G + C)
        g_last = gcol_u[u][r.stop - 1:r.stop, :]
        return (uw_u[u][r, :Dh],
                jnp.concatenate([w16_u[u][r], qe16_u[u][r]], axis=0),
                qk16_u[u][r, r],
                (k_u[u][r] * jnp.exp(g_last - gcol_u[u][r])).T.astype(BF16),
                jnp.exp(g_last))

    ops = [[chunk_operands(h, c) for c in range(n_chunks)] for h in range(H)]

    state = [s_ref[h] for h in range(H)]
    o_l = [[] for _ in range(H)]
    for c in range(n_chunks):
        ws = [_dot(ops[h][c][1], state[h].astype(BF16)) for h in range(H)]
        v_new = [(ops[h][c][0] - ws[h][:C]).astype(BF16) for h in range(H)]
        state = [state[h] * ops[h][c][4] + _dot(ops[h][c][3], v_new[h]) for h in range(H)]
        for h in range(H):
            o_l[h].append(ws[h][C:] + _dot(ops[h][c][2], v_new[h]))
    for h in range(H):
        s_ref[h] = state[h]

    z = z_ref[...]
    for h in range(H):
        cols = slice(h * Dh, (h + 1) * Dh)
        o = jnp.concatenate(o_l[h], axis=0)
        y_ref[:, cols] = (_rms_rows(o, og_ref[...]) * _silu(z[:, cols])).astype(BF16)


def _gdn(pg, pab, conv_w, alog_vec, dtb_vec, out_gain, batch, seq):
    n = pg.shape[0]
    rows = GDN_ROWS
    tiles = seq // rows
    W = GDN_WIDTH
    return pl.pallas_call(
        _gdn_kernel,
        grid=(batch, tiles),
        in_specs=[pl.BlockSpec((rows, 3 * W), lambda b, t: (b * tiles + t, 0)),
                  pl.BlockSpec((rows, W), lambda b, t: (b * tiles + t, 3)),
                  pl.BlockSpec((rows, pab.shape[1]), lambda b, t: (b * tiles + t, 0)),
                  _resident(), _resident(), _resident(), _resident()],
        out_specs=pl.BlockSpec((rows, W), lambda b, t: (b * tiles + t, 0)),
        out_shape=jax.ShapeDtypeStruct((n, W), BF16),
        scratch_shapes=[pltpu.VMEM((rows + 8, 3 * W), F32),
                        pltpu.VMEM((GDN_HEADS, GDN_HEAD_DIM, GDN_HEAD_DIM), F32)],
        compiler_params=pltpu.CompilerParams(dimension_semantics=("arbitrary", "arbitrary"),
                                             vmem_limit_bytes=V7X_VMEM_LIMIT_BYTES),
        name="gdn",
    )(pg, pg, pab, conv_w, alog_vec, dtb_vec, out_gain)


def _alibi_slope(h):
    return 2.0 ** (-8.0 * (h + 1) / ATT_HEADS)


def _moba_kernel(p_ref, qg_ref, kg_ref, cg_ref, km_ref, vmt_ref, y_ref,
                 k_ref, vt_ref, kmean_ref, sel_ref, m_ref, acc_ref, s_ref, kaug_ref, rhs_ref):
    H, Dh, BLK, Wd = ATT_HEADS, ATT_HEAD_DIM, MOBA_BLOCK, ATT_WIDTH
    PAIR = 2 * Dh
    AUG = V7X_BF16_SUBLANES
    nb = kmean_ref.shape[0]
    i = pl.program_id(1)
    pair_row_head = lax.broadcasted_iota(jnp.int32, (PAIR, BLK), 0) // Dh

    @pl.when((pl.program_id(0) == 0) & (i == 0))
    def _():
        lane = lax.broadcasted_iota(jnp.int32, (BLK, V7X_LANES), 1)
        key_pos = lax.broadcasted_iota(jnp.int32, (BLK, V7X_LANES), 0).astype(F32)
        kaug_ref[...] = jnp.where(lane < BIAS_TERMS, key_pos,
                                  jnp.where(lane < 2 * BIAS_TERMS, 1.0, 0.0)).astype(BF16)
        rhs_ref[...] = jnp.zeros_like(rhs_ref)

    @pl.when(i == 0)
    def _():
        kmean_ref[...] = jnp.zeros_like(kmean_ref)

    def pair_lanes(h):
        return slice((h // 2) * PAIR, (h // 2 + 1) * PAIR)

    def value_rows(h):
        return slice(h * ATT_VROWS, (h + 1) * ATT_VROWS)

    def normalised(acc):
        return acc[0:Dh] / acc[Dh:Dh + 1]

    def head_queries(q_t, h):
        qp = q_t[pair_lanes(h), :]
        return jnp.where(pair_row_head == h % 2, qp, jnp.zeros_like(qp))

    block_ones = _head_block_ones(Wd, Dh)
    qn = _head_rms(p_ref[:, 0:Wd], qg_ref[...], block_ones, Dh)
    kn = _head_rms(p_ref[:, Wd:2 * Wd], kg_ref[...], block_ones, Dh)
    cqn = _head_rms(p_ref[:, 3 * Wd:4 * Wd], cg_ref[...], block_ones, Dh)
    scale = Dh ** -0.5 * LOG2_E
    q_t = (qn * scale).T.astype(BF16)
    cq_t = (cqn * scale).T.astype(BF16)
    k_ref[pl.ds(pl.multiple_of(i * BLK, BLK), BLK), :] = kn.astype(BF16)
    vt_ref[i] = _values_with_ones(p_ref[:, 2 * Wd:3 * Wd].T)
    for h in range(H):
        rhs_ref[h, 0:PAIR, :] = head_queries(q_t, h)

    kmean = kmean_ref[...]
    blk_row = lax.broadcasted_iota(jnp.int32, (nb, Wd), 0)
    lane_head = lax.broadcasted_iota(jnp.int32, (nb, Wd), 1) // Dh
    kmean_heads = jnp.concatenate([jnp.where(lane_head == h, kmean, 0.0) for h in range(H)], axis=0)
    gate_all = _dot_nt(kmean_heads, qn, HIGHEST)
    jdx_f = lax.broadcasted_iota(jnp.int32, (nb, BLK), 0).astype(F32)
    past = lax.broadcasted_iota(jnp.int32, (nb, BLK), 0) < i
    for h in range(H):
        gate = jnp.where(past, gate_all[h * nb:(h + 1) * nb], -jnp.inf)
        chosen = jnp.zeros(gate.shape, jnp.bool_)
        for _ in range(MOBA_TOPK):
            top = jnp.max(gate, axis=0, keepdims=True)
            first = jnp.min(jnp.where(gate == top, jdx_f, float(nb)), axis=0, keepdims=True)
            pick = jdx_f == first
            chosen = chosen | pick
            gate = jnp.where(pick, -jnp.inf, gate)
        sel_ref[h] = jnp.where(chosen & past, 0.0, MASKED)
    kmean_ref[...] = jnp.where(blk_row == i, jnp.mean(kn, axis=0, keepdims=True), kmean)

    aug_row = lax.broadcasted_iota(jnp.int32, (AUG, BLK), 0)
    query_pos = lax.broadcasted_iota(jnp.int32, (1, BLK), 1).astype(F32)

    def scores(blk, own):
        r0 = pl.multiple_of(blk * BLK, BLK)
        distance = (i - blk).astype(F32) * BLK
        keys_aug = kaug_ref[...]
        out = []
        for h in range(H):
            c = _alibi_slope(h) * LOG2_E
            row_bias = -c * (query_pos + distance)
            if not own:
                row_bias = row_bias + sel_ref[h, pl.ds(blk, 1), :]
            rows = _bf16_terms_const(c) + _bf16_terms(row_bias)
            aug = jnp.zeros((AUG, BLK), F32)
            for n, val in enumerate(rows):
                aug = jnp.where(aug_row == n, val, aug)
            rhs_ref[h, PAIR:PAIR + AUG, :] = aug.astype(BF16)
            keys = jnp.concatenate([k_ref[pl.ds(r0, BLK), pair_lanes(h)], keys_aug], axis=1)
            out.append(_dot(keys, rhs_ref[h]))
        return out

    own_scores = scores(i, True)
    mem_scores = [_dot(km_ref[:, pair_lanes(h)], head_queries(cq_t, h)) for h in range(H)]
    first_past = scores(0, False)
    causal = (lax.broadcasted_iota(jnp.int32, (BLK, BLK), 0) <= lax.broadcasted_iota(jnp.int32, (BLK, BLK), 1))
    own_p, mem_p = [], []
    for h in range(H):
        s_t = jnp.where(causal, own_scores[h], MASKED)
        m = jnp.max(s_t, axis=0, keepdims=True)
        p = jnp.exp2(s_t - m)
        m_ref[h] = m
        own_p.append(p.astype(BF16))
    for h in range(H):
        p = jnp.exp2(mem_scores[h] - jnp.max(mem_scores[h], axis=0, keepdims=True))
        mem_p.append(p.astype(BF16))
    for h in range(H):
        acc_ref[h] = _dot(vt_ref[i, value_rows(h), :], own_p[h])
        s_ref[0, h] = first_past[h]
    for h in range(H):
        acc_ref[H + h] = _dot(vmt_ref[value_rows(h), :], mem_p[h])

    def block_step(j, slot, nxt):
        next_scores = None if nxt is None else scores(nxt, False)
        for h in range(H):
            s_t = s_ref[slot, h]
            m_old = m_ref[h]
            m_new = jnp.maximum(m_old, jnp.max(s_t, axis=0, keepdims=True))
            p = jnp.exp2(s_t - m_new)
            alpha = jnp.exp2(m_old - m_new)
            acc_ref[h] = alpha * acc_ref[h] + _dot(vt_ref[j, value_rows(h), :], p.astype(BF16))
            m_ref[h] = m_new
        if next_scores is not None:
            for h in range(H):
                s_ref[1 - slot, h] = next_scores[h]

    last_past = jnp.maximum(i - 1, 0)

    def block_run(j0, count):
        for n in range(count):
            block_step(j0 + n, n % 2, jnp.minimum(j0 + n + 1, last_past))

    def block_octet(jj, carry):
        block_run(8 * jj, 8)
        return carry

    lax.fori_loop(0, i // 8, block_octet, 0)

    @pl.when(i % 8 >= 4)
    def _():
        block_run(8 * (i // 8), 4)

    @pl.when(i % 4 >= 2)
    def _():
        block_run(4 * (i // 4), 2)

    @pl.when(i % 2 == 1)
    def _():
        block_step(i - 1, 0, None)

    out_t = jnp.concatenate([normalised(acc_ref[a]) for a in range(2 * H)], axis=0)
    y_ref[...] = out_t.T.astype(BF16)


def _moba(patt, qg, kg, cg, kmem, vmem_t, batch, seq):
    nb = seq // MOBA_BLOCK
    Wd, H, BLK = ATT_WIDTH, ATT_HEADS, MOBA_BLOCK
    n_mem = kmem.shape[1]
    return pl.pallas_call(
        _moba_kernel,
        grid=(batch, nb),
        in_specs=[pl.BlockSpec((BLK, 4 * Wd), lambda b, i: (b * nb + i, 0)),
                  _resident(), _resident(), _resident(),
                  pl.BlockSpec((None, n_mem, Wd), lambda b, i: (b, 0, 0)),
                  pl.BlockSpec((None, H * ATT_VROWS, n_mem), lambda b, i: (b, 0, 0))],
        out_specs=pl.BlockSpec((BLK, 2 * Wd), lambda b, i: (b * nb + i, 0)),
        out_shape=jax.ShapeDtypeStruct((batch * seq, 2 * Wd), BF16),
        scratch_shapes=[pltpu.VMEM((seq, Wd), BF16),
                        pltpu.VMEM((nb, H * ATT_VROWS, BLK), BF16),
                        pltpu.VMEM((nb, Wd), F32),
                        pltpu.VMEM((H, nb, BLK), F32),
                        pltpu.VMEM((H, 1, BLK), F32),
                        pltpu.VMEM((2 * H, ATT_VROWS, BLK), F32),
                        pltpu.VMEM((2, H, BLK, BLK), F32),
                        pltpu.VMEM((BLK, V7X_LANES), BF16),
                        pltpu.VMEM((H, 2 * ATT_HEAD_DIM + V7X_LANES, BLK), BF16)],
        compiler_params=pltpu.CompilerParams(dimension_semantics=("arbitrary", "arbitrary"),
                                             vmem_limit_bytes=V7X_VMEM_LIMIT_BYTES),
        name="moba",
    )(patt, qg, kg, cg, kmem, vmem_t)


def _lane_vector(values, width):
    return jnp.zeros((1, width), F32).at[0, :values.shape[0]].set(values.astype(F32))


def _layer(x2d, mem, batch, seq, ffn1_norm, ffn1_w_gate, ffn1_w_up, ffn1_w_down, mix_norm, w_in,
           gdn_conv_w, gdn_a_log, gdn_dt_bias, gdn_out_norm, moba_q_norm, moba_k_norm,
           mem_norm, w_mem_kv, mem_q_norm, mem_k_norm, w_out, ffn2_norm, ffn2_w_gate, ffn2_w_up, ffn2_w_down):
    W, H = GDN_WIDTH, GDN_HEADS
    row = lambda v: v.reshape(1, -1).astype(F32)
    tile_heads = lambda v: jnp.tile(v.astype(F32), ATT_HEADS).reshape(1, -1)
    ab0 = 4 * W
    att0 = ab0 + 2 * H
    ab_width = V7X_LANES
    w_in_sections = jnp.concatenate(
        [w_in[:, :ab0], w_in[:, ab0:att0], jnp.zeros((w_in.shape[0], ab_width - 2 * H), w_in.dtype),
         w_in[:, att0:]], axis=1).astype(BF16)

    x1, pg, pab, patt = _ffn1_inproj(
        x2d, row(ffn1_norm), ffn1_w_gate.astype(BF16), ffn1_w_up.astype(BF16), ffn1_w_down.astype(BF16),
        row(mix_norm), w_in_sections, ab0, ab_width)

    y_gdn = _gdn(pg, pab, gdn_conv_w.astype(F32), _lane_vector(gdn_a_log, ab_width),
                 _lane_vector(gdn_dt_bias, ab_width), row(gdn_out_norm), batch, seq)

    kmem, vmem_t = _mem_kv(mem, row(mem_norm), w_mem_kv.astype(BF16), tile_heads(mem_k_norm))
    y_att = _moba(patt, tile_heads(moba_q_norm), tile_heads(moba_k_norm), tile_heads(mem_q_norm),
                  kmem, vmem_t, batch, seq)

    return _outproj_ffn2(x1, y_gdn, y_att, w_out.astype(BF16), row(ffn2_norm),
                         ffn2_w_gate.astype(BF16), ffn2_w_up.astype(BF16), ffn2_w_down.astype(BF16))


def kernel(x, mem, ffn1_norm, ffn1_w_gate, ffn1_w_up, ffn1_w_down, mix_norm, w_in, gdn_conv_w, gdn_a_log,
           gdn_dt_bias, gdn_out_norm, moba_q_norm, moba_k_norm, mem_norm, w_mem_kv, mem_q_norm, mem_k_norm,
           w_out, ffn2_norm, ffn2_w_gate, ffn2_w_up, ffn2_w_down):
    batch, seq, d = x.shape
    assert seq % MOBA_BLOCK == 0 and seq % GDN_ROWS == 0 and (batch * seq) % FFN_ROWS == 0
    depth = w_in.shape[0]
    x2d = x.reshape(batch * seq, d)
    for l in range(depth):
        x2d = _layer(x2d, mem, batch, seq, ffn1_norm[l], ffn1_w_gate[l], ffn1_w_up[l], ffn1_w_down[l],
                     mix_norm[l], w_in[l], gdn_conv_w[l], gdn_a_log[l], gdn_dt_bias[l], gdn_out_norm[l],
                     moba_q_norm[l], moba_k_norm[l], mem_norm[l], w_mem_kv[l], mem_q_norm[l], mem_k_norm[l],
                     w_out[l], ffn2_norm[l], ffn2_w_gate[l], ffn2_w_up[l], ffn2_w_down[l])
    return x2d.reshape(batch, seq, d)
```

```python
import struct

import jax
import jax.numpy as jnp
from jax import lax
from jax.experimental import pallas as pl
from jax.experimental.pallas import tpu as pltpu

F32 = jnp.float32
BF16 = jnp.bfloat16
HIGHEST = lax.Precision.HIGHEST

NORM_EPS = 1e-6
D_FF = 2816
GDN_HEADS = 4
GDN_HEAD_DIM = 128
GDN_WIDTH = GDN_HEADS * GDN_HEAD_DIM
GDN_CONV = 4
GDN_CHUNK = 64
GDN_GROUP = 2 * GDN_CHUNK
ATT_HEADS = 4
ATT_HEAD_DIM = 64
ATT_WIDTH = ATT_HEADS * ATT_HEAD_DIM
MOBA_BLOCK = 256
MOBA_TOPK = 3
MASKED = -1e30
LOG2_E = 1.4426950408889634
BIAS_TERMS = 3

V7X_VMEM_LIMIT_BYTES = 56 * 1024 * 1024
V7X_LANES = 128
V7X_BF16_SUBLANES = 16
ATT_VROWS = ATT_HEAD_DIM + V7X_BF16_SUBLANES
FFN_ROWS = 256
FFN1_SLABS = 2
FFN2_SLABS = 4
GDN_ROWS = 512


def _dot(a, b, precision=None):
    return jnp.dot(a, b, preferred_element_type=F32, precision=precision)


def _dot_nt(a, b, precision=None):
    return lax.dot_general(a, b, (((1,), (1,)), ((), ())), preferred_element_type=F32,
                           precision=precision)


def _rms_rows(x, gain):
    return x * lax.rsqrt(jnp.mean(x * x, axis=-1, keepdims=True) + NORM_EPS) * gain


def _silu(x):
    half = 0.5 * x
    return half + half * jnp.tanh(half)


def _swiglu(x, gain, wg_ref, wu_ref, wd_ref):
    h = _rms_rows(x, gain).astype(BF16)
    g = _dot(h, wg_ref[...])
    u = _dot(h, wu_ref[...])
    return _dot((_silu(g) * u).astype(BF16), wd_ref[...])


def _resident():
    return pl.BlockSpec(memory_space=pltpu.VMEM)


def _bf16_terms(x):
    terms, rest = [], x
    for _ in range(BIAS_TERMS):
        t = rest.astype(BF16).astype(F32)
        terms.append(t)
        rest = rest - t
    return terms


def _bf16_terms_const(value):
    def f32(v):
        return struct.unpack("<f", struct.pack("<f", v))[0]

    def bf16_round(v):
        bits = struct.unpack("<I", struct.pack("<f", v))[0]
        bits = (bits + 0x7FFF + ((bits >> 16) & 1)) & 0xFFFF0000
        return struct.unpack("<f", struct.pack("<I", bits))[0]

    terms, rest = [], f32(value)
    for _ in range(BIAS_TERMS):
        t = bf16_round(rest)
        terms.append(t)
        rest = f32(rest - t)
    return terms


def _ffn1_inproj_kernel(x_ref, n1_ref, wg_ref, wu_ref, wd_ref, n2_ref, win_ref,
                        x1_ref, pg_ref, pab_ref, patt_ref):
    gdn_w, ab_w = pg_ref.shape[1], pab_ref.shape[1]
    for r0 in range(0, x_ref.shape[0], FFN_ROWS):
        rows = slice(r0, r0 + FFN_ROWS)
        x = x_ref[rows, :]
        x1 = x + 0.5 * _swiglu(x, n1_ref[...], wg_ref, wu_ref, wd_ref)
        x1_ref[rows, :] = x1
        h = _rms_rows(x1, n2_ref[...]).astype(BF16)
        pg_ref[rows, :] = _dot(h, win_ref[:, 0:gdn_w])
        pab_ref[rows, :] = _dot(h, win_ref[:, gdn_w:gdn_w + ab_w])
        patt_ref[rows, :] = _dot(h, win_ref[:, gdn_w + ab_w:])


def _ffn1_inproj(x2d, n1, wg, wu, wd, n2, win, gdn_w, ab_w):
    n, d = x2d.shape
    att_w = win.shape[1] - gdn_w - ab_w
    rows = FFN1_SLABS * FFN_ROWS
    row_spec = lambda w: pl.BlockSpec((rows, w), lambda i: (i, 0))
    return pl.pallas_call(
        _ffn1_inproj_kernel,
        grid=(n // rows,),
        in_specs=[row_spec(d)] + [_resident()] * 6,
        out_specs=[row_spec(d), row_spec(gdn_w), row_spec(ab_w), row_spec(att_w)],
        out_shape=[jax.ShapeDtypeStruct((n, d), F32),
                   jax.ShapeDtypeStruct((n, gdn_w), F32),
                   jax.ShapeDtypeStruct((n, ab_w), F32),
                   jax.ShapeDtypeStruct((n, att_w), F32)],
        compiler_params=pltpu.CompilerParams(dimension_semantics=("arbitrary",),
                                             vmem_limit_bytes=V7X_VMEM_LIMIT_BYTES),
        name="ffn1_inproj",
    )(x2d, n1, wg, wu, wd, n2, win)


def _outproj_ffn2_kernel(x1_ref, yg_ref, ya_ref, wo_ref, n_ref, wg_ref, wu_ref, wd_ref, o_ref):
    gdn_w = yg_ref.shape[1]
    for r0 in range(0, x1_ref.shape[0], FFN_ROWS):
        rows = slice(r0, r0 + FFN_ROWS)
        x2 = (x1_ref[rows, :] + _dot(yg_ref[rows, :], wo_ref[0:gdn_w, :])
              + _dot(ya_ref[rows, :], wo_ref[gdn_w:, :]))
        o_ref[rows, :] = x2 + 0.5 * _swiglu(x2, n_ref[...], wg_ref, wu_ref, wd_ref)


def _outproj_ffn2(x1, yg, ya, wo, nrm, wg, wu, wd):
    n, d = x1.shape
    rows = FFN2_SLABS * FFN_ROWS
    row_spec = lambda w: pl.BlockSpec((rows, w), lambda i: (i, 0))
    return pl.pallas_call(
        _outproj_ffn2_kernel,
        grid=(n // rows,),
        in_specs=[row_spec(d), row_spec(yg.shape[1]), row_spec(ya.shape[1])] + [_resident()] * 5,
        out_specs=row_spec(d),
        out_shape=jax.ShapeDtypeStruct((n, d), F32),
        compiler_params=pltpu.CompilerParams(dimension_semantics=("arbitrary",),
                                             vmem_limit_bytes=V7X_VMEM_LIMIT_BYTES),
        name="outproj_ffn2",
    )(x1, yg, ya, wo, nrm, wg, wu, wd)


def _head_block_ones(width, head_dim):
    r = lax.broadcasted_iota(jnp.int32, (width, width), 0) // head_dim
    c = lax.broadcasted_iota(jnp.int32, (width, width), 1) // head_dim
    return jnp.where(r == c, 1.0, 0.0).astype(BF16)


def _head_rms(x, gain, block_ones, head_dim):
    sq = x * x
    hi = sq.astype(BF16)
    lo = (sq - hi.astype(F32)).astype(BF16)
    ss = _dot(hi, block_ones) + _dot(lo, block_ones)
    return x * lax.rsqrt(ss * (1.0 / head_dim) + NORM_EPS) * gain


def _values_with_ones(v_t):
    Dh = ATT_HEAD_DIM
    row = lax.broadcasted_iota(jnp.int32, (V7X_BF16_SUBLANES, v_t.shape[1]), 0)
    ones_block = jnp.where(row == 0, 1.0, 0.0)
    parts = []
    for h in range(ATT_HEADS):
        parts += [v_t[h * Dh:(h + 1) * Dh], ones_block]
    return jnp.concatenate(parts, axis=0).astype(BF16)


def _mem_kv_kernel(mem_ref, nrm_ref, wkv_ref, kg_ref, k_ref, vt_ref):
    h = _rms_rows(mem_ref[...], nrm_ref[...]).astype(BF16)
    kv = _dot(h, wkv_ref[...])
    k = _head_rms(kv[:, :ATT_WIDTH], kg_ref[...], _head_block_ones(ATT_WIDTH, ATT_HEAD_DIM), ATT_HEAD_DIM)
    k_ref[...] = k.astype(BF16)
    vt_ref[...] = _values_with_ones(kv[:, ATT_WIDTH:].T)


def _mem_kv(mem, nrm, wkv, kgain):
    b, m, d = mem.shape
    return pl.pallas_call(
        _mem_kv_kernel,
        grid=(b,),
        in_specs=[pl.BlockSpec((None, m, d), lambda i: (i, 0, 0)), _resident(), _resident(), _resident()],
        out_specs=[pl.BlockSpec((None, m, ATT_WIDTH), lambda i: (i, 0, 0)),
                   pl.BlockSpec((None, ATT_HEADS * ATT_VROWS, m), lambda i: (i, 0, 0))],
        out_shape=[jax.ShapeDtypeStruct((b, m, ATT_WIDTH), BF16),
                   jax.ShapeDtypeStruct((b, ATT_HEADS * ATT_VROWS, m), BF16)],
        compiler_params=pltpu.CompilerParams(dimension_semantics=("arbitrary",)),
        name="mem_kv",
    )(mem, nrm, wkv, kgain)


def _softplus(x):
    return jnp.maximum(x, 0.0) + jnp.log(1.0 + jnp.exp(-jnp.abs(x)))


def _unit_lower_inverses(a_list, eye):
    x16 = [(-a).astype(BF16) for a in a_list]
    t = [eye - a for a in a_list]
    for _ in range(5):
        x16 = [_dot(xb, xb).astype(BF16) for xb in x16]
        t = [ti + _dot(ti.astype(BF16), xb) for ti, xb in zip(t, x16)]
    t16 = [ti.astype(BF16) for ti in t]
    a_hi = [a.astype(BF16) for a in a_list]
    a_lo = [(a - hi.astype(F32)).astype(BF16) for a, hi in zip(a_list, a_hi)]
    resid = [(eye - tb.astype(F32) - _dot(hi, tb) - _dot(lo, tb)).astype(BF16)
             for tb, hi, lo in zip(t16, a_hi, a_lo)]
    return [tb.astype(F32) + _dot(tb, r) for tb, r in zip(t16, resid)]


def _gdn_kernel(qkv_ref, z_ref, ab_ref, cw_ref, alog_ref, dtb_ref, og_ref, y_ref, xbuf_ref, s_ref):
    rows = z_ref.shape[0]
    H, Dh, C, W = GDN_HEADS, GDN_HEAD_DIM, GDN_CHUNK, GDN_WIDTH
    G = GDN_GROUP
    n_chunks = rows // C
    n_groups = rows // G
    t = pl.program_id(1)

    @pl.when(t == 0)
    def _():
        xbuf_ref[0:8, :] = jnp.zeros((8, 3 * W), F32)
        s_ref[...] = jnp.zeros_like(s_ref)

    xbuf_ref[8:, :] = qkv_ref[...]
    conv = cw_ref[0:1, :] * xbuf_ref[5:5 + rows, :]
    for j in range(1, GDN_CONV):
        conv = conv + cw_ref[j:j + 1, :] * xbuf_ref[5 + j:5 + j + rows, :]
    xbuf_ref[0:8, :] = xbuf_ref[rows:rows + 8, :]
    act = _silu(conv)

    ab = ab_ref[...]
    log_decay = -jnp.exp(alog_ref[...]) * _softplus(ab + dtb_ref[...])
    beta_all = jax.nn.sigmoid(ab)

    ri = lax.broadcasted_iota(jnp.int32, (G, G), 0)
    ci = lax.broadcasted_iota(jnp.int32, (G, G), 1)
    same_chunk = (ri // C) == (ci // C)
    lower = same_chunk & (ri >= ci)
    strict = same_chunk & (ri > ci)
    eye = (ri == ci).astype(F32)

    lower_f = lower.astype(F32)
    gc_all = jnp.concatenate([_dot(lower_f, log_decay[g * G:(g + 1) * G], HIGHEST) for g in range(n_groups)],
                             axis=0)
    gc_rows = gc_all.T

    units = [(h, g) for h in range(H) for g in range(n_groups)]
    q_h, k_h, v_h, gcol_h, beta_h = [], [], [], [], []
    for h in range(H):
        qh = act[:, h * Dh:(h + 1) * Dh]
        kh = act[:, W + h * Dh:W + (h + 1) * Dh]
        v_h.append(act[:, 2 * W + h * Dh:2 * W + (h + 1) * Dh])
        q_h.append(qh * (lax.rsqrt(jnp.sum(qh * qh, axis=-1, keepdims=True) + NORM_EPS) * Dh ** -0.5))
        k_h.append(kh * lax.rsqrt(jnp.sum(kh * kh, axis=-1, keepdims=True) + NORM_EPS))
        gcol_h.append(jnp.broadcast_to(gc_all[:, h:h + 1], (rows, Dh)))
        beta_h.append(jnp.broadcast_to(beta_all[:, H + h:H + h + 1], (rows, Dh)))

    def unit_rows(x, g):
        return x[g * G:(g + 1) * G]

    k_u = [unit_rows(k_h[h], g) for h, g in units]
    gcol_u = [unit_rows(gcol_h[h], g) for h, g in units]
    kb_u = [k * unit_rows(beta_h[h], g) for k, (h, g) in zip(k_u, units)]
    k16_u = [k.astype(BF16) for k in k_u]
    q_u = [unit_rows(q_h[h], g) for h, g in units]
    kk_u = [_dot_nt(kb.astype(BF16), k16) for kb, k16 in zip(kb_u, k16_u)]
    qk_u = [_dot_nt(q.astype(BF16), k16) for q, k16 in zip(q_u, k16_u)]
    decay_u = []
    for gcol, (h, g) in zip(gcol_u, units):
        diff = gcol - gc_rows[h:h + 1, g * G:(g + 1) * G]
        decay_u.append(jnp.where(lower, jnp.exp(jnp.where(lower, diff, 0.0)), 0.0))
    a_u = [jnp.where(strict, kk * decay, 0.0) for kk, decay in zip(kk_u, decay_u)]
    qk16_u = [(qk * decay).astype(BF16) for qk, decay in zip(qk_u, decay_u)]
    tinv_u = _unit_lower_inverses(a_u, eye)
    egc_u = [jnp.exp(gcol) for gcol in gcol_u]
    rhs_u = [jnp.concatenate([unit_rows(v_h[h], g) * unit_rows(beta_h[h], g), kb * egc], axis=1).astype(BF16)
             for kb, egc, (h, g) in zip(kb_u, egc_u, units)]
    uw_u = [_dot(tinv.astype(BF16), rhs) for tinv, rhs in zip(tinv_u, rhs_u)]
    w16_u = [uw[:, Dh:].astype(BF16) for uw in uw_u]
    qe16_u = [(q * egc).astype(BF16) for q, egc in zip(q_u, egc_u)]

    def chunk_operands(h, c):
        u = h * n_groups + (c * C) // G
        r = slice((c * C) % G, (c * C) % G + C)
        g_last = gcol_u[u][r.stop - 1:r.stop, :]
        return (uw_u[u][r, :Dh],
                jnp.concatenate([w16_u[u][r], qe16_u[u][r]], axis=0),
                qk16_u[u][r, r],
                (k_u[u][r] * jnp.exp(g_last - gcol_u[u][r])).T.astype(BF16),
                jnp.exp(g_last))

    ops = [[chunk_operands(h, c) for c in range(n_chunks)] for h in range(H)]

    state = [s_ref[h] for h in range(H)]
    o_l = [[] for _ in range(H)]
    for c in range(n_chunks):
        ws = [_dot(ops[h][c][1], state[h].astype(BF16)) for h in range(H)]
        v_new = [(ops[h][c][0] - ws[h][:C]).astype(BF16) for h in range(H)]
        state = [state[h] * ops[h][c][4] + _dot(ops[h][c][3], v_new[h]) for h in range(H)]
        for h in range(H):
            o_l[h].append(ws[h][C:] + _dot(ops[h][c][2], v_new[h]))
    for h in range(H):
        s_ref[h] = state[h]

    z = z_ref[...]
    for h in range(H):
        cols = slice(h * Dh, (h + 1) * Dh)
        o = jnp.concatenate(o_l[h], axis=0)
        y_ref[:, cols] = (_rms_rows(o, og_ref[...]) * _silu(z[:, cols])).astype(BF16)


def _gdn(pg, pab, conv_w, alog_vec, dtb_vec, out_gain, batch, seq):
    n = pg.shape[0]
    rows = GDN_ROWS
    tiles = seq // rows
    W = GDN_WIDTH
    return pl.pallas_call(
        _gdn_kernel,
        grid=(batch, tiles),
        in_specs=[pl.BlockSpec((rows, 3 * W), lambda b, t: (b * tiles + t, 0)),
                  pl.BlockSpec((rows, W), lambda b, t: (b * tiles + t, 3)),
                  pl.BlockSpec((rows, pab.shape[1]), lambda b, t: (b * tiles + t, 0)),
                  _resident(), _resident(), _resident(), _resident()],
        out_specs=pl.BlockSpec((rows, W), lambda b, t: (b * tiles + t, 0)),
        out_shape=jax.ShapeDtypeStruct((n, W), BF16),
        scratch_shapes=[pltpu.VMEM((rows + 8, 3 * W), F32),
                        pltpu.VMEM((GDN_HEADS, GDN_HEAD_DIM, GDN_HEAD_DIM), F32)],
        compiler_params=pltpu.CompilerParams(dimension_semantics=("arbitrary", "arbitrary"),
                                             vmem_limit_bytes=V7X_VMEM_LIMIT_BYTES),
        name="gdn",
    )(pg, pg, pab, conv_w, alog_vec, dtb_vec, out_gain)


def _alibi_slope(h):
    return 2.0 ** (-8.0 * (h + 1) / ATT_HEADS)


def _moba_kernel(p_ref, qg_ref, kg_ref, cg_ref, km_ref, vmt_ref, y_ref,
                 k_ref, vt_ref, kmean_ref, sel_ref, m_ref, acc_ref, s_ref, kaug_ref, rhs_ref):
    H, Dh, BLK, Wd = ATT_HEADS, ATT_HEAD_DIM, MOBA_BLOCK, ATT_WIDTH
    PAIR = 2 * Dh
    AUG = V7X_BF16_SUBLANES
    nb = kmean_ref.shape[0]
    i = pl.program_id(1)
    pair_row_head = lax.broadcasted_iota(jnp.int32, (PAIR, BLK), 0) // Dh

    @pl.when((pl.program_id(0) == 0) & (i == 0))
    def _():
        lane = lax.broadcasted_iota(jnp.int32, (BLK, V7X_LANES), 1)
        key_pos = lax.broadcasted_iota(jnp.int32, (BLK, V7X_LANES), 0).astype(F32)
        kaug_ref[...] = jnp.where(lane < BIAS_TERMS, key_pos,
                                  jnp.where(lane < 2 * BIAS_TERMS, 1.0, 0.0)).astype(BF16)
        rhs_ref[...] = jnp.zeros_like(rhs_ref)

    @pl.when(i == 0)
    def _():
        kmean_ref[...] = jnp.zeros_like(kmean_ref)

    def pair_lanes(h):
        return slice((h // 2) * PAIR, (h // 2 + 1) * PAIR)

    def value_rows(h):
        return slice(h * ATT_VROWS, (h + 1) * ATT_VROWS)

    def normalised(acc):
        return acc[0:Dh] / acc[Dh:Dh + 1]

    def head_queries(q_t, h):
        qp = q_t[pair_lanes(h), :]
        return jnp.where(pair_row_head == h % 2, qp, jnp.zeros_like(qp))

    block_ones = _head_block_ones(Wd, Dh)
    qn = _head_rms(p_ref[:, 0:Wd], qg_ref[...], block_ones, Dh)
    kn = _head_rms(p_ref[:, Wd:2 * Wd], kg_ref[...], block_ones, Dh)
    cqn = _head_rms(p_ref[:, 3 * Wd:4 * Wd], cg_ref[...], block_ones, Dh)
    scale = Dh ** -0.5 * LOG2_E
    q_t = (qn * scale).T.astype(BF16)
    cq_t = (cqn * scale).T.astype(BF16)
    k_ref[pl.ds(pl.multiple_of(i * BLK, BLK), BLK), :] = kn.astype(BF16)
    vt_ref[i] = _values_with_ones(p_ref[:, 2 * Wd:3 * Wd].T)
    for h in range(H):
        rhs_ref[h, 0:PAIR, :] = head_queries(q_t, h)

    kmean = kmean_ref[...]
    blk_row = lax.broadcasted_iota(jnp.int32, (nb, Wd), 0)
    lane_head = lax.broadcasted_iota(jnp.int32, (nb, Wd), 1) // Dh
    kmean_heads = jnp.concatenate([jnp.where(lane_head == h, kmean, 0.0) for h in range(H)], axis=0)
    gate_all = _dot_nt(kmean_heads, qn, HIGHEST)
    jdx_f = lax.broadcasted_iota(jnp.int32, (nb, BLK), 0).astype(F32)
    past = lax.broadcasted_iota(jnp.int32, (nb, BLK), 0) < i
    for h in range(H):
        gate = jnp.where(past, gate_all[h * nb:(h + 1) * nb], -jnp.inf)
        chosen = jnp.zeros(gate.shape, jnp.bool_)
        for _ in range(MOBA_TOPK):
            top = jnp.max(gate, axis=0, keepdims=True)
            first = jnp.min(jnp.where(gate == top, jdx_f, float(nb)), axis=0, keepdims=True)
            pick = jdx_f == first
            chosen = chosen | pick
            gate = jnp.where(pick, -jnp.inf, gate)
        sel_ref[h] = jnp.where(chosen & past, 0.0, MASKED)
    kmean_ref[...] = jnp.where(blk_row == i, jnp.mean(kn, axis=0, keepdims=True), kmean)

    aug_row = lax.broadcasted_iota(jnp.int32, (AUG, BLK), 0)
    query_pos = lax.broadcasted_iota(jnp.int32, (1, BLK), 1).astype(F32)

    def scores(blk, own):
        r0 = pl.multiple_of(blk * BLK, BLK)
        distance = (i - blk).astype(F32) * BLK
        keys_aug = kaug_ref[...]
        out = []
        for h in range(H):
            c = _alibi_slope(h) * LOG2_E
            row_bias = -c * (query_pos + distance)
            if not own:
                row_bias = row_bias + sel_ref[h, pl.ds(blk, 1), :]
            rows = _bf16_terms_const(c) + _bf16_terms(row_bias)
            aug = jnp.zeros((AUG, BLK), F32)
            for n, val in enumerate(rows):
                aug = jnp.where(aug_row == n, val, aug)
            rhs_ref[h, PAIR:PAIR + AUG, :] = aug.astype(BF16)
            keys = jnp.concatenate([k_ref[pl.ds(r0, BLK), pair_lanes(h)], keys_aug], axis=1)
            out.append(_dot(keys, rhs_ref[h]))
        return out

    own_scores = scores(i, True)
    mem_scores = [_dot(km_ref[:, pair_lanes(h)], head_queries(cq_t, h)) for h in range(H)]
    first_past = scores(0, False)
    causal = (lax.broadcasted_iota(jnp.int32, (BLK, BLK), 0) <= lax.broadcasted_iota(jnp.int32, (BLK, BLK), 1))
    own_p, mem_p = [], []
    for h in range(H):
        s_t = jnp.where(causal, own_scores[h], MASKED)
        m = jnp.max(s_t, axis=0, keepdims=True)
        p = jnp.exp2(s_t - m)
        m_ref[h] = m
        own_p.append(p.astype(BF16))
    for h in range(H):
        p = jnp.exp2(mem_scores[h] - jnp.max(mem_scores[h], axis=0, keepdims=True))
        mem_p.append(p.astype(BF16))
    for h in range(H):
        acc_ref[h] = _dot(vt_ref[i, value_rows(h), :], own_p[h])
        s_ref[0, h] = first_past[h]
    for h in range(H):
        acc_ref[H + h] = _dot(vmt_ref[value_rows(h), :], mem_p[h])

    def block_step(j, slot, nxt):
        next_scores = None if nxt is None else scores(nxt, False)
        for h in range(H):
            s_t = s_ref[slot, h]
            m_old = m_ref[h]
            m_new = jnp.maximum(m_old, jnp.max(s_t, axis=0, keepdims=True))
            p = jnp.exp2(s_t - m_new)
            alpha = jnp.exp2(m_old - m_new)
            acc_ref[h] = alpha * acc_ref[h] + _dot(vt_ref[j, value_rows(h), :], p.astype(BF16))
            m_ref[h] = m_new
        if next_scores is not None:
            for h in range(H):
                s_ref[1 - slot, h] = next_scores[h]

    last_past = jnp.maximum(i - 1, 0)

    def block_run(j0, count):
        for n in range(count):
            block_step(j0 + n, n % 2, jnp.minimum(j0 + n + 1, last_past))

    def block_octet(jj, carry):
        block_run(8 * jj, 8)
        return carry

    lax.fori_loop(0, i // 8, block_octet, 0)

    @pl.when(i % 8 >= 4)
    def _():
        block_run(8 * (i // 8), 4)

    @pl.when(i % 4 >= 2)
    def _():
        block_run(4 * (i // 4), 2)

    @pl.when(i % 2 == 1)
    def _():
        block_step(i - 1, 0, None)

    out_t = jnp.concatenate([normalised(acc_ref[a]) for a in range(2 * H)], axis=0)
    y_ref[...] = out_t.T.astype(BF16)


def _moba(patt, qg, kg, cg, kmem, vmem_t, batch, seq):
    nb = seq // MOBA_BLOCK
    Wd, H, BLK = ATT_WIDTH, ATT_HEADS, MOBA_BLOCK
    n_mem = kmem.shape[1]
    return pl.pallas_call(
        _moba_kernel,
        grid=(batch, nb),
        in_specs=[pl.BlockSpec((BLK, 4 * Wd), lambda b, i: (b * nb + i, 0)),
                  _resident(), _resident(), _resident(),
                  pl.BlockSpec((None, n_mem, Wd), lambda b, i: (b, 0, 0)),
                  pl.BlockSpec((None, H * ATT_VROWS, n_mem), lambda b, i: (b, 0, 0))],
        out_specs=pl.BlockSpec((BLK, 2 * Wd), lambda b, i: (b * nb + i, 0)),
        out_shape=jax.ShapeDtypeStruct((batch * seq, 2 * Wd), BF16),
        scratch_shapes=[pltpu.VMEM((seq, Wd), BF16),
                        pltpu.VMEM((nb, H * ATT_VROWS, BLK), BF16),
                        pltpu.VMEM((nb, Wd), F32),
                        pltpu.VMEM((H, nb, BLK), F32),
                        pltpu.VMEM((H, 1, BLK), F32),
                        pltpu.VMEM((2 * H, ATT_VROWS, BLK), F32),
                        pltpu.VMEM((2, H, BLK, BLK), F32),
                        pltpu.VMEM((BLK, V7X_LANES), BF16),
                        pltpu.VMEM((H, 2 * ATT_HEAD_DIM + V7X_LANES, BLK), BF16)],
        compiler_params=pltpu.CompilerParams(dimension_semantics=("arbitrary", "arbitrary"),
                                             vmem_limit_bytes=V7X_VMEM_LIMIT_BYTES),
        name="moba",
    )(patt, qg, kg, cg, kmem, vmem_t)


def _lane_vector(values, width):
    return jnp.zeros((1, width), F32).at[0, :values.shape[0]].set(values.astype(F32))


def _layer(x2d, mem, batch, seq, ffn1_norm, ffn1_w_gate, ffn1_w_up, ffn1_w_down, mix_norm, w_in,
           gdn_conv_w, gdn_a_log, gdn_dt_bias, gdn_out_norm, moba_q_norm, moba_k_norm,
           mem_norm, w_mem_kv, mem_q_norm, mem_k_norm, w_out, ffn2_norm, ffn2_w_gate, ffn2_w_up, ffn2_w_down):
    W, H = GDN_WIDTH, GDN_HEADS
    row = lambda v: v.reshape(1, -1).astype(F32)
    tile_heads = lambda v: jnp.tile(v.astype(F32), ATT_HEADS).reshape(1, -1)
    ab0 = 4 * W
    att0 = ab0 + 2 * H
    ab_width = V7X_LANES
    w_in_sections = jnp.concatenate(
        [w_in[:, :ab0], w_in[:, ab0:att0], jnp.zeros((w_in.shape[0], ab_width - 2 * H), w_in.dtype),
         w_in[:, att0:]], axis=1).astype(BF16)

    x1, pg, pab, patt = _ffn1_inproj(
        x2d, row(ffn1_norm), ffn1_w_gate.astype(BF16), ffn1_w_up.astype(BF16), ffn1_w_down.astype(BF16),
        row(mix_norm), w_in_sections, ab0, ab_width)

    y_gdn = _gdn(pg, pab, gdn_conv_w.astype(F32), _lane_vector(gdn_a_log, ab_width),
                 _lane_vector(gdn_dt_bias, ab_width), row(gdn_out_norm), batch, seq)

    kmem, vmem_t = _mem_kv(mem, row(mem_norm), w_mem_kv.astype(BF16), tile_heads(mem_k_norm))
    y_att = _moba(patt, tile_heads(moba_q_norm), tile_heads(moba_k_norm), tile_heads(mem_q_norm),
                  kmem, vmem_t, batch, seq)

    return _outproj_ffn2(x1, y_gdn, y_att, w_out.astype(BF16), row(ffn2_norm),
                         ffn2_w_gate.astype(BF16), ffn2_w_up.astype(BF16), ffn2_w_down.astype(BF16))


def kernel(x, mem, ffn1_norm, ffn1_w_gate, ffn1_w_up, ffn1_w_down, mix_norm, w_in, gdn_conv_w, gdn_a_log,
           gdn_dt_bias, gdn_out_norm, moba_q_norm, moba_k_norm, mem_norm, w_mem_kv, mem_q_norm, mem_k_norm,
           w_out, ffn2_norm, ffn2_w_gate, ffn2_w_up, ffn2_w_down):
    batch, seq, d = x.shape
    assert seq % MOBA_BLOCK == 0 and seq % GDN_ROWS == 0 and (batch * seq) % FFN_ROWS == 0
    depth = w_in.shape[0]
    x2d = x.reshape(batch * seq, d)
    for l in range(depth):
        x2d = _layer(x2d, mem, batch, seq, ffn1_norm[l], ffn1_w_gate[l], ffn1_w_up[l], ffn1_w_down[l],
                     mix_norm[l], w_in[l], gdn_conv_w[l], gdn_a_log[l], gdn_dt_bias[l], gdn_out_norm[l],
                     moba_q_norm[l], moba_k_norm[l], mem_norm[l], w_mem_kv[l], mem_q_norm[l], mem_k_norm[l],
                     w_out[l], ffn2_norm[l], ffn2_w_gate[l], ffn2_w_up[l], ffn2_w_down[l])
    return x2d.reshape(batch, seq, d)
```

```python
import struct

import jax
import jax.numpy as jnp
from jax import lax
from jax.experimental import pallas as pl
from jax.experimental.pallas import tpu as pltpu

F32 = jnp.float32
BF16 = jnp.bfloat16
HIGHEST = lax.Precision.HIGHEST

NORM_EPS = 1e-6
D_FF = 2816
GDN_HEADS = 4
GDN_HEAD_DIM = 128
GDN_WIDTH = GDN_HEADS * GDN_HEAD_DIM
GDN_CONV = 4
GDN_CHUNK = 64
GDN_GROUP = 2 * GDN_CHUNK
ATT_HEADS = 4
ATT_HEAD_DIM = 64
ATT_WIDTH = ATT_HEADS * ATT_HEAD_DIM
MOBA_BLOCK = 256
MOBA_TOPK = 3
MASKED = -1e30
LOG2_E = 1.4426950408889634
BIAS_TERMS = 3

V7X_VMEM_LIMIT_BYTES = 56 * 1024 * 1024
V7X_LANES = 128
V7X_BF16_SUBLANES = 16
ATT_VROWS = ATT_HEAD_DIM + V7X_BF16_SUBLANES
FFN_ROWS = 256
FFN1_SLABS = 2
FFN2_SLABS = 4
GDN_ROWS = 256
GDN_SEQS = 4


def _dot(a, b, precision=None):
    return jnp.dot(a, b, preferred_element_type=F32, precision=precision)


def _dot_nt(a, b, precision=None):
    return lax.dot_general(a, b, (((1,), (1,)), ((), ())), preferred_element_type=F32,
                           precision=precision)


def _rms_rows(x, gain):
    return x * lax.rsqrt(jnp.mean(x * x, axis=-1, keepdims=True) + NORM_EPS) * gain


def _silu(x):
    half = 0.5 * x
    return half + half * jnp.tanh(half)


def _swiglu(x, gain, wg_ref, wu_ref, wd_ref):
    h = _rms_rows(x, gain).astype(BF16)
    g = _dot(h, wg_ref[...])
    u = _dot(h, wu_ref[...])
    return _dot((_silu(g) * u).astype(BF16), wd_ref[...])


def _resident():
    return pl.BlockSpec(memory_space=pltpu.VMEM)


def _bf16_terms(x):
    terms, rest = [], x
    for _ in range(BIAS_TERMS):
        t = rest.astype(BF16).astype(F32)
        terms.append(t)
        rest = rest - t
    return terms


def _bf16_terms_const(value):
    def f32(v):
        return struct.unpack("<f", struct.pack("<f", v))[0]

    def bf16_round(v):
        bits = struct.unpack("<I", struct.pack("<f", v))[0]
        bits = (bits + 0x7FFF + ((bits >> 16) & 1)) & 0xFFFF0000
        return struct.unpack("<f", struct.pack("<I", bits))[0]

    terms, rest = [], f32(value)
    for _ in range(BIAS_TERMS):
        t = bf16_round(rest)
        terms.append(t)
        rest = f32(rest - t)
    return terms


def _ffn1_inproj_kernel(x_ref, n1_ref, wg_ref, wu_ref, wd_ref, n2_ref, win_ref,
                        x1_ref, pg_ref, pab_ref, patt_ref):
    gdn_w, ab_w = pg_ref.shape[1], pab_ref.shape[1]
    for r0 in range(0, x_ref.shape[0], FFN_ROWS):
        rows = slice(r0, r0 + FFN_ROWS)
        x = x_ref[rows, :]
        x1 = x + 0.5 * _swiglu(x, n1_ref[...], wg_ref, wu_ref, wd_ref)
        x1_ref[rows, :] = x1
        h = _rms_rows(x1, n2_ref[...]).astype(BF16)
        pg_ref[rows, :] = _dot(h, win_ref[:, 0:gdn_w])
        pab_ref[rows, :] = _dot(h, win_ref[:, gdn_w:gdn_w + ab_w])
        patt_ref[rows, :] = _dot(h, win_ref[:, gdn_w + ab_w:])


def _ffn1_inproj(x2d, n1, wg, wu, wd, n2, win, gdn_w, ab_w):
    n, d = x2d.shape
    att_w = win.shape[1] - gdn_w - ab_w
    rows = FFN1_SLABS * FFN_ROWS
    row_spec = lambda w: pl.BlockSpec((rows, w), lambda i: (i, 0))
    return pl.pallas_call(
        _ffn1_inproj_kernel,
        grid=(n // rows,),
        in_specs=[row_spec(d)] + [_resident()] * 6,
        out_specs=[row_spec(d), row_spec(gdn_w), row_spec(ab_w), row_spec(att_w)],
        out_shape=[jax.ShapeDtypeStruct((n, d), F32),
                   jax.ShapeDtypeStruct((n, gdn_w), F32),
                   jax.ShapeDtypeStruct((n, ab_w), F32),
                   jax.ShapeDtypeStruct((n, att_w), F32)],
        compiler_params=pltpu.CompilerParams(dimension_semantics=("arbitrary",),
                                             vmem_limit_bytes=V7X_VMEM_LIMIT_BYTES),
        name="ffn1_inproj",
    )(x2d, n1, wg, wu, wd, n2, win)


def _outproj_ffn2_kernel(x1_ref, yg_ref, ya_ref, wo_ref, n_ref, wg_ref, wu_ref, wd_ref, o_ref):
    gdn_w = yg_ref.shape[1]
    for r0 in range(0, x1_ref.shape[0], FFN_ROWS):
        rows = slice(r0, r0 + FFN_ROWS)
        x2 = (x1_ref[rows, :] + _dot(yg_ref[rows, :], wo_ref[0:gdn_w, :])
              + _dot(ya_ref[rows, :], wo_ref[gdn_w:, :]))
        o_ref[rows, :] = x2 + 0.5 * _swiglu(x2, n_ref[...], wg_ref, wu_ref, wd_ref)


def _outproj_ffn2(x1, yg, ya, wo, nrm, wg, wu, wd):
    n, d = x1.shape
    rows = FFN2_SLABS * FFN_ROWS
    row_spec = lambda w: pl.BlockSpec((rows, w), lambda i: (i, 0))
    return pl.pallas_call(
        _outproj_ffn2_kernel,
        grid=(n // rows,),
        in_specs=[row_spec(d), row_spec(yg.shape[1]), row_spec(ya.shape[1])] + [_resident()] * 5,
        out_specs=row_spec(d),
        out_shape=jax.ShapeDtypeStruct((n, d), F32),
        compiler_params=pltpu.CompilerParams(dimension_semantics=("arbitrary",),
                                             vmem_limit_bytes=V7X_VMEM_LIMIT_BYTES),
        name="outproj_ffn2",
    )(x1, yg, ya, wo, nrm, wg, wu, wd)


def _head_block_ones(width, head_dim):
    r = lax.broadcasted_iota(jnp.int32, (width, width), 0) // head_dim
    c = lax.broadcasted_iota(jnp.int32, (width, width), 1) // head_dim
    return jnp.where(r == c, 1.0, 0.0).astype(BF16)


def _head_rms(x, gain, block_ones, head_dim):
    sq = x * x
    hi = sq.astype(BF16)
    lo = (sq - hi.astype(F32)).astype(BF16)
    ss = _dot(hi, block_ones) + _dot(lo, block_ones)
    return x * lax.rsqrt(ss * (1.0 / head_dim) + NORM_EPS) * gain


def _values_with_ones(v_t):
    Dh = ATT_HEAD_DIM
    row = lax.broadcasted_iota(jnp.int32, (V7X_BF16_SUBLANES, v_t.shape[1]), 0)
    ones_block = jnp.where(row == 0, 1.0, 0.0)
    parts = []
    for h in range(ATT_HEADS):
        parts += [v_t[h * Dh:(h + 1) * Dh], ones_block]
    return jnp.concatenate(parts, axis=0).astype(BF16)


def _mem_kv_kernel(mem_ref, nrm_ref, wkv_ref, kg_ref, k_ref, vt_ref):
    h = _rms_rows(mem_ref[...], nrm_ref[...]).astype(BF16)
    kv = _dot(h, wkv_ref[...])
    k = _head_rms(kv[:, :ATT_WIDTH], kg_ref[...], _head_block_ones(ATT_WIDTH, ATT_HEAD_DIM), ATT_HEAD_DIM)
    k_ref[...] = k.astype(BF16)
    vt_ref[...] = _values_with_ones(kv[:, ATT_WIDTH:].T)


def _mem_kv(mem, nrm, wkv, kgain):
    b, m, d = mem.shape
    return pl.pallas_call(
        _mem_kv_kernel,
        grid=(b,),
        in_specs=[pl.BlockSpec((None, m, d), lambda i: (i, 0, 0)), _resident(), _resident(), _resident()],
        out_specs=[pl.BlockSpec((None, m, ATT_WIDTH), lambda i: (i, 0, 0)),
                   pl.BlockSpec((None, ATT_HEADS * ATT_VROWS, m), lambda i: (i, 0, 0))],
        out_shape=[jax.ShapeDtypeStruct((b, m, ATT_WIDTH), BF16),
                   jax.ShapeDtypeStruct((b, ATT_HEADS * ATT_VROWS, m), BF16)],
        compiler_params=pltpu.CompilerParams(dimension_semantics=("arbitrary",)),
        name="mem_kv",
    )(mem, nrm, wkv, kgain)


def _softplus(x):
    return jnp.maximum(x, 0.0) + jnp.log(1.0 + jnp.exp(-jnp.abs(x)))


def _unit_lower_inverses(a_list, eye):
    x16 = [(-a).astype(BF16) for a in a_list]
    t = [eye - a for a in a_list]
    for _ in range(5):
        x16 = [_dot(xb, xb).astype(BF16) for xb in x16]
        t = [ti + _dot(ti.astype(BF16), xb) for ti, xb in zip(t, x16)]
    t16 = [ti.astype(BF16) for ti in t]
    a_hi = [a.astype(BF16) for a in a_list]
    a_lo = [(a - hi.astype(F32)).astype(BF16) for a, hi in zip(a_list, a_hi)]
    resid = [(eye - tb.astype(F32) - _dot(hi, tb) - _dot(lo, tb)).astype(BF16)
             for tb, hi, lo in zip(t16, a_hi, a_lo)]
    return [tb.astype(F32) + _dot(tb, r) for tb, r in zip(t16, resid)]


def _gdn_kernel(qkv_ref, z_ref, ab_ref, cw_ref, alog_ref, dtb_ref, og_ref, y_ref, xbuf_ref, s_ref):
    n_seq, rows = z_ref.shape[0], z_ref.shape[1]
    H, Dh, C, W = GDN_HEADS, GDN_HEAD_DIM, GDN_CHUNK, GDN_WIDTH
    G = GDN_GROUP
    n_chunks = rows // C
    n_groups = rows // G

    @pl.when(pl.program_id(1) == 0)
    def _():
        xbuf_ref[:, 0:8, :] = jnp.zeros((n_seq, 8, 3 * W), F32)
        s_ref[...] = jnp.zeros_like(s_ref)

    ri = lax.broadcasted_iota(jnp.int32, (G, G), 0)
    ci = lax.broadcasted_iota(jnp.int32, (G, G), 1)
    same_chunk = (ri // C) == (ci // C)
    lower = same_chunk & (ri >= ci)
    strict = same_chunk & (ri > ci)
    eye = (ri == ci).astype(F32)
    lower_f = lower.astype(F32)

    chains = [(b, h) for b in range(n_seq) for h in range(H)]
    q_c, k_c, v_c, gcol_c, beta_c, grow_c = [], [], [], [], [], []
    for b in range(n_seq):
        xbuf_ref[b, 8:, :] = qkv_ref[b]
        conv = cw_ref[0:1, :] * xbuf_ref[b, 5:5 + rows, :]
        for j in range(1, GDN_CONV):
            conv = conv + cw_ref[j:j + 1, :] * xbuf_ref[b, 5 + j:5 + j + rows, :]
        xbuf_ref[b, 0:8, :] = xbuf_ref[b, rows:rows + 8, :]
        act = _silu(conv)

        ab = ab_ref[b]
        log_decay = -jnp.exp(alog_ref[...]) * _softplus(ab + dtb_ref[...])
        beta_all = jax.nn.sigmoid(ab)
        gc_all = jnp.concatenate(
            [_dot(lower_f, log_decay[g * G:(g + 1) * G], HIGHEST) for g in range(n_groups)], axis=0)
        gc_rows = gc_all.T
        for h in range(H):
            qh = act[:, h * Dh:(h + 1) * Dh]
            kh = act[:, W + h * Dh:W + (h + 1) * Dh]
            v_c.append(act[:, 2 * W + h * Dh:2 * W + (h + 1) * Dh])
            q_c.append(qh * (lax.rsqrt(jnp.sum(qh * qh, axis=-1, keepdims=True) + NORM_EPS) * Dh ** -0.5))
            k_c.append(kh * lax.rsqrt(jnp.sum(kh * kh, axis=-1, keepdims=True) + NORM_EPS))
            gcol_c.append(jnp.broadcast_to(gc_all[:, h:h + 1], (rows, Dh)))
            beta_c.append(jnp.broadcast_to(beta_all[:, H + h:H + h + 1], (rows, Dh)))
            grow_c.append(gc_rows[h:h + 1, :])

    units = [(c, g) for c in range(len(chains)) for g in range(n_groups)]

    def unit_rows(x, g):
        return x[g * G:(g + 1) * G]

    k_u = [unit_rows(k_c[c], g) for c, g in units]
    gcol_u = [unit_rows(gcol_c[c], g) for c, g in units]
    beta_u = [unit_rows(beta_c[c], g) for c, g in units]
    kb_u = [k * beta for k, beta in zip(k_u, beta_u)]
    k16_u = [k.astype(BF16) for k in k_u]
    q_u = [unit_rows(q_c[c], g) for c, g in units]
    kk_u = [_dot_nt(kb.astype(BF16), k16) for kb, k16 in zip(kb_u, k16_u)]
    qk_u = [_dot_nt(q.astype(BF16), k16) for q, k16 in zip(q_u, k16_u)]
    decay_u = []
    for gcol, (c, g) in zip(gcol_u, units):
        diff = gcol - grow_c[c][:, g * G:(g + 1) * G]
        decay_u.append(jnp.where(lower, jnp.exp(jnp.where(lower, diff, 0.0)), 0.0))
    a_u = [jnp.where(strict, kk * decay, 0.0) for kk, decay in zip(kk_u, decay_u)]
    qk16_u = [(qk * decay).astype(BF16) for qk, decay in zip(qk_u, decay_u)]
    tinv_u = _unit_lower_inverses(a_u, eye)
    egc_u = [jnp.exp(gcol) for gcol in gcol_u]
    rhs_u = [jnp.concatenate([unit_rows(v_c[c], g) * beta, kb * egc], axis=1).astype(BF16)
             for beta, kb, egc, (c, g) in zip(beta_u, kb_u, egc_u, units)]
    uw_u = [_dot(tinv.astype(BF16), rhs) for tinv, rhs in zip(tinv_u, rhs_u)]
    w16_u = [uw[:, Dh:].astype(BF16) for uw in uw_u]
    qe16_u = [(q * egc).astype(BF16) for q, egc in zip(q_u, egc_u)]

    def chunk_operands(c, n):
        u = c * n_groups + (n * C) // G
        r = slice((n * C) % G, (n * C) % G + C)
        g_last = gcol_u[u][r.stop - 1:r.stop, :]
        return (uw_u[u][r, :Dh],
                jnp.concatenate([w16_u[u][r], qe16_u[u][r]], axis=0),
                qk16_u[u][r, r],
                (k_u[u][r] * jnp.exp(g_last - gcol_u[u][r])).T.astype(BF16),
                jnp.exp(g_last))

    n_chains = len(chains)
    ops = [[chunk_operands(c, n) for n in range(n_chunks)] for c in range(n_chains)]

    state = [s_ref[c] for c in range(n_chains)]
    o_l = [[] for _ in range(n_chains)]
    for n in range(n_chunks):
        ws = [_dot(ops[c][n][1], state[c].astype(BF16)) for c in range(n_chains)]
        v_new = [(ops[c][n][0] - ws[c][:C]).astype(BF16) for c in range(n_chains)]
        state = [state[c] * ops[c][n][4] + _dot(ops[c][n][3], v_new[c]) for c in range(n_chains)]
        for c in range(n_chains):
            o_l[c].append(ws[c][C:] + _dot(ops[c][n][2], v_new[c]))
    for c, (b, h) in enumerate(chains):
        s_ref[c] = state[c]
        o = jnp.concatenate(o_l[c], axis=0)
        z = z_ref[b, :, h * Dh:(h + 1) * Dh]
        y_ref[b, :, h * Dh:(h + 1) * Dh] = (_rms_rows(o, og_ref[...]) * _silu(z)).astype(BF16)


def _gdn(pg, pab, conv_w, alog_vec, dtb_vec, out_gain, batch, seq):
    rows = GDN_ROWS
    n_seq = GDN_SEQS
    W = GDN_WIDTH
    pg3 = pg.reshape(batch, seq, pg.shape[1])
    pab3 = pab.reshape(batch, seq, pab.shape[1])
    y = pl.pallas_call(
        _gdn_kernel,
        grid=(batch // n_seq, seq // rows),
        in_specs=[pl.BlockSpec((n_seq, rows, 3 * W), lambda b, t: (b, t, 0)),
                  pl.BlockSpec((n_seq, rows, W), lambda b, t: (b, t, 3)),
                  pl.BlockSpec((n_seq, rows, pab.shape[1]), lambda b, t: (b, t, 0)),
                  _resident(), _resident(), _resident(), _resident()],
        out_specs=pl.BlockSpec((n_seq, rows, W), lambda b, t: (b, t, 0)),
        out_shape=jax.ShapeDtypeStruct((batch, seq, W), BF16),
        scratch_shapes=[pltpu.VMEM((n_seq, rows + 8, 3 * W), F32),
                        pltpu.VMEM((n_seq * GDN_HEADS, GDN_HEAD_DIM, GDN_HEAD_DIM), F32)],
        compiler_params=pltpu.CompilerParams(dimension_semantics=("arbitrary", "arbitrary"),
                                             vmem_limit_bytes=V7X_VMEM_LIMIT_BYTES),
        name="gdn",
    )(pg3, pg3, pab3, conv_w, alog_vec, dtb_vec, out_gain)
    return y.reshape(batch * seq, W)


def _alibi_slope(h):
    return 2.0 ** (-8.0 * (h + 1) / ATT_HEADS)


def _moba_kernel(p_ref, qg_ref, kg_ref, cg_ref, km_ref, vmt_ref, y_ref,
                 k_ref, vt_ref, kmean_ref, sel_ref, m_ref, acc_ref, s_ref, kaug_ref, rhs_ref):
    H, Dh, BLK, Wd = ATT_HEADS, ATT_HEAD_DIM, MOBA_BLOCK, ATT_WIDTH
    PAIR = 2 * Dh
    AUG = V7X_BF16_SUBLANES
    nb = kmean_ref.shape[0]
    i = pl.program_id(1)
    pair_row_head = lax.broadcasted_iota(jnp.int32, (PAIR, BLK), 0) // Dh

    @pl.when((pl.program_id(0) == 0) & (i == 0))
    def _():
        lane = lax.broadcasted_iota(jnp.int32, (BLK, V7X_LANES), 1)
        key_pos = lax.broadcasted_iota(jnp.int32, (BLK, V7X_LANES), 0).astype(F32)
        kaug_ref[...] = jnp.where(lane < BIAS_TERMS, key_pos,
                                  jnp.where(lane < 2 * BIAS_TERMS, 1.0, 0.0)).astype(BF16)
        rhs_ref[...] = jnp.zeros_like(rhs_ref)

    @pl.when(i == 0)
    def _():
        kmean_ref[...] = jnp.zeros_like(kmean_ref)

    def pair_lanes(h):
        return slice((h // 2) * PAIR, (h // 2 + 1) * PAIR)

    def value_rows(h):
        return slice(h * ATT_VROWS, (h + 1) * ATT_VROWS)

    def normalised(acc):
        return acc[0:Dh] / acc[Dh:Dh + 1]

    def head_queries(q_t, h):
        qp = q_t[pair_lanes(h), :]
        return jnp.where(pair_row_head == h % 2, qp, jnp.zeros_like(qp))

    block_ones = _head_block_ones(Wd, Dh)
    qn = _head_rms(p_ref[:, 0:Wd], qg_ref[...], block_ones, Dh)
    kn = _head_rms(p_ref[:, Wd:2 * Wd], kg_ref[...], block_ones, Dh)
    cqn = _head_rms(p_ref[:, 3 * Wd:4 * Wd], cg_ref[...], block_ones, Dh)
    scale = Dh ** -0.5 * LOG2_E
    q_t = (qn * scale).T.astype(BF16)
    cq_t = (cqn * scale).T.astype(BF16)
    k_ref[pl.ds(pl.multiple_of(i * BLK, BLK), BLK), :] = kn.astype(BF16)
    vt_ref[i] = _values_with_ones(p_ref[:, 2 * Wd:3 * Wd].T)
    for h in range(H):
        rhs_ref[h, 0:PAIR, :] = head_queries(q_t, h)

    kmean = kmean_ref[...]
    blk_row = lax.broadcasted_iota(jnp.int32, (nb, Wd), 0)
    lane_head = lax.broadcasted_iota(jnp.int32, (nb, Wd), 1) // Dh
    kmean_heads = jnp.concatenate([jnp.where(lane_head == h, kmean, 0.0) for h in range(H)], axis=0)
    gate_all = _dot_nt(kmean_heads, qn, HIGHEST)
    jdx_f = lax.broadcasted_iota(jnp.int32, (nb, BLK), 0).astype(F32)
    past = lax.broadcasted_iota(jnp.int32, (nb, BLK), 0) < i
    for h in range(H):
        gate = jnp.where(past, gate_all[h * nb:(h + 1) * nb], -jnp.inf)
        chosen = jnp.zeros(gate.shape, jnp.bool_)
        for _ in range(MOBA_TOPK):
            top = jnp.max(gate, axis=0, keepdims=True)
            first = jnp.min(jnp.where(gate == top, jdx_f, float(nb)), axis=0, keepdims=True)
            pick = jdx_f == first
            chosen = chosen | pick
            gate = jnp.where(pick, -jnp.inf, gate)
        sel_ref[h] = jnp.where(chosen & past, 0.0, MASKED)
    kmean_ref[...] = jnp.where(blk_row == i, jnp.mean(kn, axis=0, keepdims=True), kmean)

    aug_row = lax.broadcasted_iota(jnp.int32, (AUG, BLK), 0)
    query_pos = lax.broadcasted_iota(jnp.int32, (1, BLK), 1).astype(F32)

    def scores(blk, own):
        r0 = pl.multiple_of(blk * BLK, BLK)
        distance = (i - blk).astype(F32) * BLK
        keys_aug = kaug_ref[...]
        out = []
        for h in range(H):
            c = _alibi_slope(h) * LOG2_E
            row_bias = -c * (query_pos + distance)
            if not own:
                row_bias = row_bias + sel_ref[h, pl.ds(blk, 1), :]
            rows = _bf16_terms_const(c) + _bf16_terms(row_bias)
            aug = jnp.zeros((AUG, BLK), F32)
            for n, val in enumerate(rows):
                aug = jnp.where(aug_row == n, val, aug)
            rhs_ref[h, PAIR:PAIR + AUG, :] = aug.astype(BF16)
            keys = jnp.concatenate([k_ref[pl.ds(r0, BLK), pair_lanes(h)], keys_aug], axis=1)
            out.append(_dot(keys, rhs_ref[h]))
        return out

    own_scores = scores(i, True)
    mem_scores = [_dot(km_ref[:, pair_lanes(h)], head_queries(cq_t, h)) for h in range(H)]
    first_past = scores(0, False)
    causal = (lax.broadcasted_iota(jnp.int32, (BLK, BLK), 0) <= lax.broadcasted_iota(jnp.int32, (BLK, BLK), 1))
    own_p, mem_p = [], []
    for h in range(H):
        s_t = jnp.where(causal, own_scores[h], MASKED)
        m = jnp.max(s_t, axis=0, keepdims=True)
        p = jnp.exp2(s_t - m)
        m_ref[h] = m
        own_p.append(p.astype(BF16))
    for h in range(H):
        p = jnp.exp2(mem_scores[h] - jnp.max(mem_scores[h], axis=0, keepdims=True))
        mem_p.append(p.astype(BF16))
    for h in range(H):
        acc_ref[h] = _dot(vt_ref[i, value_rows(h), :], own_p[h])
        s_ref[0, h] = first_past[h]
    for h in range(H):
        acc_ref[H + h] = _dot(vmt_ref[value_rows(h), :], mem_p[h])

    def block_step(j, slot, nxt):
        next_scores = None if nxt is None else scores(nxt, False)
        for h in range(H):
            m_old = m_ref[h]
            m_new = jnp.maximum(m_old, jnp.max(s_ref[slot, h], axis=0, keepdims=True))
            m_ref[h] = m_new
            p = jnp.exp2(s_ref[slot, h] - m_new)
            alpha = jnp.exp2(m_old - m_new)
            acc_ref[h] = alpha * acc_ref[h] + _dot(vt_ref[j, value_rows(h), :], p.astype(BF16))
        if next_scores is not None:
            for h in range(H):
                s_ref[1 - slot, h] = next_scores[h]

    last_past = jnp.maximum(i - 1, 0)

    def block_run(j0, count):
        for n in range(count):
            block_step(j0 + n, n % 2, jnp.minimum(j0 + n + 1, last_past))

    def block_octet(jj, carry):
        block_run(8 * jj, 8)
        return carry

    lax.fori_loop(0, i // 8, block_octet, 0)

    @pl.when(i % 8 >= 4)
    def _():
        block_run(8 * (i // 8), 4)

    @pl.when(i % 4 >= 2)
    def _():
        block_run(4 * (i // 4), 2)

    @pl.when(i % 2 == 1)
    def _():
        block_step(i - 1, 0, None)

    out_t = jnp.concatenate([normalised(acc_ref[a]) for a in range(2 * H)], axis=0)
    y_ref[...] = out_t.T.astype(BF16)


def _moba(patt, qg, kg, cg, kmem, vmem_t, batch, seq):
    nb = seq // MOBA_BLOCK
    Wd, H, BLK = ATT_WIDTH, ATT_HEADS, MOBA_BLOCK
    n_mem = kmem.shape[1]
    return pl.pallas_call(
        _moba_kernel,
        grid=(batch, nb),
        in_specs=[pl.BlockSpec((BLK, 4 * Wd), lambda b, i: (b * nb + i, 0)),
                  _resident(), _resident(), _resident(),
                  pl.BlockSpec((None, n_mem, Wd), lambda b, i: (b, 0, 0)),
                  pl.BlockSpec((None, H * ATT_VROWS, n_mem), lambda b, i: (b, 0, 0))],
        out_specs=pl.BlockSpec((BLK, 2 * Wd), lambda b, i: (b * nb + i, 0)),
        out_shape=jax.ShapeDtypeStruct((batch * seq, 2 * Wd), BF16),
        scratch_shapes=[pltpu.VMEM((seq, Wd), BF16),
                        pltpu.VMEM((nb, H * ATT_VROWS, BLK), BF16),
                        pltpu.VMEM((nb, Wd), F32),
                        pltpu.VMEM((H, nb, BLK), F32),
                        pltpu.VMEM((H, 1, BLK), F32),
                        pltpu.VMEM((2 * H, ATT_VROWS, BLK), F32),
                        pltpu.VMEM((2, H, BLK, BLK), F32),
                        pltpu.VMEM((BLK, V7X_LANES), BF16),
                        pltpu.VMEM((H, 2 * ATT_HEAD_DIM + V7X_LANES, BLK), BF16)],
        compiler_params=pltpu.CompilerParams(dimension_semantics=("arbitrary", "arbitrary"),
                                             vmem_limit_bytes=V7X_VMEM_LIMIT_BYTES),
        name="moba",
    )(patt, qg, kg, cg, kmem, vmem_t)


def _lane_vector(values, width):
    return jnp.zeros((1, width), F32).at[0, :values.shape[0]].set(values.astype(F32))


def _layer(x2d, mem, batch, seq, ffn1_norm, ffn1_w_gate, ffn1_w_up, ffn1_w_down, mix_norm, w_in,
           gdn_conv_w, gdn_a_log, gdn_dt_bias, gdn_out_norm, moba_q_norm, moba_k_norm,
           mem_norm, w_mem_kv, mem_q_norm, mem_k_norm, w_out, ffn2_norm, ffn2_w_gate, ffn2_w_up, ffn2_w_down):
    W, H = GDN_WIDTH, GDN_HEADS
    row = lambda v: v.reshape(1, -1).astype(F32)
    tile_heads = lambda v: jnp.tile(v.astype(F32), ATT_HEADS).reshape(1, -1)
    ab0 = 4 * W
    att0 = ab0 + 2 * H
    ab_width = V7X_LANES
    w_in_sections = jnp.concatenate(
        [w_in[:, :ab0], w_in[:, ab0:att0], jnp.zeros((w_in.shape[0], ab_width - 2 * H), w_in.dtype),
         w_in[:, att0:]], axis=1).astype(BF16)

    x1, pg, pab, patt = _ffn1_inproj(
        x2d, row(ffn1_norm), ffn1_w_gate.astype(BF16), ffn1_w_up.astype(BF16), ffn1_w_down.astype(BF16),
        row(mix_norm), w_in_sections, ab0, ab_width)

    y_gdn = _gdn(pg, pab, gdn_conv_w.astype(F32), _lane_vector(gdn_a_log, ab_width),
                 _lane_vector(gdn_dt_bias, ab_width), row(gdn_out_norm), batch, seq)

    kmem, vmem_t = _mem_kv(mem, row(mem_norm), w_mem_kv.astype(BF16), tile_heads(mem_k_norm))
    y_att = _moba(patt, tile_heads(moba_q_norm), tile_heads(moba_k_norm), tile_heads(mem_q_norm),
                  kmem, vmem_t, batch, seq)

    return _outproj_ffn2(x1, y_gdn, y_att, w_out.astype(BF16), row(ffn2_norm),
                         ffn2_w_gate.astype(BF16), ffn2_w_up.astype(BF16), ffn2_w_down.astype(BF16))


def kernel(x, mem, ffn1_norm, ffn1_w_gate, ffn1_w_up, ffn1_w_down, mix_norm, w_in, gdn_conv_w, gdn_a_log,
           gdn_dt_bias, gdn_out_norm, moba_q_norm, moba_k_norm, mem_norm, w_mem_kv, mem_q_norm, mem_k_norm,
           w_out, ffn2_norm, ffn2_w_gate, ffn2_w_up, ffn2_w_down):
    batch, seq, d = x.shape
    assert seq % MOBA_BLOCK == 0 and seq % GDN_ROWS == 0 and batch % GDN_SEQS == 0
    assert (batch * seq) % (FFN1_SLABS * FFN_ROWS) == 0 and (batch * seq) % (FFN2_SLABS * FFN_ROWS) == 0
    depth = w_in.shape[0]
    x2d = x.reshape(batch * seq, d)
    for l in range(depth):
        x2d = _layer(x2d, mem, batch, seq, ffn1_norm[l], ffn1_w_gate[l], ffn1_w_up[l], ffn1_w_down[l],
                     mix_norm[l], w_in[l], gdn_conv_w[l], gdn_a_log[l], gdn_dt_bias[l], gdn_out_norm[l],
                     moba_q_norm[l], moba_k_norm[l], mem_norm[l], w_mem_kv[l], mem_q_norm[l], mem_k_norm[l],
                     w_out[l], ffn2_norm[l], ffn2_w_gate[l], ffn2_w_up[l], ffn2_w_down[l])
    return x2d.reshape(batch, seq, d)
```

```python
import struct

import jax
import jax.numpy as jnp
from jax import lax
from jax.experimental import pallas as pl
from jax.experimental.pallas import tpu as pltpu

F32 = jnp.float32
BF16 = jnp.bfloat16
HIGHEST = lax.Precision.HIGHEST

NORM_EPS = 1e-6
D_FF = 2816
GDN_HEADS = 4
GDN_HEAD_DIM = 128
GDN_WIDTH = GDN_HEADS * GDN_HEAD_DIM
GDN_CONV = 4
GDN_CHUNK = 64
GDN_GROUP = 2 * GDN_CHUNK
ATT_HEADS = 4
ATT_HEAD_DIM = 64
ATT_WIDTH = ATT_HEADS * ATT_HEAD_DIM
MOBA_BLOCK = 256
MOBA_TOPK = 3
MOBA_SEQS = 2
MOBA_RUN = 4
MASKED = -1e30
LOG2_E = 1.4426950408889634
BIAS_TERMS = 3

V7X_VMEM_LIMIT_BYTES = 56 * 1024 * 1024
V7X_LANES = 128
V7X_BF16_SUBLANES = 16
ATT_VROWS = ATT_HEAD_DIM + V7X_BF16_SUBLANES
FFN_ROWS = 256
FFN1_SLABS = 2
FFN2_SLABS = 4
GDN_ROWS = 256
GDN_SEQS = 4


def _dot(a, b, precision=None):
    return jnp.dot(a, b, preferred_element_type=F32, precision=precision)


def _dot_nt(a, b, precision=None):
    return lax.dot_general(a, b, (((1,), (1,)), ((), ())), preferred_element_type=F32,
                           precision=precision)


def _rms_rows(x, gain):
    return x * lax.rsqrt(jnp.mean(x * x, axis=-1, keepdims=True) + NORM_EPS) * gain


def _silu(x):
    half = 0.5 * x
    return half + half * jnp.tanh(half)


def _swiglu(x, gain, wg_ref, wu_ref, wd_ref):
    h = _rms_rows(x, gain).astype(BF16)
    g = _dot(h, wg_ref[...])
    u = _dot(h, wu_ref[...])
    return _dot((_silu(g) * u).astype(BF16), wd_ref[...])


def _resident():
    return pl.BlockSpec(memory_space=pltpu.VMEM)


def _bf16_terms(x):
    terms, rest = [], x
    for _ in range(BIAS_TERMS):
        t = rest.astype(BF16).astype(F32)
        terms.append(t)
        rest = rest - t
    return terms


def _bf16_terms_const(value):
    def f32(v):
        return struct.unpack("<f", struct.pack("<f", v))[0]

    def bf16_round(v):
        bits = struct.unpack("<I", struct.pack("<f", v))[0]
        bits = (bits + 0x7FFF + ((bits >> 16) & 1)) & 0xFFFF0000
        return struct.unpack("<f", struct.pack("<I", bits))[0]

    terms, rest = [], f32(value)
    for _ in range(BIAS_TERMS):
        t = bf16_round(rest)
        terms.append(t)
        rest = f32(rest - t)
    return terms


def _ffn1_inproj_kernel(x_ref, n1_ref, wg_ref, wu_ref, wd_ref, n2_ref, win_ref,
                        x1_ref, pg_ref, pab_ref, patt_ref):
    gdn_w, ab_w = pg_ref.shape[1], pab_ref.shape[1]
    for r0 in range(0, x_ref.shape[0], FFN_ROWS):
        rows = slice(r0, r0 + FFN_ROWS)
        x = x_ref[rows, :]
        x1 = x + 0.5 * _swiglu(x, n1_ref[...], wg_ref, wu_ref, wd_ref)
        x1_ref[rows, :] = x1
        h = _rms_rows(x1, n2_ref[...]).astype(BF16)
        pg_ref[rows, :] = _dot(h, win_ref[:, 0:gdn_w])
        pab_ref[rows, :] = _dot(h, win_ref[:, gdn_w:gdn_w + ab_w])
        patt_ref[rows, :] = _dot(h, win_ref[:, gdn_w + ab_w:])


def _ffn1_inproj(x2d, n1, wg, wu, wd, n2, win, gdn_w, ab_w):
    n, d = x2d.shape
    att_w = win.shape[1] - gdn_w - ab_w
    rows = FFN1_SLABS * FFN_ROWS
    row_spec = lambda w: pl.BlockSpec((rows, w), lambda i: (i, 0))
    return pl.pallas_call(
        _ffn1_inproj_kernel,
        grid=(n // rows,),
        in_specs=[row_spec(d)] + [_resident()] * 6,
        out_specs=[row_spec(d), row_spec(gdn_w), row_spec(ab_w), row_spec(att_w)],
        out_shape=[jax.ShapeDtypeStruct((n, d), F32),
                   jax.ShapeDtypeStruct((n, gdn_w), F32),
                   jax.ShapeDtypeStruct((n, ab_w), F32),
                   jax.ShapeDtypeStruct((n, att_w), F32)],
        compiler_params=pltpu.CompilerParams(dimension_semantics=("arbitrary",),
                                             vmem_limit_bytes=V7X_VMEM_LIMIT_BYTES),
        name="ffn1_inproj",
    )(x2d, n1, wg, wu, wd, n2, win)


def _outproj_ffn2_kernel(x1_ref, yg_ref, ya_ref, wo_ref, n_ref, wg_ref, wu_ref, wd_ref, o_ref):
    gdn_w = yg_ref.shape[1]
    for r0 in range(0, x1_ref.shape[0], FFN_ROWS):
        rows = slice(r0, r0 + FFN_ROWS)
        x2 = (x1_ref[rows, :] + _dot(yg_ref[rows, :], wo_ref[0:gdn_w, :])
              + _dot(ya_ref[rows, :], wo_ref[gdn_w:, :]))
        o_ref[rows, :] = x2 + 0.5 * _swiglu(x2, n_ref[...], wg_ref, wu_ref, wd_ref)


def _outproj_ffn2(x1, yg, ya, wo, nrm, wg, wu, wd):
    n, d = x1.shape
    rows = FFN2_SLABS * FFN_ROWS
    row_spec = lambda w: pl.BlockSpec((rows, w), lambda i: (i, 0))
    return pl.pallas_call(
        _outproj_ffn2_kernel,
        grid=(n // rows,),
        in_specs=[row_spec(d), row_spec(yg.shape[1]), row_spec(ya.shape[1])] + [_resident()] * 5,
        out_specs=row_spec(d),
        out_shape=jax.ShapeDtypeStruct((n, d), F32),
        compiler_params=pltpu.CompilerParams(dimension_semantics=("arbitrary",),
                                             vmem_limit_bytes=V7X_VMEM_LIMIT_BYTES),
        name="outproj_ffn2",
    )(x1, yg, ya, wo, nrm, wg, wu, wd)


def _head_block_ones(width, head_dim):
    r = lax.broadcasted_iota(jnp.int32, (width, width), 0) // head_dim
    c = lax.broadcasted_iota(jnp.int32, (width, width), 1) // head_dim
    return jnp.where(r == c, 1.0, 0.0).astype(BF16)


def _head_rms(x, gain, block_ones, head_dim):
    sq = x * x
    hi = sq.astype(BF16)
    lo = (sq - hi.astype(F32)).astype(BF16)
    ss = _dot(hi, block_ones) + _dot(lo, block_ones)
    return x * lax.rsqrt(ss * (1.0 / head_dim) + NORM_EPS) * gain


def _values_with_ones(v_t):
    Dh = ATT_HEAD_DIM
    row = lax.broadcasted_iota(jnp.int32, (V7X_BF16_SUBLANES, v_t.shape[1]), 0)
    ones_block = jnp.where(row == 0, 1.0, 0.0)
    parts = []
    for h in range(ATT_HEADS):
        parts += [v_t[h * Dh:(h + 1) * Dh], ones_block]
    return jnp.concatenate(parts, axis=0).astype(BF16)


def _mem_kv_kernel(mem_ref, nrm_ref, wkv_ref, kg_ref, k_ref, vt_ref):
    h = _rms_rows(mem_ref[...], nrm_ref[...]).astype(BF16)
    kv = _dot(h, wkv_ref[...])
    k = _head_rms(kv[:, :ATT_WIDTH], kg_ref[...], _head_block_ones(ATT_WIDTH, ATT_HEAD_DIM), ATT_HEAD_DIM)
    k_ref[...] = k.astype(BF16)
    vt_ref[...] = _values_with_ones(kv[:, ATT_WIDTH:].T)


def _mem_kv(mem, nrm, wkv, kgain):
    b, m, d = mem.shape
    return pl.pallas_call(
        _mem_kv_kernel,
        grid=(b,),
        in_specs=[pl.BlockSpec((None, m, d), lambda i: (i, 0, 0)), _resident(), _resident(), _resident()],
        out_specs=[pl.BlockSpec((None, m, ATT_WIDTH), lambda i: (i, 0, 0)),
                   pl.BlockSpec((None, ATT_HEADS * ATT_VROWS, m), lambda i: (i, 0, 0))],
        out_shape=[jax.ShapeDtypeStruct((b, m, ATT_WIDTH), BF16),
                   jax.ShapeDtypeStruct((b, ATT_HEADS * ATT_VROWS, m), BF16)],
        compiler_params=pltpu.CompilerParams(dimension_semantics=("arbitrary",)),
        name="mem_kv",
    )(mem, nrm, wkv, kgain)


def _softplus(x):
    return jnp.maximum(x, 0.0) + jnp.log(1.0 + jnp.exp(-jnp.abs(x)))


def _unit_lower_inverses(a_list, eye):
    x16 = [(-a).astype(BF16) for a in a_list]
    t = [eye - a for a in a_list]
    for _ in range(5):
        x16 = [_dot(xb, xb).astype(BF16) for xb in x16]
        t = [ti + _dot(ti.astype(BF16), xb) for ti, xb in zip(t, x16)]
    t16 = [ti.astype(BF16) for ti in t]
    a_hi = [a.astype(BF16) for a in a_list]
    a_lo = [(a - hi.astype(F32)).astype(BF16) for a, hi in zip(a_list, a_hi)]
    resid = [(eye - tb.astype(F32) - _dot(hi, tb) - _dot(lo, tb)).astype(BF16)
             for tb, hi, lo in zip(t16, a_hi, a_lo)]
    return [tb.astype(F32) + _dot(tb, r) for tb, r in zip(t16, resid)]


def _gdn_kernel(qkv_ref, z_ref, ab_ref, cw_ref, alog_ref, dtb_ref, og_ref, y_ref, xbuf_ref, s_ref):
    n_seq, rows = z_ref.shape[0], z_ref.shape[1]
    H, Dh, C, W = GDN_HEADS, GDN_HEAD_DIM, GDN_CHUNK, GDN_WIDTH
    G = GDN_GROUP
    n_chunks = rows // C
    n_groups = rows // G

    @pl.when(pl.program_id(1) == 0)
    def _():
        xbuf_ref[:, 0:8, :] = jnp.zeros((n_seq, 8, 3 * W), F32)
        s_ref[...] = jnp.zeros_like(s_ref)

    ri = lax.broadcasted_iota(jnp.int32, (G, G), 0)
    ci = lax.broadcasted_iota(jnp.int32, (G, G), 1)
    same_chunk = (ri // C) == (ci // C)
    lower = same_chunk & (ri >= ci)
    strict = same_chunk & (ri > ci)
    eye = (ri == ci).astype(F32)
    lower_f = lower.astype(F32)

    chains = [(b, h) for b in range(n_seq) for h in range(H)]
    q_c, k_c, v_c, gcol_c, beta_c, grow_c = [], [], [], [], [], []
    for b in range(n_seq):
        xbuf_ref[b, 8:, :] = qkv_ref[b]
        conv = cw_ref[0:1, :] * xbuf_ref[b, 5:5 + rows, :]
        for j in range(1, GDN_CONV):
            conv = conv + cw_ref[j:j + 1, :] * xbuf_ref[b, 5 + j:5 + j + rows, :]
        xbuf_ref[b, 0:8, :] = xbuf_ref[b, rows:rows + 8, :]
        act = _silu(conv)

        ab = ab_ref[b]
        log_decay = -jnp.exp(alog_ref[...]) * _softplus(ab + dtb_ref[...])
        beta_all = jax.nn.sigmoid(ab)
        gc_all = jnp.concatenate(
            [_dot(lower_f, log_decay[g * G:(g + 1) * G], HIGHEST) for g in range(n_groups)], axis=0)
        gc_rows = gc_all.T
        for h in range(H):
            qh = act[:, h * Dh:(h + 1) * Dh]
            kh = act[:, W + h * Dh:W + (h + 1) * Dh]
            v_c.append(act[:, 2 * W + h * Dh:2 * W + (h + 1) * Dh])
            q_c.append(qh * (lax.rsqrt(jnp.sum(qh * qh, axis=-1, keepdims=True) + NORM_EPS) * Dh ** -0.5))
            k_c.append(kh * lax.rsqrt(jnp.sum(kh * kh, axis=-1, keepdims=True) + NORM_EPS))
            gcol_c.append(jnp.broadcast_to(gc_all[:, h:h + 1], (rows, Dh)))
            beta_c.append(jnp.broadcast_to(beta_all[:, H + h:H + h + 1], (rows, Dh)))
            grow_c.append(gc_rows[h:h + 1, :])

    units = [(c, g) for c in range(len(chains)) for g in range(n_groups)]

    def unit_rows(x, g):
        return x[g * G:(g + 1) * G]

    k_u = [unit_rows(k_c[c], g) for c, g in units]
    gcol_u = [unit_rows(gcol_c[c], g) for c, g in units]
    beta_u = [unit_rows(beta_c[c], g) for c, g in units]
    kb_u = [k * beta for k, beta in zip(k_u, beta_u)]
    k16_u = [k.astype(BF16) for k in k_u]
    q_u = [unit_rows(q_c[c], g) for c, g in units]
    kk_u = [_dot_nt(kb.astype(BF16), k16) for kb, k16 in zip(kb_u, k16_u)]
    qk_u = [_dot_nt(q.astype(BF16), k16) for q, k16 in zip(q_u, k16_u)]
    decay_u = []
    for gcol, (c, g) in zip(gcol_u, units):
        diff = gcol - grow_c[c][:, g * G:(g + 1) * G]
        decay_u.append(jnp.where(lower, jnp.exp(jnp.where(lower, diff, 0.0)), 0.0))
    a_u = [jnp.where(strict, kk * decay, 0.0) for kk, decay in zip(kk_u, decay_u)]
    qk16_u = [(qk * decay).astype(BF16) for qk, decay in zip(qk_u, decay_u)]
    tinv_u = _unit_lower_inverses(a_u, eye)
    egc_u = [jnp.exp(gcol) for gcol in gcol_u]
    rhs_u = [jnp.concatenate([unit_rows(v_c[c], g) * beta, kb * egc], axis=1).astype(BF16)
             for beta, kb, egc, (c, g) in zip(beta_u, kb_u, egc_u, units)]
    uw_u = [_dot(tinv.astype(BF16), rhs) for tinv, rhs in zip(tinv_u, rhs_u)]
    w16_u = [uw[:, Dh:].astype(BF16) for uw in uw_u]
    qe16_u = [(q * egc).astype(BF16) for q, egc in zip(q_u, egc_u)]

    def chunk_operands(c, n):
        u = c * n_groups + (n * C) // G
        r = slice((n * C) % G, (n * C) % G + C)
        g_last = gcol_u[u][r.stop - 1:r.stop, :]
        return (uw_u[u][r, :Dh],
                jnp.concatenate([w16_u[u][r], qe16_u[u][r]], axis=0),
                qk16_u[u][r, r],
                (k_u[u][r] * jnp.exp(g_last - gcol_u[u][r])).T.astype(BF16),
                jnp.exp(g_last))

    n_chains = len(chains)
    ops = [[chunk_operands(c, n) for n in range(n_chunks)] for c in range(n_chains)]

    state = [s_ref[c] for c in range(n_chains)]
    o_l = [[] for _ in range(n_chains)]
    for n in range(n_chunks):
        ws = [_dot(ops[c][n][1], state[c].astype(BF16)) for c in range(n_chains)]
        v_new = [(ops[c][n][0] - ws[c][:C]).astype(BF16) for c in range(n_chains)]
        state = [state[c] * ops[c][n][4] + _dot(ops[c][n][3], v_new[c]) for c in range(n_chains)]
        for c in range(n_chains):
            o_l[c].append(ws[c][C:] + _dot(ops[c][n][2], v_new[c]))
    for c, (b, h) in enumerate(chains):
        s_ref[c] = state[c]
        o = jnp.concatenate(o_l[c], axis=0)
        z = z_ref[b, :, h * Dh:(h + 1) * Dh]
        y_ref[b, :, h * Dh:(h + 1) * Dh] = (_rms_rows(o, og_ref[...]) * _silu(z)).astype(BF16)


def _gdn(pg, pab, conv_w, alog_vec, dtb_vec, out_gain, batch, seq):
    rows = GDN_ROWS
    n_seq = GDN_SEQS
    W = GDN_WIDTH
    pg3 = pg.reshape(batch, seq, pg.shape[1])
    pab3 = pab.reshape(batch, seq, pab.shape[1])
    y = pl.pallas_call(
        _gdn_kernel,
        grid=(batch // n_seq, seq // rows),
        in_specs=[pl.BlockSpec((n_seq, rows, 3 * W), lambda b, t: (b, t, 0)),
                  pl.BlockSpec((n_seq, rows, W), lambda b, t: (b, t, 3)),
                  pl.BlockSpec((n_seq, rows, pab.shape[1]), lambda b, t: (b, t, 0)),
                  _resident(), _resident(), _resident(), _resident()],
        out_specs=pl.BlockSpec((n_seq, rows, W), lambda b, t: (b, t, 0)),
        out_shape=jax.ShapeDtypeStruct((batch, seq, W), BF16),
        scratch_shapes=[pltpu.VMEM((n_seq, rows + 8, 3 * W), F32),
                        pltpu.VMEM((n_seq * GDN_HEADS, GDN_HEAD_DIM, GDN_HEAD_DIM), F32)],
        compiler_params=pltpu.CompilerParams(dimension_semantics=("arbitrary", "arbitrary"),
                                             vmem_limit_bytes=V7X_VMEM_LIMIT_BYTES),
        name="gdn",
    )(pg3, pg3, pab3, conv_w, alog_vec, dtb_vec, out_gain)
    return y.reshape(batch * seq, W)


def _alibi_slope(h):
    return 2.0 ** (-8.0 * (h + 1) / ATT_HEADS)


def _moba_kernel(p_ref, qg_ref, kg_ref, cg_ref, km_ref, vmt_ref, y_ref,
                 k_ref, vt_ref, kmean_ref, sel_ref, m_ref, acc_ref, s_ref, kaug_ref, rhs_ref):
    H, Dh, BLK, Wd = ATT_HEADS, ATT_HEAD_DIM, MOBA_BLOCK, ATT_WIDTH
    PAIR = 2 * Dh
    AUG = V7X_BF16_SUBLANES
    n_seq = p_ref.shape[0]
    nb = kmean_ref.shape[1]
    i = pl.program_id(1)
    pair_row_head = lax.broadcasted_iota(jnp.int32, (PAIR, BLK), 0) // Dh
    chains = [(b, h) for b in range(n_seq) for h in range(H)]

    @pl.when((pl.program_id(0) == 0) & (i == 0))
    def _():
        lane = lax.broadcasted_iota(jnp.int32, (BLK, V7X_LANES), 1)
        key_pos = lax.broadcasted_iota(jnp.int32, (BLK, V7X_LANES), 0).astype(F32)
        kaug_ref[...] = jnp.where(lane < BIAS_TERMS, key_pos,
                                  jnp.where(lane < 2 * BIAS_TERMS, 1.0, 0.0)).astype(BF16)
        rhs_ref[...] = jnp.zeros_like(rhs_ref)

    @pl.when(i == 0)
    def _():
        kmean_ref[...] = jnp.zeros_like(kmean_ref)

    def pair_lanes(h):
        return slice((h // 2) * PAIR, (h // 2 + 1) * PAIR)

    def value_rows(h):
        return slice(h * ATT_VROWS, (h + 1) * ATT_VROWS)

    def normalised(acc):
        return acc[0:Dh] / acc[Dh:Dh + 1]

    def head_queries(q_t, h):
        qp = q_t[pair_lanes(h), :]
        return jnp.where(pair_row_head == h % 2, qp, jnp.zeros_like(qp))

    block_ones = _head_block_ones(Wd, Dh)
    scale = Dh ** -0.5 * LOG2_E
    blk_row = lax.broadcasted_iota(jnp.int32, (nb, Wd), 0)
    lane_head = lax.broadcasted_iota(jnp.int32, (nb, Wd), 1) // Dh
    jdx_f = lax.broadcasted_iota(jnp.int32, (nb, BLK), 0).astype(F32)
    past = lax.broadcasted_iota(jnp.int32, (nb, BLK), 0) < i
    cq_t = []
    for b in range(n_seq):
        qn = _head_rms(p_ref[b, :, 0:Wd], qg_ref[...], block_ones, Dh)
        kn = _head_rms(p_ref[b, :, Wd:2 * Wd], kg_ref[...], block_ones, Dh)
        cqn = _head_rms(p_ref[b, :, 3 * Wd:4 * Wd], cg_ref[...], block_ones, Dh)
        q_t = (qn * scale).T.astype(BF16)
        cq_t.append((cqn * scale).T.astype(BF16))
        k_ref[b, pl.ds(pl.multiple_of(i * BLK, BLK), BLK), :] = kn.astype(BF16)
        vt_ref[b, i] = _values_with_ones(p_ref[b, :, 2 * Wd:3 * Wd].T)
        for h in range(H):
            rhs_ref[b * H + h, 0:PAIR, :] = head_queries(q_t, h)

        kmean = kmean_ref[b]
        kmean_heads = jnp.concatenate([jnp.where(lane_head == h, kmean, 0.0) for h in range(H)], axis=0)
        gate_all = _dot_nt(kmean_heads, qn, HIGHEST)
        for h in range(H):
            gate = jnp.where(past, gate_all[h * nb:(h + 1) * nb], -jnp.inf)
            chosen = jnp.zeros(gate.shape, jnp.bool_)
            for _ in range(MOBA_TOPK):
                top = jnp.max(gate, axis=0, keepdims=True)
                first = jnp.min(jnp.where(gate == top, jdx_f, float(nb)), axis=0, keepdims=True)
                pick = jdx_f == first
                chosen = chosen | pick
                gate = jnp.where(pick, -jnp.inf, gate)
            sel_ref[b * H + h] = jnp.where(chosen & past, 0.0, MASKED)
        kmean_ref[b] = jnp.where(blk_row == i, jnp.mean(kn, axis=0, keepdims=True), kmean)

    aug_row = lax.broadcasted_iota(jnp.int32, (AUG, BLK), 0)
    query_pos = lax.broadcasted_iota(jnp.int32, (1, BLK), 1).astype(F32)

    def scores(blk, own):
        r0 = pl.multiple_of(blk * BLK, BLK)
        distance = (i - blk).astype(F32) * BLK
        keys_aug = kaug_ref[...]
        out = []
        for c, (b, h) in enumerate(chains):
            slope = _alibi_slope(h) * LOG2_E
            row_bias = -slope * (query_pos + distance)
            if not own:
                row_bias = row_bias + sel_ref[c, pl.ds(blk, 1), :]
            rows = _bf16_terms_const(slope) + _bf16_terms(row_bias)
            aug = jnp.zeros((AUG, BLK), F32)
            for n, val in enumerate(rows):
                aug = jnp.where(aug_row == n, val, aug)
            rhs_ref[c, PAIR:PAIR + AUG, :] = aug.astype(BF16)
            keys = jnp.concatenate([k_ref[b, pl.ds(r0, BLK), pair_lanes(h)], keys_aug], axis=1)
            out.append(_dot(keys, rhs_ref[c]))
        return out

    own_scores = scores(i, True)
    mem_scores = [_dot(km_ref[b, :, pair_lanes(h)], head_queries(cq_t[b], h)) for b, h in chains]
    first_past = scores(0, False)
    causal = (lax.broadcasted_iota(jnp.int32, (BLK, BLK), 0) <= lax.broadcasted_iota(jnp.int32, (BLK, BLK), 1))
    own_p, mem_p = [], []
    for c in range(len(chains)):
        s_t = jnp.where(causal, own_scores[c], MASKED)
        m = jnp.max(s_t, axis=0, keepdims=True)
        p = jnp.exp2(s_t - m)
        m_ref[c] = m
        own_p.append(p.astype(BF16))
    for c in range(len(chains)):
        p = jnp.exp2(mem_scores[c] - jnp.max(mem_scores[c], axis=0, keepdims=True))
        mem_p.append(p.astype(BF16))
    for c, (b, h) in enumerate(chains):
        acc_ref[b * 2 * H + h] = _dot(vt_ref[b, i, value_rows(h), :], own_p[c])
        s_ref[0, c] = first_past[c]
    for c, (b, h) in enumerate(chains):
        acc_ref[b * 2 * H + H + h] = _dot(vmt_ref[b, value_rows(h), :], mem_p[c])

    def block_step(j, slot, nxt):
        next_scores = None if nxt is None else scores(nxt, False)
        for c, (b, h) in enumerate(chains):
            a = b * 2 * H + h
            m_old = m_ref[c]
            m_new = jnp.maximum(m_old, jnp.max(s_ref[slot, c], axis=0, keepdims=True))
            m_ref[c] = m_new
            p = jnp.exp2(s_ref[slot, c] - m_new)
            alpha = jnp.exp2(m_old - m_new)
            acc_ref[a] = alpha * acc_ref[a] + _dot(vt_ref[b, j, value_rows(h), :], p.astype(BF16))
        if next_scores is not None:
            for c in range(len(chains)):
                s_ref[1 - slot, c] = next_scores[c]

    last_past = jnp.maximum(i - 1, 0)

    def block_run(j0, count):
        for n in range(count):
            block_step(j0 + n, n % 2, jnp.minimum(j0 + n + 1, last_past))

    def full_run(jj, carry):
        block_run(MOBA_RUN * jj, MOBA_RUN)
        return carry

    lax.fori_loop(0, i // MOBA_RUN, full_run, 0)
    run = MOBA_RUN // 2
    while run >= 2:
        @pl.when(i % (2 * run) >= run)
        def _(run=run):
            block_run((2 * run) * (i // (2 * run)), run)
        run //= 2

    @pl.when(i % 2 == 1)
    def _():
        block_step(i - 1, 0, None)

    for b in range(n_seq):
        out_t = jnp.concatenate([normalised(acc_ref[b * 2 * H + a]) for a in range(2 * H)], axis=0)
        y_ref[b] = out_t.T.astype(BF16)


def _moba(patt, qg, kg, cg, kmem, vmem_t, batch, seq):
    nb = seq // MOBA_BLOCK
    Wd, H, BLK = ATT_WIDTH, ATT_HEADS, MOBA_BLOCK
    n_seq = MOBA_SEQS
    n_mem = kmem.shape[1]
    y = pl.pallas_call(
        _moba_kernel,
        grid=(batch // n_seq, nb),
        in_specs=[pl.BlockSpec((n_seq, BLK, 4 * Wd), lambda b, i: (b, i, 0)),
                  _resident(), _resident(), _resident(),
                  pl.BlockSpec((n_seq, n_mem, Wd), lambda b, i: (b, 0, 0)),
                  pl.BlockSpec((n_seq, H * ATT_VROWS, n_mem), lambda b, i: (b, 0, 0))],
        out_specs=pl.BlockSpec((n_seq, BLK, 2 * Wd), lambda b, i: (b, i, 0)),
        out_shape=jax.ShapeDtypeStruct((batch, seq, 2 * Wd), BF16),
        scratch_shapes=[pltpu.VMEM((n_seq, seq, Wd), BF16),
                        pltpu.VMEM((n_seq, nb, H * ATT_VROWS, BLK), BF16),
                        pltpu.VMEM((n_seq, nb, Wd), F32),
                        pltpu.VMEM((n_seq * H, nb, BLK), F32),
                        pltpu.VMEM((n_seq * H, 1, BLK), F32),
                        pltpu.VMEM((n_seq * 2 * H, ATT_VROWS, BLK), F32),
                        pltpu.VMEM((2, n_seq * H, BLK, BLK), F32),
                        pltpu.VMEM((BLK, V7X_LANES), BF16),
                        pltpu.VMEM((n_seq * H, 2 * ATT_HEAD_DIM + V7X_LANES, BLK), BF16)],
        compiler_params=pltpu.CompilerParams(dimension_semantics=("arbitrary", "arbitrary"),
                                             vmem_limit_bytes=V7X_VMEM_LIMIT_BYTES),
        name="moba",
    )(patt.reshape(batch, seq, 4 * Wd), qg, kg, cg, kmem, vmem_t)
    return y.reshape(batch * seq, 2 * Wd)


def _lane_vector(values, width):
    return jnp.zeros((1, width), F32).at[0, :values.shape[0]].set(values.astype(F32))


def _layer(x2d, mem, batch, seq, ffn1_norm, ffn1_w_gate, ffn1_w_up, ffn1_w_down, mix_norm, w_in,
           gdn_conv_w, gdn_a_log, gdn_dt_bias, gdn_out_norm, moba_q_norm, moba_k_norm,
           mem_norm, w_mem_kv, mem_q_norm, mem_k_norm, w_out, ffn2_norm, ffn2_w_gate, ffn2_w_up, ffn2_w_down):
    W, H = GDN_WIDTH, GDN_HEADS
    row = lambda v: v.reshape(1, -1).astype(F32)
    tile_heads = lambda v: jnp.tile(v.astype(F32), ATT_HEADS).reshape(1, -1)
    ab0 = 4 * W
    att0 = ab0 + 2 * H
    ab_width = V7X_LANES
    w_in_sections = jnp.concatenate(
        [w_in[:, :ab0], w_in[:, ab0:att0], jnp.zeros((w_in.shape[0], ab_width - 2 * H), w_in.dtype),
         w_in[:, att0:]], axis=1).astype(BF16)

    x1, pg, pab, patt = _ffn1_inproj(
        x2d, row(ffn1_norm), ffn1_w_gate.astype(BF16), ffn1_w_up.astype(BF16), ffn1_w_down.astype(BF16),
        row(mix_norm), w_in_sections, ab0, ab_width)

    y_gdn = _gdn(pg, pab, gdn_conv_w.astype(F32), _lane_vector(gdn_a_log, ab_width),
                 _lane_vector(gdn_dt_bias, ab_width), row(gdn_out_norm), batch, seq)

    kmem, vmem_t = _mem_kv(mem, row(mem_norm), w_mem_kv.astype(BF16), tile_heads(mem_k_norm))
    y_att = _moba(patt, tile_heads(moba_q_norm), tile_heads(moba_k_norm), tile_heads(mem_q_norm),
                  kmem, vmem_t, batch, seq)

    return _outproj_ffn2(x1, y_gdn, y_att, w_out.astype(BF16), row(ffn2_norm),
                         ffn2_w_gate.astype(BF16), ffn2_w_up.astype(BF16), ffn2_w_down.astype(BF16))


def kernel(x, mem, ffn1_norm, ffn1_w_gate, ffn1_w_up, ffn1_w_down, mix_norm, w_in, gdn_conv_w, gdn_a_log,
           gdn_dt_bias, gdn_out_norm, moba_q_norm, moba_k_norm, mem_norm, w_mem_kv, mem_q_norm, mem_k_norm,
           w_out, ffn2_norm, ffn2_w_gate, ffn2_w_up, ffn2_w_down):
    batch, seq, d = x.shape
    assert seq % MOBA_BLOCK == 0 and seq % GDN_ROWS == 0 and batch % GDN_SEQS == 0 and batch % MOBA_SEQS == 0
    assert (batch * seq) % (FFN1_SLABS * FFN_ROWS) == 0 and (batch * seq) % (FFN2_SLABS * FFN_ROWS) == 0
    depth = w_in.shape[0]
    x2d = x.reshape(batch * seq, d)
    for l in range(depth):
        x2d = _layer(x2d, mem, batch, seq, ffn1_norm[l], ffn1_w_gate[l], ffn1_w_up[l], ffn1_w_down[l],
                     mix_norm[l], w_in[l], gdn_conv_w[l], gdn_a_log[l], gdn_dt_bias[l], gdn_out_norm[l],
                     moba_q_norm[l], moba_k_norm[l], mem_norm[l], w_mem_kv[l], mem_q_norm[l], mem_k_norm[l],
                     w_out[l], ffn2_norm[l], ffn2_w_gate[l], ffn2_w_up[l], ffn2_w_down[l])
    return x2d.reshape(batch, seq, d)
```

```python
import struct

import jax
import jax.numpy as jnp
from jax import lax
from jax.experimental import pallas as pl
from jax.experimental.pallas import tpu as pltpu

F32 = jnp.float32
BF16 = jnp.bfloat16
HIGHEST = lax.Precision.HIGHEST

NORM_EPS = 1e-6
GDN_HEADS = 4
GDN_HEAD_DIM = 128
GDN_WIDTH = GDN_HEADS * GDN_HEAD_DIM
GDN_CONV = 4
GDN_CHUNK = 64
GDN_GROUP = 2 * GDN_CHUNK
ATT_HEADS = 4
ATT_HEAD_DIM = 64
ATT_WIDTH = ATT_HEADS * ATT_HEAD_DIM
MOBA_BLOCK = 256
MOBA_TOPK = 3
MOBA_SEQS = 2
MOBA_RUN = 4
MASKED = -1e30
LOG2_E = 1.4426950408889634
BIAS_TERMS = 3

V7X_VMEM_LIMIT_BYTES = 56 * 1024 * 1024
V7X_LANES = 128
V7X_BF16_SUBLANES = 16
ATT_VROWS = ATT_HEAD_DIM + V7X_BF16_SUBLANES
FFN_ROWS = 256
FFN1_SLABS = 2
FFN2_SLABS = 4
GDN_ROWS = 256
GDN_SEQS = 4


def _dot(a, b, precision=None):
    return jnp.dot(a, b, preferred_element_type=F32, precision=precision)


def _dot_nt(a, b, precision=None):
    return lax.dot_general(a, b, (((1,), (1,)), ((), ())), preferred_element_type=F32,
                           precision=precision)


def _rms_rows(x, gain):
    return x * lax.rsqrt(jnp.mean(x * x, axis=-1, keepdims=True) + NORM_EPS) * gain


def _silu(x):
    half = 0.5 * x
    return half + half * jnp.tanh(half)


def _swiglu(x, gain, wg_ref, wu_ref, wd_ref):
    h = _rms_rows(x, gain).astype(BF16)
    g = _dot(h, wg_ref[...])
    u = _dot(h, wu_ref[...])
    return _dot((_silu(g) * u).astype(BF16), wd_ref[...])


def _resident():
    return pl.BlockSpec(memory_space=pltpu.VMEM)


def _bf16_terms(x):
    terms, rest = [], x
    for _ in range(BIAS_TERMS):
        t = rest.astype(BF16).astype(F32)
        terms.append(t)
        rest = rest - t
    return terms


def _bf16_terms_const(value):
    def f32(v):
        return struct.unpack("<f", struct.pack("<f", v))[0]

    def bf16_round(v):
        bits = struct.unpack("<I", struct.pack("<f", v))[0]
        bits = (bits + 0x7FFF + ((bits >> 16) & 1)) & 0xFFFF0000
        return struct.unpack("<f", struct.pack("<I", bits))[0]

    terms, rest = [], f32(value)
    for _ in range(BIAS_TERMS):
        t = bf16_round(rest)
        terms.append(t)
        rest = f32(rest - t)
    return terms


def _ffn1_inproj_kernel(x_ref, n1_ref, wg_ref, wu_ref, wd_ref, n2_ref, win_ref, watt_ref,
                        x1_ref, pg_ref, pab_ref, patt_ref):
    gdn_w, ab_w = pg_ref.shape[1], pab_ref.shape[1]
    for r0 in range(0, x_ref.shape[0], FFN_ROWS):
        rows = slice(r0, r0 + FFN_ROWS)
        x = x_ref[rows, :]
        x1 = x + 0.5 * _swiglu(x, n1_ref[...], wg_ref, wu_ref, wd_ref)
        x1_ref[rows, :] = x1
        h = _rms_rows(x1, n2_ref[...]).astype(BF16)
        pg_ref[rows, :] = _dot(h, win_ref[:, 0:gdn_w])
        pab_ref[rows, :] = _dot(h, win_ref[:, gdn_w:gdn_w + ab_w])
        patt_ref[rows, :] = _dot(h, watt_ref[...])


def _ffn1_inproj(x2d, n1, wg, wu, wd, n2, win, watt, gdn_w):
    n, d = x2d.shape
    ab_w = win.shape[1] - gdn_w
    att_w = watt.shape[1]
    rows = FFN1_SLABS * FFN_ROWS
    row_spec = lambda w: pl.BlockSpec((rows, w), lambda i: (i, 0))
    return pl.pallas_call(
        _ffn1_inproj_kernel,
        grid=(n // rows,),
        in_specs=[row_spec(d)] + [_resident()] * 7,
        out_specs=[row_spec(d), row_spec(gdn_w), row_spec(ab_w), row_spec(att_w)],
        out_shape=[jax.ShapeDtypeStruct((n, d), F32),
                   jax.ShapeDtypeStruct((n, gdn_w), F32),
                   jax.ShapeDtypeStruct((n, ab_w), F32),
                   jax.ShapeDtypeStruct((n, att_w), F32)],
        compiler_params=pltpu.CompilerParams(dimension_semantics=("arbitrary",),
                                             vmem_limit_bytes=V7X_VMEM_LIMIT_BYTES),
        name="ffn1_inproj",
    )(x2d, n1, wg, wu, wd, n2, win, watt)


def _outproj_ffn2_kernel(x1_ref, yg_ref, ya_ref, wo_ref, n_ref, wg_ref, wu_ref, wd_ref, o_ref):
    gdn_w = yg_ref.shape[1]
    for r0 in range(0, x1_ref.shape[0], FFN_ROWS):
        rows = slice(r0, r0 + FFN_ROWS)
        x2 = (x1_ref[rows, :] + _dot(yg_ref[rows, :], wo_ref[0:gdn_w, :])
              + _dot(ya_ref[rows, :], wo_ref[gdn_w:, :]))
        o_ref[rows, :] = x2 + 0.5 * _swiglu(x2, n_ref[...], wg_ref, wu_ref, wd_ref)


def _outproj_ffn2(x1, yg, ya, wo, nrm, wg, wu, wd):
    n, d = x1.shape
    rows = FFN2_SLABS * FFN_ROWS
    row_spec = lambda w: pl.BlockSpec((rows, w), lambda i: (i, 0))
    return pl.pallas_call(
        _outproj_ffn2_kernel,
        grid=(n // rows,),
        in_specs=[row_spec(d), row_spec(yg.shape[1]), row_spec(ya.shape[1])] + [_resident()] * 5,
        out_specs=row_spec(d),
        out_shape=jax.ShapeDtypeStruct((n, d), F32),
        compiler_params=pltpu.CompilerParams(dimension_semantics=("arbitrary",),
                                             vmem_limit_bytes=V7X_VMEM_LIMIT_BYTES),
        name="outproj_ffn2",
    )(x1, yg, ya, wo, nrm, wg, wu, wd)


def _head_block_ones(width, head_dim):
    r = lax.broadcasted_iota(jnp.int32, (width, width), 0) // head_dim
    c = lax.broadcasted_iota(jnp.int32, (width, width), 1) // head_dim
    return jnp.where(r == c, 1.0, 0.0).astype(BF16)


def _head_rms(x, gain, block_ones, head_dim):
    sq = x * x
    hi = sq.astype(BF16)
    lo = (sq - hi.astype(F32)).astype(BF16)
    ss = _dot(hi, block_ones) + _dot(lo, block_ones)
    return x * lax.rsqrt(ss * (1.0 / head_dim) + NORM_EPS) * gain


def _values_with_ones(v_t):
    Dh = ATT_HEAD_DIM
    row = lax.broadcasted_iota(jnp.int32, (V7X_BF16_SUBLANES, v_t.shape[1]), 0)
    ones_block = jnp.where(row == 0, 1.0, 0.0)
    parts = []
    for h in range(ATT_HEADS):
        parts += [v_t[h * Dh:(h + 1) * Dh], ones_block]
    return jnp.concatenate(parts, axis=0).astype(BF16)


def _mem_kv_kernel(mem_ref, nrm_ref, wkv_ref, kg_ref, k_ref, vt_ref):
    h = _rms_rows(mem_ref[...], nrm_ref[...]).astype(BF16)
    kv = _dot(h, wkv_ref[...])
    k = _head_rms(kv[:, :ATT_WIDTH], kg_ref[...], _head_block_ones(ATT_WIDTH, ATT_HEAD_DIM), ATT_HEAD_DIM)
    k_ref[...] = k.astype(BF16)
    vt_ref[...] = _values_with_ones(kv[:, ATT_WIDTH:].T)


def _mem_kv(mem, nrm, wkv, kgain):
    b, m, d = mem.shape
    return pl.pallas_call(
        _mem_kv_kernel,
        grid=(b,),
        in_specs=[pl.BlockSpec((None, m, d), lambda i: (i, 0, 0)), _resident(), _resident(), _resident()],
        out_specs=[pl.BlockSpec((None, m, ATT_WIDTH), lambda i: (i, 0, 0)),
                   pl.BlockSpec((None, ATT_HEADS * ATT_VROWS, m), lambda i: (i, 0, 0))],
        out_shape=[jax.ShapeDtypeStruct((b, m, ATT_WIDTH), BF16),
                   jax.ShapeDtypeStruct((b, ATT_HEADS * ATT_VROWS, m), BF16)],
        compiler_params=pltpu.CompilerParams(dimension_semantics=("arbitrary",)),
        name="mem_kv",
    )(mem, nrm, wkv, kgain)


def _softplus(x):
    return jnp.maximum(x, 0.0) + jnp.log(1.0 + jnp.exp(-jnp.abs(x)))


def _unit_lower_inverses(a_list, eye):
    x16 = [(-a).astype(BF16) for a in a_list]
    t = [eye - a for a in a_list]
    for _ in range(5):
        x16 = [_dot(xb, xb).astype(BF16) for xb in x16]
        t = [ti + _dot(ti.astype(BF16), xb) for ti, xb in zip(t, x16)]
    t16 = [ti.astype(BF16) for ti in t]
    a_hi = [a.astype(BF16) for a in a_list]
    a_lo = [(a - hi.astype(F32)).astype(BF16) for a, hi in zip(a_list, a_hi)]
    resid = [(eye - tb.astype(F32) - _dot(hi, tb) - _dot(lo, tb)).astype(BF16)
             for tb, hi, lo in zip(t16, a_hi, a_lo)]
    return [tb.astype(F32) + _dot(tb, r) for tb, r in zip(t16, resid)]


def _gdn_kernel(qkv_ref, z_ref, ab_ref, cw_ref, alog_ref, dtb_ref, og_ref, y_ref, xbuf_ref, s_ref):
    n_seq, rows = z_ref.shape[0], z_ref.shape[1]
    H, Dh, C, W = GDN_HEADS, GDN_HEAD_DIM, GDN_CHUNK, GDN_WIDTH
    G = GDN_GROUP
    n_chunks = rows // C
    n_groups = rows // G

    @pl.when(pl.program_id(1) == 0)
    def _():
        xbuf_ref[:, 0:8, :] = jnp.zeros((n_seq, 8, 3 * W), F32)
        s_ref[...] = jnp.zeros_like(s_ref)

    ri = lax.broadcasted_iota(jnp.int32, (G, G), 0)
    ci = lax.broadcasted_iota(jnp.int32, (G, G), 1)
    same_chunk = (ri // C) == (ci // C)
    lower = same_chunk & (ri >= ci)
    strict = same_chunk & (ri > ci)
    eye = (ri == ci).astype(F32)
    lower_f = lower.astype(F32)

    chains = [(b, h) for b in range(n_seq) for h in range(H)]
    q_c, k_c, v_c, gcol_c, beta_c, grow_c = [], [], [], [], [], []
    for b in range(n_seq):
        xbuf_ref[b, 8:, :] = qkv_ref[b]
        conv = cw_ref[0:1, :] * xbuf_ref[b, 5:5 + rows, :]
        for j in range(1, GDN_CONV):
            conv = conv + cw_ref[j:j + 1, :] * xbuf_ref[b, 5 + j:5 + j + rows, :]
        xbuf_ref[b, 0:8, :] = xbuf_ref[b, rows:rows + 8, :]
        act = _silu(conv)

        ab = ab_ref[b]
        log_decay = -jnp.exp(alog_ref[...]) * _softplus(ab + dtb_ref[...])
        beta_all = jax.nn.sigmoid(ab)
        gc_all = jnp.concatenate(
            [_dot(lower_f, log_decay[g * G:(g + 1) * G], HIGHEST) for g in range(n_groups)], axis=0)
        gc_rows = gc_all.T
        for h in range(H):
            qh = act[:, h * Dh:(h + 1) * Dh]
            kh = act[:, W + h * Dh:W + (h + 1) * Dh]
            v_c.append(act[:, 2 * W + h * Dh:2 * W + (h + 1) * Dh])
            q_c.append(qh * (lax.rsqrt(jnp.sum(qh * qh, axis=-1, keepdims=True) + NORM_EPS) * Dh ** -0.5))
            k_c.append(kh * lax.rsqrt(jnp.sum(kh * kh, axis=-1, keepdims=True) + NORM_EPS))
            gcol_c.append(jnp.broadcast_to(gc_all[:, h:h + 1], (rows, Dh)))
            beta_c.append(jnp.broadcast_to(beta_all[:, H + h:H + h + 1], (rows, Dh)))
            grow_c.append(gc_rows[h:h + 1, :])

    units = [(c, g) for c in range(len(chains)) for g in range(n_groups)]

    def unit_rows(x, g):
        return x[g * G:(g + 1) * G]

    k_u = [unit_rows(k_c[c], g) for c, g in units]
    gcol_u = [unit_rows(gcol_c[c], g) for c, g in units]
    beta_u = [unit_rows(beta_c[c], g) for c, g in units]
    kb_u = [k * beta for k, beta in zip(k_u, beta_u)]
    k16_u = [k.astype(BF16) for k in k_u]
    q_u = [unit_rows(q_c[c], g) for c, g in units]
    kk_u = [_dot_nt(kb.astype(BF16), k16) for kb, k16 in zip(kb_u, k16_u)]
    qk_u = [_dot_nt(q.astype(BF16), k16) for q, k16 in zip(q_u, k16_u)]
    decay_u = []
    for gcol, (c, g) in zip(gcol_u, units):
        diff = gcol - grow_c[c][:, g * G:(g + 1) * G]
        decay_u.append(jnp.where(lower, jnp.exp(jnp.where(lower, diff, 0.0)), 0.0))
    a_u = [jnp.where(strict, kk * decay, 0.0) for kk, decay in zip(kk_u, decay_u)]
    qk16_u = [(qk * decay).astype(BF16) for qk, decay in zip(qk_u, decay_u)]
    tinv_u = _unit_lower_inverses(a_u, eye)
    egc_u = [jnp.exp(gcol) for gcol in gcol_u]
    rhs_u = [jnp.concatenate([unit_rows(v_c[c], g) * beta, kb * egc], axis=1).astype(BF16)
             for beta, kb, egc, (c, g) in zip(beta_u, kb_u, egc_u, units)]
    uw_u = [_dot(tinv.astype(BF16), rhs) for tinv, rhs in zip(tinv_u, rhs_u)]
    w16_u = [uw[:, Dh:].astype(BF16) for uw in uw_u]
    qe16_u = [(q * egc).astype(BF16) for q, egc in zip(q_u, egc_u)]

    def chunk_operands(c, n):
        u = c * n_groups + (n * C) // G
        r = slice((n * C) % G, (n * C) % G + C)
        g_last = gcol_u[u][r.stop - 1:r.stop, :]
        return (uw_u[u][r, :Dh],
                jnp.concatenate([w16_u[u][r], qe16_u[u][r]], axis=0),
                qk16_u[u][r, r],
                (k_u[u][r] * jnp.exp(g_last - gcol_u[u][r])).T.astype(BF16),
                jnp.exp(g_last))

    n_chains = len(chains)
    ops = [[chunk_operands(c, n) for n in range(n_chunks)] for c in range(n_chains)]

    state = [s_ref[c] for c in range(n_chains)]
    o_l = [[] for _ in range(n_chains)]
    for n in range(n_chunks):
        ws = [_dot(ops[c][n][1], state[c].astype(BF16)) for c in range(n_chains)]
        v_new = [(ops[c][n][0] - ws[c][:C]).astype(BF16) for c in range(n_chains)]
        state = [state[c] * ops[c][n][4] + _dot(ops[c][n][3], v_new[c]) for c in range(n_chains)]
        for c in range(n_chains):
            o_l[c].append(ws[c][C:] + _dot(ops[c][n][2], v_new[c]))
    for c, (b, h) in enumerate(chains):
        s_ref[c] = state[c]
        o = jnp.concatenate(o_l[c], axis=0)
        z = z_ref[b, :, h * Dh:(h + 1) * Dh]
        y_ref[b, :, h * Dh:(h + 1) * Dh] = (_rms_rows(o, og_ref[...]) * _silu(z)).astype(BF16)


def _gdn(pg, pab, conv_w, alog_vec, dtb_vec, out_gain, batch, seq):
    rows = GDN_ROWS
    n_seq = GDN_SEQS
    W = GDN_WIDTH
    pg3 = pg.reshape(batch, seq, pg.shape[1])
    pab3 = pab.reshape(batch, seq, pab.shape[1])
    y = pl.pallas_call(
        _gdn_kernel,
        grid=(batch // n_seq, seq // rows),
        in_specs=[pl.BlockSpec((n_seq, rows, 3 * W), lambda b, t: (b, t, 0)),
                  pl.BlockSpec((n_seq, rows, W), lambda b, t: (b, t, 3)),
                  pl.BlockSpec((n_seq, rows, pab.shape[1]), lambda b, t: (b, t, 0)),
                  _resident(), _resident(), _resident(), _resident()],
        out_specs=pl.BlockSpec((n_seq, rows, W), lambda b, t: (b, t, 0)),
        out_shape=jax.ShapeDtypeStruct((batch, seq, W), BF16),
        scratch_shapes=[pltpu.VMEM((n_seq, rows + 8, 3 * W), F32),
                        pltpu.VMEM((n_seq * GDN_HEADS, GDN_HEAD_DIM, GDN_HEAD_DIM), F32)],
        compiler_params=pltpu.CompilerParams(dimension_semantics=("arbitrary", "arbitrary"),
                                             vmem_limit_bytes=V7X_VMEM_LIMIT_BYTES),
        name="gdn",
    )(pg3, pg3, pab3, conv_w, alog_vec, dtb_vec, out_gain)
    return y.reshape(batch * seq, W)


def _alibi_slope(h):
    return 2.0 ** (-8.0 * (h + 1) / ATT_HEADS)


def _moba_kernel(p_ref, qg_ref, kg_ref, cg_ref, km_ref, vmt_ref, y_ref,
                 k_ref, vt_ref, kmean_ref, sel_ref, m_ref, acc_ref, s_ref, kaug_ref, rhs_ref):
    H, Dh, BLK, Wd = ATT_HEADS, ATT_HEAD_DIM, MOBA_BLOCK, ATT_WIDTH
    PAIR = 2 * Dh
    AUG = V7X_BF16_SUBLANES
    n_seq = p_ref.shape[0]
    nb = kmean_ref.shape[1]
    i = pl.program_id(1)
    pair_row_head = lax.broadcasted_iota(jnp.int32, (PAIR, BLK), 0) // Dh
    chains = [(b, h) for b in range(n_seq) for h in range(H)]

    @pl.when((pl.program_id(0) == 0) & (i == 0))
    def _():
        lane = lax.broadcasted_iota(jnp.int32, (BLK, V7X_LANES), 1)
        key_pos = lax.broadcasted_iota(jnp.int32, (BLK, V7X_LANES), 0).astype(F32)
        kaug_ref[...] = jnp.where(lane < BIAS_TERMS, key_pos,
                                  jnp.where(lane < 2 * BIAS_TERMS, 1.0, 0.0)).astype(BF16)
        rhs_ref[...] = jnp.zeros_like(rhs_ref)

    @pl.when(i == 0)
    def _():
        kmean_ref[...] = jnp.zeros_like(kmean_ref)

    def pair_lanes(h):
        return slice((h // 2) * PAIR, (h // 2 + 1) * PAIR)

    def value_rows(h):
        return slice(h * ATT_VROWS, (h + 1) * ATT_VROWS)

    def normalised(acc):
        return acc[0:Dh] / acc[Dh:Dh + 1]

    def head_queries(q_t, h):
        qp = q_t[pair_lanes(h), :]
        return jnp.where(pair_row_head == h % 2, qp, jnp.zeros_like(qp))

    block_ones = _head_block_ones(Wd, Dh)
    scale = Dh ** -0.5 * LOG2_E
    blk_row = lax.broadcasted_iota(jnp.int32, (nb, Wd), 0)
    lane_head = lax.broadcasted_iota(jnp.int32, (nb, Wd), 1) // Dh
    jdx_f = lax.broadcasted_iota(jnp.int32, (nb, BLK), 0).astype(F32)
    past = lax.broadcasted_iota(jnp.int32, (nb, BLK), 0) < i
    cq_t = []
    for b in range(n_seq):
        qn = _head_rms(p_ref[b, :, 0:Wd], qg_ref[...], block_ones, Dh)
        kn = _head_rms(p_ref[b, :, Wd:2 * Wd], kg_ref[...], block_ones, Dh)
        cqn = _head_rms(p_ref[b, :, 3 * Wd:4 * Wd], cg_ref[...], block_ones, Dh)
        q_t = (qn * scale).T.astype(BF16)
        cq_t.append((cqn * scale).T.astype(BF16))
        k_ref[b, pl.ds(pl.multiple_of(i * BLK, BLK), BLK), :] = kn.astype(BF16)
        vt_ref[b, i] = _values_with_ones(p_ref[b, :, 2 * Wd:3 * Wd].T)
        for h in range(H):
            rhs_ref[b * H + h, 0:PAIR, :] = head_queries(q_t, h)

        kmean = kmean_ref[b]
        kmean_heads = jnp.concatenate([jnp.where(lane_head == h, kmean, 0.0) for h in range(H)], axis=0)
        gate_all = _dot_nt(kmean_heads, qn, HIGHEST)
        for h in range(H):
            gate = jnp.where(past, gate_all[h * nb:(h + 1) * nb], -jnp.inf)
            chosen = jnp.zeros(gate.shape, jnp.bool_)
            for _ in range(MOBA_TOPK):
                top = jnp.max(gate, axis=0, keepdims=True)
                first = jnp.min(jnp.where(gate == top, jdx_f, float(nb)), axis=0, keepdims=True)
                pick = jdx_f == first
                chosen = chosen | pick
                gate = jnp.where(pick, -jnp.inf, gate)
            sel_ref[b * H + h] = jnp.where(chosen & past, 0.0, MASKED)
        kmean_ref[b] = jnp.where(blk_row == i, jnp.mean(kn, axis=0, keepdims=True), kmean)

    aug_row = lax.broadcasted_iota(jnp.int32, (AUG, BLK), 0)
    query_pos = lax.broadcasted_iota(jnp.int32, (1, BLK), 1).astype(F32)

    def scores(blk, own):
        r0 = pl.multiple_of(blk * BLK, BLK)
        distance = (i - blk).astype(F32) * BLK
        keys_aug = kaug_ref[...]
        out = []
        for c, (b, h) in enumerate(chains):
            slope = _alibi_slope(h) * LOG2_E
            row_bias = -slope * (query_pos + distance)
            if not own:
                row_bias = row_bias + sel_ref[c, pl.ds(blk, 1), :]
            rows = _bf16_terms_const(slope) + _bf16_terms(row_bias)
            aug = jnp.zeros((AUG, BLK), F32)
            for n, val in enumerate(rows):
                aug = jnp.where(aug_row == n, val, aug)
            rhs_ref[c, PAIR:PAIR + AUG, :] = aug.astype(BF16)
            keys = jnp.concatenate([k_ref[b, pl.ds(r0, BLK), pair_lanes(h)], keys_aug], axis=1)
            out.append(_dot(keys, rhs_ref[c]))
        return out

    own_scores = scores(i, True)
    mem_scores = [_dot(km_ref[b, :, pair_lanes(h)], head_queries(cq_t[b], h)) for b, h in chains]
    first_past = scores(0, False)
    causal = (lax.broadcasted_iota(jnp.int32, (BLK, BLK), 0) <= lax.broadcasted_iota(jnp.int32, (BLK, BLK), 1))
    own_p, mem_p = [], []
    for c in range(len(chains)):
        s_t = jnp.where(causal, own_scores[c], MASKED)
        m = jnp.max(s_t, axis=0, keepdims=True)
        p = jnp.exp2(s_t - m)
        m_ref[c] = m
        own_p.append(p.astype(BF16))
    for c in range(len(chains)):
        p = jnp.exp2(mem_scores[c] - jnp.max(mem_scores[c], axis=0, keepdims=True))
        mem_p.append(p.astype(BF16))
    for c, (b, h) in enumerate(chains):
        acc_ref[b * 2 * H + h] = _dot(vt_ref[b, i, value_rows(h), :], own_p[c])
        s_ref[0, c] = first_past[c]
    for c, (b, h) in enumerate(chains):
        acc_ref[b * 2 * H + H + h] = _dot(vmt_ref[b, value_rows(h), :], mem_p[c])

    def block_step(j, slot, nxt):
        next_scores = None if nxt is None else scores(nxt, False)
        for c, (b, h) in enumerate(chains):
            a = b * 2 * H + h
            m_old = m_ref[c]
            m_new = jnp.maximum(m_old, jnp.max(s_ref[slot, c], axis=0, keepdims=True))
            m_ref[c] = m_new
            p = jnp.exp2(s_ref[slot, c] - m_new)
            alpha = jnp.exp2(m_old - m_new)
            acc_ref[a] = alpha * acc_ref[a] + _dot(vt_ref[b, j, value_rows(h), :], p.astype(BF16))
        if next_scores is not None:
            for c in range(len(chains)):
                s_ref[1 - slot, c] = next_scores[c]

    last_past = jnp.maximum(i - 1, 0)

    def block_run(j0, count):
        for n in range(count):
            block_step(j0 + n, n % 2, jnp.minimum(j0 + n + 1, last_past))

    def full_run(jj, carry):
        block_run(MOBA_RUN * jj, MOBA_RUN)
        return carry

    lax.fori_loop(0, i // MOBA_RUN, full_run, 0)
    run = MOBA_RUN // 2
    while run >= 2:
        @pl.when(i % (2 * run) >= run)
        def _(run=run):
            block_run((2 * run) * (i // (2 * run)), run)
        run //= 2

    @pl.when(i % 2 == 1)
    def _():
        block_step(i - 1, 0, None)

    for b in range(n_seq):
        out_t = jnp.concatenate([normalised(acc_ref[b * 2 * H + a]) for a in range(2 * H)], axis=0)
        y_ref[b] = out_t.T.astype(BF16)


def _moba(patt, qg, kg, cg, kmem, vmem_t, batch, seq):
    nb = seq // MOBA_BLOCK
    Wd, H, BLK = ATT_WIDTH, ATT_HEADS, MOBA_BLOCK
    n_seq = MOBA_SEQS
    n_mem = kmem.shape[1]
    y = pl.pallas_call(
        _moba_kernel,
        grid=(batch // n_seq, nb),
        in_specs=[pl.BlockSpec((n_seq, BLK, 4 * Wd), lambda b, i: (b, i, 0)),
                  _resident(), _resident(), _resident(),
                  pl.BlockSpec((n_seq, n_mem, Wd), lambda b, i: (b, 0, 0)),
                  pl.BlockSpec((n_seq, H * ATT_VROWS, n_mem), lambda b, i: (b, 0, 0))],
        out_specs=pl.BlockSpec((n_seq, BLK, 2 * Wd), lambda b, i: (b, i, 0)),
        out_shape=jax.ShapeDtypeStruct((batch, seq, 2 * Wd), BF16),
        scratch_shapes=[pltpu.VMEM((n_seq, seq, Wd), BF16),
                        pltpu.VMEM((n_seq, nb, H * ATT_VROWS, BLK), BF16),
                        pltpu.VMEM((n_seq, nb, Wd), F32),
                        pltpu.VMEM((n_seq * H, nb, BLK), F32),
                        pltpu.VMEM((n_seq * H, 1, BLK), F32),
                        pltpu.VMEM((n_seq * 2 * H, ATT_VROWS, BLK), F32),
                        pltpu.VMEM((2, n_seq * H, BLK, BLK), F32),
                        pltpu.VMEM((BLK, V7X_LANES), BF16),
                        pltpu.VMEM((n_seq * H, 2 * ATT_HEAD_DIM + V7X_LANES, BLK), BF16)],
        compiler_params=pltpu.CompilerParams(dimension_semantics=("arbitrary", "arbitrary"),
                                             vmem_limit_bytes=V7X_VMEM_LIMIT_BYTES),
        name="moba",
    )(patt.reshape(batch, seq, 4 * Wd), qg, kg, cg, kmem, vmem_t)
    return y.reshape(batch * seq, 2 * Wd)


def _lane_vector(values, width):
    return jnp.zeros((1, width), F32).at[0, :values.shape[0]].set(values.astype(F32))


def _layer(x2d, mem, batch, seq, ffn1_norm, ffn1_w_gate, ffn1_w_up, ffn1_w_down, mix_norm, w_in,
           gdn_conv_w, gdn_a_log, gdn_dt_bias, gdn_out_norm, moba_q_norm, moba_k_norm,
           mem_norm, w_mem_kv, mem_q_norm, mem_k_norm, w_out, ffn2_norm, ffn2_w_gate, ffn2_w_up, ffn2_w_down):
    W, H = GDN_WIDTH, GDN_HEADS
    row = lambda v: v.reshape(1, -1).astype(F32)
    tile_heads = lambda v: jnp.tile(v.astype(F32), ATT_HEADS).reshape(1, -1)
    ab0 = 4 * W
    att0 = ab0 + 2 * H
    ab_width = V7X_LANES
    w_in16 = w_in.astype(BF16)
    x1, pg, pab, patt = _ffn1_inproj(
        x2d, row(ffn1_norm), ffn1_w_gate.astype(BF16), ffn1_w_up.astype(BF16), ffn1_w_down.astype(BF16),
        row(mix_norm), w_in16[:, :ab0 + ab_width], w_in16[:, att0:], ab0)

    y_gdn = _gdn(pg, pab, gdn_conv_w.astype(F32), _lane_vector(gdn_a_log, ab_width),
                 _lane_vector(gdn_dt_bias, ab_width), row(gdn_out_norm), batch, seq)

    kmem, vmem_t = _mem_kv(mem, row(mem_norm), w_mem_kv.astype(BF16), tile_heads(mem_k_norm))
    y_att = _moba(patt, tile_heads(moba_q_norm), tile_heads(moba_k_norm), tile_heads(mem_q_norm),
                  kmem, vmem_t, batch, seq)

    return _outproj_ffn2(x1, y_gdn, y_att, w_out.astype(BF16), row(ffn2_norm),
                         ffn2_w_gate.astype(BF16), ffn2_w_up.astype(BF16), ffn2_w_down.astype(BF16))


def kernel(x, mem, ffn1_norm, ffn1_w_gate, ffn1_w_up, ffn1_w_down, mix_norm, w_in, gdn_conv_w, gdn_a_log,
           gdn_dt_bias, gdn_out_norm, moba_q_norm, moba_k_norm, mem_norm, w_mem_kv, mem_q_norm, mem_k_norm,
           w_out, ffn2_norm, ffn2_w_gate, ffn2_w_up, ffn2_w_down):
    batch, seq, d = x.shape
    assert seq % MOBA_BLOCK == 0 and seq % GDN_ROWS == 0 and batch % GDN_SEQS == 0 and batch % MOBA_SEQS == 0
    assert (batch * seq) % (FFN1_SLABS * FFN_ROWS) == 0 and (batch * seq) % (FFN2_SLABS * FFN_ROWS) == 0
    depth = w_in.shape[0]
    x2d = x.reshape(batch * seq, d)
    for l in range(depth):
        x2d = _layer(x2d, mem, batch, seq, ffn1_norm[l], ffn1_w_gate[l], ffn1_w_up[l], ffn1_w_down[l],
                     mix_norm[l], w_in[l], gdn_conv_w[l], gdn_a_log[l], gdn_dt_bias[l], gdn_out_norm[l],
                     moba_q_norm[l], moba_k_norm[l], mem_norm[l], w_mem_kv[l], mem_q_norm[l], mem_k_norm[l],
                     w_out[l], ffn2_norm[l], ffn2_w_gate[l], ffn2_w_up[l], ffn2_w_down[l])
    return x2d.reshape(batch, seq, d)
```

```python
import struct

import jax
import jax.numpy as jnp
from jax import lax
from jax.experimental import pallas as pl
from jax.experimental.pallas import tpu as pltpu

F32 = jnp.float32
BF16 = jnp.bfloat16
HIGHEST = lax.Precision.HIGHEST

NORM_EPS = 1e-6
GDN_HEADS = 4
GDN_HEAD_DIM = 128
GDN_WIDTH = GDN_HEADS * GDN_HEAD_DIM
GDN_CONV = 4
GDN_CHUNK = 64
GDN_GROUP = 2 * GDN_CHUNK
ATT_HEADS = 4
ATT_HEAD_DIM = 64
ATT_WIDTH = ATT_HEADS * ATT_HEAD_DIM
MOBA_BLOCK = 256
MOBA_TOPK = 3
MOBA_SEQS = 2
MOBA_RUN = 4
MASKED = -1e30
LOG2_E = 1.4426950408889634
BIAS_TERMS = 3

V7X_VMEM_LIMIT_BYTES = 56 * 1024 * 1024
V7X_LANES = 128
V7X_BF16_SUBLANES = 16
ATT_VROWS = ATT_HEAD_DIM + V7X_BF16_SUBLANES
FFN_ROWS = 256
FFN1_SLABS = 2
FFN2_SLABS = 4
GDN_ROWS = 256
GDN_SEQS = 4


def _dot(a, b, precision=None):
    return jnp.dot(a, b, preferred_element_type=F32, precision=precision)


def _dot_nt(a, b, precision=None):
    return lax.dot_general(a, b, (((1,), (1,)), ((), ())), preferred_element_type=F32,
                           precision=precision)


def _rms_rows(x, gain):
    return x * lax.rsqrt(jnp.mean(x * x, axis=-1, keepdims=True) + NORM_EPS) * gain


def _silu(x):
    half = 0.5 * x
    return half + half * jnp.tanh(half)


def _swiglu(x, gain, wg_ref, wu_ref, wd_ref):
    h = _rms_rows(x, gain).astype(BF16)
    g = _dot(h, wg_ref[...])
    u = _dot(h, wu_ref[...])
    return _dot((_silu(g) * u).astype(BF16), wd_ref[...])


def _resident():
    return pl.BlockSpec(memory_space=pltpu.VMEM)


def _bf16_terms(x):
    terms, rest = [], x
    for _ in range(BIAS_TERMS):
        t = rest.astype(BF16).astype(F32)
        terms.append(t)
        rest = rest - t
    return terms


def _bf16_terms_const(value):
    def f32(v):
        return struct.unpack("<f", struct.pack("<f", v))[0]

    def bf16_round(v):
        bits = struct.unpack("<I", struct.pack("<f", v))[0]
        bits = (bits + 0x7FFF + ((bits >> 16) & 1)) & 0xFFFF0000
        return struct.unpack("<f", struct.pack("<I", bits))[0]

    terms, rest = [], f32(value)
    for _ in range(BIAS_TERMS):
        t = bf16_round(rest)
        terms.append(t)
        rest = f32(rest - t)
    return terms


def _ffn1_inproj_kernel(x_ref, n1_ref, wg_ref, wu_ref, wd_ref, n2_ref, win_ref, watt_ref,
                        x1_ref, pg_ref, pab_ref, patt_ref):
    gdn_w, ab_w = pg_ref.shape[1], pab_ref.shape[1]
    for r0 in range(0, x_ref.shape[0], FFN_ROWS):
        rows = slice(r0, r0 + FFN_ROWS)
        x = x_ref[rows, :]
        x1 = x + 0.5 * _swiglu(x, n1_ref[...], wg_ref, wu_ref, wd_ref)
        x1_ref[rows, :] = x1
        h = _rms_rows(x1, n2_ref[...]).astype(BF16)
        pg_ref[rows, :] = _dot(h, win_ref[:, 0:gdn_w])
        pab_ref[rows, :] = _dot(h, win_ref[:, gdn_w:gdn_w + ab_w])
        patt_ref[rows, :] = _dot(h, watt_ref[...])


def _ffn1_inproj(x2d, n1, wg, wu, wd, n2, win, watt, gdn_w):
    n, d = x2d.shape
    ab_w = win.shape[1] - gdn_w
    att_w = watt.shape[1]
    rows = FFN1_SLABS * FFN_ROWS
    row_spec = lambda w: pl.BlockSpec((rows, w), lambda i: (i, 0))
    return pl.pallas_call(
        _ffn1_inproj_kernel,
        grid=(n // rows,),
        in_specs=[row_spec(d)] + [_resident()] * 7,
        out_specs=[row_spec(d), row_spec(gdn_w), row_spec(ab_w), row_spec(att_w)],
        out_shape=[jax.ShapeDtypeStruct((n, d), F32),
                   jax.ShapeDtypeStruct((n, gdn_w), F32),
                   jax.ShapeDtypeStruct((n, ab_w), F32),
                   jax.ShapeDtypeStruct((n, att_w), F32)],
        compiler_params=pltpu.CompilerParams(dimension_semantics=("arbitrary",),
                                             vmem_limit_bytes=V7X_VMEM_LIMIT_BYTES),
        name="ffn1_inproj",
    )(x2d, n1, wg, wu, wd, n2, win, watt)


def _outproj_ffn2_kernel(x1_ref, og_ref, z_ref, ya_ref, ogain_ref, wo_ref, n_ref, wg_ref, wu_ref, wd_ref,
                         o_ref):
    gdn_w = og_ref.shape[1]
    Dh = GDN_HEAD_DIM
    for r0 in range(0, x1_ref.shape[0], FFN_ROWS):
        rows = slice(r0, r0 + FFN_ROWS)
        o_gdn = og_ref[rows, :].astype(F32)
        z = z_ref[rows, :]
        y_gdn = jnp.concatenate(
            [_rms_rows(o_gdn[:, h * Dh:(h + 1) * Dh], ogain_ref[...]) * _silu(z[:, h * Dh:(h + 1) * Dh])
             for h in range(GDN_HEADS)], axis=1).astype(BF16)
        x2 = (x1_ref[rows, :] + _dot(y_gdn, wo_ref[0:gdn_w, :])
              + _dot(ya_ref[rows, :], wo_ref[gdn_w:, :]))
        o_ref[rows, :] = x2 + 0.5 * _swiglu(x2, n_ref[...], wg_ref, wu_ref, wd_ref)


def _outproj_ffn2(x1, o_gdn, pg, ya, ogain, wo, nrm, wg, wu, wd):
    n, d = x1.shape
    rows = FFN2_SLABS * FFN_ROWS
    row_spec = lambda w: pl.BlockSpec((rows, w), lambda i: (i, 0))
    return pl.pallas_call(
        _outproj_ffn2_kernel,
        grid=(n // rows,),
        in_specs=[row_spec(d), row_spec(o_gdn.shape[1]),
                  pl.BlockSpec((rows, GDN_WIDTH), lambda i: (i, 3)),
                  row_spec(ya.shape[1])] + [_resident()] * 6,
        out_specs=row_spec(d),
        out_shape=jax.ShapeDtypeStruct((n, d), F32),
        compiler_params=pltpu.CompilerParams(dimension_semantics=("arbitrary",),
                                             vmem_limit_bytes=V7X_VMEM_LIMIT_BYTES),
        name="outproj_ffn2",
    )(x1, o_gdn, pg, ya, ogain, wo, nrm, wg, wu, wd)


def _head_block_ones(width, head_dim):
    r = lax.broadcasted_iota(jnp.int32, (width, width), 0) // head_dim
    c = lax.broadcasted_iota(jnp.int32, (width, width), 1) // head_dim
    return jnp.where(r == c, 1.0, 0.0).astype(BF16)


def _head_rms(x, gain, block_ones, head_dim):
    sq = x * x
    hi = sq.astype(BF16)
    lo = (sq - hi.astype(F32)).astype(BF16)
    ss = _dot(hi, block_ones) + _dot(lo, block_ones)
    return x * lax.rsqrt(ss * (1.0 / head_dim) + NORM_EPS) * gain


def _values_with_ones(v_t):
    Dh = ATT_HEAD_DIM
    row = lax.broadcasted_iota(jnp.int32, (V7X_BF16_SUBLANES, v_t.shape[1]), 0)
    ones_block = jnp.where(row == 0, 1.0, 0.0)
    parts = []
    for h in range(ATT_HEADS):
        parts += [v_t[h * Dh:(h + 1) * Dh], ones_block]
    return jnp.concatenate(parts, axis=0).astype(BF16)


def _mem_kv_kernel(mem_ref, nrm_ref, wkv_ref, kg_ref, k_ref, vt_ref):
    h = _rms_rows(mem_ref[...], nrm_ref[...]).astype(BF16)
    kv = _dot(h, wkv_ref[...])
    k = _head_rms(kv[:, :ATT_WIDTH], kg_ref[...], _head_block_ones(ATT_WIDTH, ATT_HEAD_DIM), ATT_HEAD_DIM)
    k_ref[...] = k.astype(BF16)
    vt_ref[...] = _values_with_ones(kv[:, ATT_WIDTH:].T)


def _mem_kv(mem, nrm, wkv, kgain):
    b, m, d = mem.shape
    return pl.pallas_call(
        _mem_kv_kernel,
        grid=(b,),
        in_specs=[pl.BlockSpec((None, m, d), lambda i: (i, 0, 0)), _resident(), _resident(), _resident()],
        out_specs=[pl.BlockSpec((None, m, ATT_WIDTH), lambda i: (i, 0, 0)),
                   pl.BlockSpec((None, ATT_HEADS * ATT_VROWS, m), lambda i: (i, 0, 0))],
        out_shape=[jax.ShapeDtypeStruct((b, m, ATT_WIDTH), BF16),
                   jax.ShapeDtypeStruct((b, ATT_HEADS * ATT_VROWS, m), BF16)],
        compiler_params=pltpu.CompilerParams(dimension_semantics=("arbitrary",)),
        name="mem_kv",
    )(mem, nrm, wkv, kgain)


def _softplus(x):
    return jnp.maximum(x, 0.0) + jnp.log(1.0 + jnp.exp(-jnp.abs(x)))


def _unit_lower_inverses(a_list, eye):
    x16 = [(-a).astype(BF16) for a in a_list]
    t = [eye - a for a in a_list]
    for _ in range(5):
        x16 = [_dot(xb, xb).astype(BF16) for xb in x16]
        t = [ti + _dot(ti.astype(BF16), xb) for ti, xb in zip(t, x16)]
    t16 = [ti.astype(BF16) for ti in t]
    a_hi = [a.astype(BF16) for a in a_list]
    a_lo = [(a - hi.astype(F32)).astype(BF16) for a, hi in zip(a_list, a_hi)]
    resid = [(eye - tb.astype(F32) - _dot(hi, tb) - _dot(lo, tb)).astype(BF16)
             for tb, hi, lo in zip(t16, a_hi, a_lo)]
    return [tb.astype(F32) + _dot(tb, r) for tb, r in zip(t16, resid)]


def _gdn_kernel(qkv_ref, ab_ref, cw_ref, alog_ref, dtb_ref, o_ref, xbuf_ref, s_ref):
    n_seq, rows = o_ref.shape[0], o_ref.shape[1]
    H, Dh, C, W = GDN_HEADS, GDN_HEAD_DIM, GDN_CHUNK, GDN_WIDTH
    G = GDN_GROUP
    n_chunks = rows // C
    n_groups = rows // G

    @pl.when(pl.program_id(1) == 0)
    def _():
        xbuf_ref[:, 0:8, :] = jnp.zeros((n_seq, 8, 3 * W), F32)
        s_ref[...] = jnp.zeros_like(s_ref)

    ri = lax.broadcasted_iota(jnp.int32, (G, G), 0)
    ci = lax.broadcasted_iota(jnp.int32, (G, G), 1)
    same_chunk = (ri // C) == (ci // C)
    lower = same_chunk & (ri >= ci)
    strict = same_chunk & (ri > ci)
    eye = (ri == ci).astype(F32)
    lower_f = lower.astype(F32)

    chains = [(b, h) for b in range(n_seq) for h in range(H)]
    q_c, k_c, v_c, gcol_c, beta_c, grow_c = [], [], [], [], [], []
    for b in range(n_seq):
        xbuf_ref[b, 8:, :] = qkv_ref[b]
        conv = cw_ref[0:1, :] * xbuf_ref[b, 5:5 + rows, :]
        for j in range(1, GDN_CONV):
            conv = conv + cw_ref[j:j + 1, :] * xbuf_ref[b, 5 + j:5 + j + rows, :]
        xbuf_ref[b, 0:8, :] = xbuf_ref[b, rows:rows + 8, :]
        act = _silu(conv)

        ab = ab_ref[b]
        log_decay = -jnp.exp(alog_ref[...]) * _softplus(ab + dtb_ref[...])
        beta_all = jax.nn.sigmoid(ab)
        gc_all = jnp.concatenate(
            [_dot(lower_f, log_decay[g * G:(g + 1) * G], HIGHEST) for g in range(n_groups)], axis=0)
        gc_rows = gc_all.T
        for h in range(H):
            qh = act[:, h * Dh:(h + 1) * Dh]
            kh = act[:, W + h * Dh:W + (h + 1) * Dh]
            v_c.append(act[:, 2 * W + h * Dh:2 * W + (h + 1) * Dh])
            q_c.append(qh * (lax.rsqrt(jnp.sum(qh * qh, axis=-1, keepdims=True) + NORM_EPS) * Dh ** -0.5))
            k_c.append(kh * lax.rsqrt(jnp.sum(kh * kh, axis=-1, keepdims=True) + NORM_EPS))
            gcol_c.append(jnp.broadcast_to(gc_all[:, h:h + 1], (rows, Dh)))
            beta_c.append(jnp.broadcast_to(beta_all[:, H + h:H + h + 1], (rows, Dh)))
            grow_c.append(gc_rows[h:h + 1, :])

    units = [(c, g) for c in range(len(chains)) for g in range(n_groups)]

    def unit_rows(x, g):
        return x[g * G:(g + 1) * G]

    k_u = [unit_rows(k_c[c], g) for c, g in units]
    gcol_u = [unit_rows(gcol_c[c], g) for c, g in units]
    beta_u = [unit_rows(beta_c[c], g) for c, g in units]
    kb_u = [k * beta for k, beta in zip(k_u, beta_u)]
    k16_u = [k.astype(BF16) for k in k_u]
    q_u = [unit_rows(q_c[c], g) for c, g in units]
    kk_u = [_dot_nt(kb.astype(BF16), k16) for kb, k16 in zip(kb_u, k16_u)]
    qk_u = [_dot_nt(q.astype(BF16), k16) for q, k16 in zip(q_u, k16_u)]
    decay_u = []
    for gcol, (c, g) in zip(gcol_u, units):
        diff = gcol - grow_c[c][:, g * G:(g + 1) * G]
        decay_u.append(jnp.where(lower, jnp.exp(jnp.where(lower, diff, 0.0)), 0.0))
    a_u = [jnp.where(strict, kk * decay, 0.0) for kk, decay in zip(kk_u, decay_u)]
    qk16_u = [(qk * decay).astype(BF16) for qk, decay in zip(qk_u, decay_u)]
    tinv_u = _unit_lower_inverses(a_u, eye)
    egc_u = [jnp.exp(gcol) for gcol in gcol_u]
    rhs_u = [jnp.concatenate([unit_rows(v_c[c], g) * beta, kb * egc], axis=1).astype(BF16)
             for beta, kb, egc, (c, g) in zip(beta_u, kb_u, egc_u, units)]
    uw_u = [_dot(tinv.astype(BF16), rhs) for tinv, rhs in zip(tinv_u, rhs_u)]
    w16_u = [uw[:, Dh:].astype(BF16) for uw in uw_u]
    qe16_u = [(q * egc).astype(BF16) for q, egc in zip(q_u, egc_u)]

    def chunk_operands(c, n):
        u = c * n_groups + (n * C) // G
        r = slice((n * C) % G, (n * C) % G + C)
        g_last = gcol_u[u][r.stop - 1:r.stop, :]
        return (uw_u[u][r, :Dh],
                jnp.concatenate([w16_u[u][r], qe16_u[u][r]], axis=0),
                qk16_u[u][r, r],
                (k_u[u][r] * jnp.exp(g_last - gcol_u[u][r])).T.astype(BF16),
                jnp.exp(g_last))

    n_chains = len(chains)
    ops = [[chunk_operands(c, n) for n in range(n_chunks)] for c in range(n_chains)]

    state = [s_ref[c] for c in range(n_chains)]
    o_l = [[] for _ in range(n_chains)]
    for n in range(n_chunks):
        ws = [_dot(ops[c][n][1], state[c].astype(BF16)) for c in range(n_chains)]
        v_new = [(ops[c][n][0] - ws[c][:C]).astype(BF16) for c in range(n_chains)]
        state = [state[c] * ops[c][n][4] + _dot(ops[c][n][3], v_new[c]) for c in range(n_chains)]
        for c in range(n_chains):
            o_l[c].append(ws[c][C:] + _dot(ops[c][n][2], v_new[c]))
    for c, (b, h) in enumerate(chains):
        s_ref[c] = state[c]
        o_ref[b, :, h * Dh:(h + 1) * Dh] = jnp.concatenate(o_l[c], axis=0).astype(BF16)


def _gdn(pg, pab, conv_w, alog_vec, dtb_vec, batch, seq):
    rows = GDN_ROWS
    n_seq = GDN_SEQS
    W = GDN_WIDTH
    pg3 = pg.reshape(batch, seq, pg.shape[1])
    pab3 = pab.reshape(batch, seq, pab.shape[1])
    y = pl.pallas_call(
        _gdn_kernel,
        grid=(batch // n_seq, seq // rows),
        in_specs=[pl.BlockSpec((n_seq, rows, 3 * W), lambda b, t: (b, t, 0)),
                  pl.BlockSpec((n_seq, rows, pab.shape[1]), lambda b, t: (b, t, 0)),
                  _resident(), _resident(), _resident()],
        out_specs=pl.BlockSpec((n_seq, rows, W), lambda b, t: (b, t, 0)),
        out_shape=jax.ShapeDtypeStruct((batch, seq, W), BF16),
        scratch_shapes=[pltpu.VMEM((n_seq, rows + 8, 3 * W), F32),
                        pltpu.VMEM((n_seq * GDN_HEADS, GDN_HEAD_DIM, GDN_HEAD_DIM), F32)],
        compiler_params=pltpu.CompilerParams(dimension_semantics=("arbitrary", "arbitrary"),
                                             vmem_limit_bytes=V7X_VMEM_LIMIT_BYTES),
        name="gdn",
    )(pg3, pab3, conv_w, alog_vec, dtb_vec)
    return y.reshape(batch * seq, W)


def _alibi_slope(h):
    return 2.0 ** (-8.0 * (h + 1) / ATT_HEADS)


def _moba_kernel(p_ref, qg_ref, kg_ref, cg_ref, km_ref, vmt_ref, y_ref,
                 k_ref, vt_ref, kmean_ref, sel_ref, m_ref, acc_ref, s_ref, kaug_ref, rhs_ref):
    H, Dh, BLK, Wd = ATT_HEADS, ATT_HEAD_DIM, MOBA_BLOCK, ATT_WIDTH
    PAIR = 2 * Dh
    AUG = V7X_BF16_SUBLANES
    n_seq = p_ref.shape[0]
    nb = kmean_ref.shape[1]
    i = pl.program_id(1)
    pair_row_head = lax.broadcasted_iota(jnp.int32, (PAIR, BLK), 0) // Dh
    chains = [(b, h) for b in range(n_seq) for h in range(H)]

    @pl.when((pl.program_id(0) == 0) & (i == 0))
    def _():
        lane = lax.broadcasted_iota(jnp.int32, (BLK, V7X_LANES), 1)
        key_pos = lax.broadcasted_iota(jnp.int32, (BLK, V7X_LANES), 0).astype(F32)
        kaug_ref[...] = jnp.where(lane < BIAS_TERMS, key_pos,
                                  jnp.where(lane < 2 * BIAS_TERMS, 1.0, 0.0)).astype(BF16)
        rhs_ref[...] = jnp.zeros_like(rhs_ref)

    @pl.when(i == 0)
    def _():
        kmean_ref[...] = jnp.zeros_like(kmean_ref)

    def pair_lanes(h):
        return slice((h // 2) * PAIR, (h // 2 + 1) * PAIR)

    def value_rows(h):
        return slice(h * ATT_VROWS, (h + 1) * ATT_VROWS)

    def normalised(acc):
        return acc[0:Dh] / acc[Dh:Dh + 1]

    def head_queries(q_t, h):
        qp = q_t[pair_lanes(h), :]
        return jnp.where(pair_row_head == h % 2, qp, jnp.zeros_like(qp))

    block_ones = _head_block_ones(Wd, Dh)
    scale = Dh ** -0.5 * LOG2_E
    blk_row = lax.broadcasted_iota(jnp.int32, (nb, Wd), 0)
    lane_head = lax.broadcasted_iota(jnp.int32, (nb, Wd), 1) // Dh
    jdx_f = lax.broadcasted_iota(jnp.int32, (nb, BLK), 0).astype(F32)
    past = lax.broadcasted_iota(jnp.int32, (nb, BLK), 0) < i
    cq_t = []
    for b in range(n_seq):
        qn = _head_rms(p_ref[b, :, 0:Wd], qg_ref[...], block_ones, Dh)
        kn = _head_rms(p_ref[b, :, Wd:2 * Wd], kg_ref[...], block_ones, Dh)
        cqn = _head_rms(p_ref[b, :, 3 * Wd:4 * Wd], cg_ref[...], block_ones, Dh)
        q_t = (qn * scale).T.astype(BF16)
        cq_t.append((cqn * scale).T.astype(BF16))
        k_ref[b, pl.ds(pl.multiple_of(i * BLK, BLK), BLK), :] = kn.astype(BF16)
        vt_ref[b, i] = _values_with_ones(p_ref[b, :, 2 * Wd:3 * Wd].T)
        for h in range(H):
            rhs_ref[b * H + h, 0:PAIR, :] = head_queries(q_t, h)

        kmean = kmean_ref[b]
        kmean_heads = jnp.concatenate([jnp.where(lane_head == h, kmean, 0.0) for h in range(H)], axis=0)
        gate_all = _dot_nt(kmean_heads, qn, HIGHEST)
        for h in range(H):
            gate = jnp.where(past, gate_all[h * nb:(h + 1) * nb], -jnp.inf)
            chosen = jnp.zeros(gate.shape, jnp.bool_)
            for _ in range(MOBA_TOPK):
                top = jnp.max(gate, axis=0, keepdims=True)
                first = jnp.min(jnp.where(gate == top, jdx_f, float(nb)), axis=0, keepdims=True)
                pick = jdx_f == first
                chosen = chosen | pick
                gate = jnp.where(pick, -jnp.inf, gate)
            sel_ref[b * H + h] = jnp.where(chosen & past, 0.0, MASKED)
        kmean_ref[b] = jnp.where(blk_row == i, jnp.mean(kn, axis=0, keepdims=True), kmean)

    aug_row = lax.broadcasted_iota(jnp.int32, (AUG, BLK), 0)
    query_pos = lax.broadcasted_iota(jnp.int32, (1, BLK), 1).astype(F32)

    def scores(blk, own):
        r0 = pl.multiple_of(blk * BLK, BLK)
        distance = (i - blk).astype(F32) * BLK
        keys_aug = kaug_ref[...]
        out = []
        for c, (b, h) in enumerate(chains):
            slope = _alibi_slope(h) * LOG2_E
            row_bias = -slope * (query_pos + distance)
            if not own:
                row_bias = row_bias + sel_ref[c, pl.ds(blk, 1), :]
            rows = _bf16_terms_const(slope) + _bf16_terms(row_bias)
            aug = jnp.zeros((AUG, BLK), F32)
            for n, val in enumerate(rows):
                aug = jnp.where(aug_row == n, val, aug)
            rhs_ref[c, PAIR:PAIR + AUG, :] = aug.astype(BF16)
            keys = jnp.concatenate([k_ref[b, pl.ds(r0, BLK), pair_lanes(h)], keys_aug], axis=1)
            out.append(_dot(keys, rhs_ref[c]))
        return out

    own_scores = scores(i, True)
    mem_scores = [_dot(km_ref[b, :, pair_lanes(h)], head_queries(cq_t[b], h)) for b, h in chains]
    first_past = scores(0, False)
    causal = (lax.broadcasted_iota(jnp.int32, (BLK, BLK), 0) <= lax.broadcasted_iota(jnp.int32, (BLK, BLK), 1))
    own_p, mem_p = [], []
    for c in range(len(chains)):
        s_t = jnp.where(causal, own_scores[c], MASKED)
        m = jnp.max(s_t, axis=0, keepdims=True)
        p = jnp.exp2(s_t - m)
        m_ref[c] = m
        own_p.append(p.astype(BF16))
    for c in range(len(chains)):
        p = jnp.exp2(mem_scores[c] - jnp.max(mem_scores[c], axis=0, keepdims=True))
        mem_p.append(p.astype(BF16))
    for c, (b, h) in enumerate(chains):
        acc_ref[b * 2 * H + h] = _dot(vt_ref[b, i, value_rows(h), :], own_p[c])
        s_ref[0, c] = first_past[c]
    for c, (b, h) in enumerate(chains):
        acc_ref[b * 2 * H + H + h] = _dot(vmt_ref[b, value_rows(h), :], mem_p[c])

    def block_step(j, slot, nxt):
        next_scores = None if nxt is None else scores(nxt, False)
        for c, (b, h) in enumerate(chains):
            a = b * 2 * H + h
            m_old = m_ref[c]
            m_new = jnp.maximum(m_old, jnp.max(s_ref[slot, c], axis=0, keepdims=True))
            m_ref[c] = m_new
            p = jnp.exp2(s_ref[slot, c] - m_new)
            alpha = jnp.exp2(m_old - m_new)
            acc_ref[a] = alpha * acc_ref[a] + _dot(vt_ref[b, j, value_rows(h), :], p.astype(BF16))
        if next_scores is not None:
            for c in range(len(chains)):
                s_ref[1 - slot, c] = next_scores[c]

    last_past = jnp.maximum(i - 1, 0)

    def block_run(j0, count):
        for n in range(count):
            block_step(j0 + n, n % 2, jnp.minimum(j0 + n + 1, last_past))

    def full_run(jj, carry):
        block_run(MOBA_RUN * jj, MOBA_RUN)
        return carry

    lax.fori_loop(0, i // MOBA_RUN, full_run, 0)
    run = MOBA_RUN // 2
    while run >= 2:
        @pl.when(i % (2 * run) >= run)
        def _(run=run):
            block_run((2 * run) * (i // (2 * run)), run)
        run //= 2

    @pl.when(i % 2 == 1)
    def _():
        block_step(i - 1, 0, None)

    for b in range(n_seq):
        out_t = jnp.concatenate([normalised(acc_ref[b * 2 * H + a]) for a in range(2 * H)], axis=0)
        y_ref[b] = out_t.T.astype(BF16)


def _moba(patt, qg, kg, cg, kmem, vmem_t, batch, seq):
    nb = seq // MOBA_BLOCK
    Wd, H, BLK = ATT_WIDTH, ATT_HEADS, MOBA_BLOCK
    n_seq = MOBA_SEQS
    n_mem = kmem.shape[1]
    y = pl.pallas_call(
        _moba_kernel,
        grid=(batch // n_seq, nb),
        in_specs=[pl.BlockSpec((n_seq, BLK, 4 * Wd), lambda b, i: (b, i, 0)),
                  _resident(), _resident(), _resident(),
                  pl.BlockSpec((n_seq, n_mem, Wd), lambda b, i: (b, 0, 0)),
                  pl.BlockSpec((n_seq, H * ATT_VROWS, n_mem), lambda b, i: (b, 0, 0))],
        out_specs=pl.BlockSpec((n_seq, BLK, 2 * Wd), lambda b, i: (b, i, 0)),
        out_shape=jax.ShapeDtypeStruct((batch, seq, 2 * Wd), BF16),
        scratch_shapes=[pltpu.VMEM((n_seq, seq, Wd), BF16),
                        pltpu.VMEM((n_seq, nb, H * ATT_VROWS, BLK), BF16),
                        pltpu.VMEM((n_seq, nb, Wd), F32),
                        pltpu.VMEM((n_seq * H, nb, BLK), F32),
                        pltpu.VMEM((n_seq * H, 1, BLK), F32),
                        pltpu.VMEM((n_seq * 2 * H, ATT_VROWS, BLK), F32),
                        pltpu.VMEM((2, n_seq * H, BLK, BLK), F32),
                        pltpu.VMEM((BLK, V7X_LANES), BF16),
                        pltpu.VMEM((n_seq * H, 2 * ATT_HEAD_DIM + V7X_LANES, BLK), BF16)],
        compiler_params=pltpu.CompilerParams(dimension_semantics=("arbitrary", "arbitrary"),
                                             vmem_limit_bytes=V7X_VMEM_LIMIT_BYTES),
        name="moba",
    )(patt.reshape(batch, seq, 4 * Wd), qg, kg, cg, kmem, vmem_t)
    return y.reshape(batch * seq, 2 * Wd)


def _lane_vector(values, width):
    return jnp.zeros((1, width), F32).at[0, :values.shape[0]].set(values.astype(F32))


def _layer(x2d, mem, batch, seq, ffn1_norm, ffn1_w_gate, ffn1_w_up, ffn1_w_down, mix_norm, w_in,
           gdn_conv_w, gdn_a_log, gdn_dt_bias, gdn_out_norm, moba_q_norm, moba_k_norm,
           mem_norm, w_mem_kv, mem_q_norm, mem_k_norm, w_out, ffn2_norm, ffn2_w_gate, ffn2_w_up, ffn2_w_down):
    W, H = GDN_WIDTH, GDN_HEADS
    row = lambda v: v.reshape(1, -1).astype(F32)
    tile_heads = lambda v: jnp.tile(v.astype(F32), ATT_HEADS).reshape(1, -1)
    ab0 = 4 * W
    att0 = ab0 + 2 * H
    ab_width = V7X_LANES
    w_in16 = w_in.astype(BF16)
    x1, pg, pab, patt = _ffn1_inproj(
        x2d, row(ffn1_norm), ffn1_w_gate.astype(BF16), ffn1_w_up.astype(BF16), ffn1_w_down.astype(BF16),
        row(mix_norm), w_in16[:, :ab0 + ab_width], w_in16[:, att0:], ab0)

    o_gdn = _gdn(pg, pab, gdn_conv_w.astype(F32), _lane_vector(gdn_a_log, ab_width),
                 _lane_vector(gdn_dt_bias, ab_width), batch, seq)

    kmem, vmem_t = _mem_kv(mem, row(mem_norm), w_mem_kv.astype(BF16), tile_heads(mem_k_norm))
    y_att = _moba(patt, tile_heads(moba_q_norm), tile_heads(moba_k_norm), tile_heads(mem_q_norm),
                  kmem, vmem_t, batch, seq)

    return _outproj_ffn2(x1, o_gdn, pg, y_att, row(gdn_out_norm), w_out.astype(BF16), row(ffn2_norm),
                         ffn2_w_gate.astype(BF16), ffn2_w_up.astype(BF16), ffn2_w_down.astype(BF16))


def kernel(x, mem, ffn1_norm, ffn1_w_gate, ffn1_w_up, ffn1_w_down, mix_norm, w_in, gdn_conv_w, gdn_a_log,
           gdn_dt_bias, gdn_out_norm, moba_q_norm, moba_k_norm, mem_norm, w_mem_kv, mem_q_norm, mem_k_norm,
           w_out, ffn2_norm, ffn2_w_gate, ffn2_w_up, ffn2_w_down):
    batch, seq, d = x.shape
    assert seq % MOBA_BLOCK == 0 and seq % GDN_ROWS == 0 and batch % GDN_SEQS == 0 and batch % MOBA_SEQS == 0
    assert (batch * seq) % (FFN1_SLABS * FFN_ROWS) == 0 and (batch * seq) % (FFN2_SLABS * FFN_ROWS) == 0
    depth = w_in.shape[0]
    x2d = x.reshape(batch * seq, d)
    for l in range(depth):
        x2d = _layer(x2d, mem, batch, seq, ffn1_norm[l], ffn1_w_gate[l], ffn1_w_up[l], ffn1_w_down[l],
                     mix_norm[l], w_in[l], gdn_conv_w[l], gdn_a_log[l], gdn_dt_bias[l], gdn_out_norm[l],
                     moba_q_norm[l], moba_k_norm[l], mem_norm[l], w_mem_kv[l], mem_q_norm[l], mem_k_norm[l],
                     w_out[l], ffn2_norm[l], ffn2_w_gate[l], ffn2_w_up[l], ffn2_w_down[l])
    return x2d.reshape(batch, seq, d)
```

```python
import struct

import jax
import jax.numpy as jnp
from jax import lax
from jax.experimental import pallas as pl
from jax.experimental.pallas import tpu as pltpu

F32 = jnp.float32
BF16 = jnp.bfloat16
HIGHEST = lax.Precision.HIGHEST

NORM_EPS = 1e-6
GDN_HEADS = 4
GDN_HEAD_DIM = 128
GDN_WIDTH = GDN_HEADS * GDN_HEAD_DIM
GDN_CONV = 4
GDN_CHUNK = 64
GDN_GROUP = 2 * GDN_CHUNK
ATT_HEADS = 4
ATT_HEAD_DIM = 64
ATT_WIDTH = ATT_HEADS * ATT_HEAD_DIM
MOBA_BLOCK = 256
MOBA_TOPK = 3
MOBA_SEQS = 2
MOBA_RUN = 4
MASKED = -1e30
LOG2_E = 1.4426950408889634
BIAS_TERMS = 3

V7X_VMEM_LIMIT_BYTES = 56 * 1024 * 1024
V7X_LANES = 128
V7X_BF16_SUBLANES = 16
ATT_VROWS = ATT_HEAD_DIM + V7X_BF16_SUBLANES
FFN_ROWS = 256
FFN1_SLABS = 2
FFN2_SLABS = 4
GDN_ROWS = 256
GDN_SEQS = 4


def _dot(a, b, precision=None):
    return jnp.dot(a, b, preferred_element_type=F32, precision=precision)


def _dot_nt(a, b, precision=None):
    return lax.dot_general(a, b, (((1,), (1,)), ((), ())), preferred_element_type=F32,
                           precision=precision)


def _rms_rows(x, gain):
    return x * lax.rsqrt(jnp.mean(x * x, axis=-1, keepdims=True) + NORM_EPS) * gain


def _silu(x):
    half = 0.5 * x
    return half + half * jnp.tanh(half)


def _swiglu(x, gain, wg_ref, wu_ref, wd_ref):
    h = _rms_rows(x, gain).astype(BF16)
    g = _dot(h, wg_ref[...])
    u = _dot(h, wu_ref[...])
    return _dot((_silu(g) * u).astype(BF16), wd_ref[...])


def _resident():
    return pl.BlockSpec(memory_space=pltpu.VMEM)


def _bf16_terms(x):
    terms, rest = [], x
    for _ in range(BIAS_TERMS):
        t = rest.astype(BF16).astype(F32)
        terms.append(t)
        rest = rest - t
    return terms


def _bf16_terms_const(value):
    def f32(v):
        return struct.unpack("<f", struct.pack("<f", v))[0]

    def bf16_round(v):
        bits = struct.unpack("<I", struct.pack("<f", v))[0]
        bits = (bits + 0x7FFF + ((bits >> 16) & 1)) & 0xFFFF0000
        return struct.unpack("<f", struct.pack("<I", bits))[0]

    terms, rest = [], f32(value)
    for _ in range(BIAS_TERMS):
        t = bf16_round(rest)
        terms.append(t)
        rest = f32(rest - t)
    return terms


def _ffn1_inproj_kernel(x_ref, n1_ref, wg_ref, wu_ref, wd_ref, n2_ref, win_ref, watt_ref,
                        x1_ref, pg_ref, pab_ref, patt_ref):
    gdn_w, ab_w = pg_ref.shape[1], pab_ref.shape[1]
    for r0 in range(0, x_ref.shape[0], FFN_ROWS):
        rows = slice(r0, r0 + FFN_ROWS)
        x = x_ref[rows, :]
        x1 = x + 0.5 * _swiglu(x, n1_ref[...], wg_ref, wu_ref, wd_ref)
        x1_ref[rows, :] = x1
        h = _rms_rows(x1, n2_ref[...]).astype(BF16)
        pg_ref[rows, :] = _dot(h, win_ref[:, 0:gdn_w])
        pab_ref[rows, :] = _dot(h, win_ref[:, gdn_w:gdn_w + ab_w])
        patt_ref[rows, :] = _dot(h, watt_ref[...])


def _ffn1_inproj(x2d, n1, wg, wu, wd, n2, win, watt, gdn_w):
    n, d = x2d.shape
    ab_w = win.shape[1] - gdn_w
    att_w = watt.shape[1]
    rows = FFN1_SLABS * FFN_ROWS
    row_spec = lambda w: pl.BlockSpec((rows, w), lambda i: (i, 0))
    return pl.pallas_call(
        _ffn1_inproj_kernel,
        grid=(n // rows,),
        in_specs=[row_spec(d)] + [_resident()] * 7,
        out_specs=[row_spec(d), row_spec(gdn_w), row_spec(ab_w), row_spec(att_w)],
        out_shape=[jax.ShapeDtypeStruct((n, d), F32),
                   jax.ShapeDtypeStruct((n, gdn_w), F32),
                   jax.ShapeDtypeStruct((n, ab_w), F32),
                   jax.ShapeDtypeStruct((n, att_w), F32)],
        compiler_params=pltpu.CompilerParams(dimension_semantics=("arbitrary",),
                                             vmem_limit_bytes=V7X_VMEM_LIMIT_BYTES),
        name="ffn1_inproj",
    )(x2d, n1, wg, wu, wd, n2, win, watt)


def _outproj_ffn2_kernel(x1_ref, og_ref, z_ref, ya_ref, ogain_ref, wo_ref, n_ref, wg_ref, wu_ref, wd_ref,
                         o_ref):
    gdn_w = og_ref.shape[1]
    Dh = GDN_HEAD_DIM
    for r0 in range(0, x1_ref.shape[0], FFN_ROWS):
        rows = slice(r0, r0 + FFN_ROWS)
        o_gdn = og_ref[rows, :].astype(F32)
        z = z_ref[rows, :]
        y_gdn = jnp.concatenate(
            [_rms_rows(o_gdn[:, h * Dh:(h + 1) * Dh], ogain_ref[...]) * _silu(z[:, h * Dh:(h + 1) * Dh])
             for h in range(GDN_HEADS)], axis=1).astype(BF16)
        x2 = (x1_ref[rows, :] + _dot(y_gdn, wo_ref[0:gdn_w, :])
              + _dot(ya_ref[rows, :], wo_ref[gdn_w:, :]))
        o_ref[rows, :] = x2 + 0.5 * _swiglu(x2, n_ref[...], wg_ref, wu_ref, wd_ref)


def _outproj_ffn2(x1, o_gdn, pg, ya, ogain, wo, nrm, wg, wu, wd):
    n, d = x1.shape
    rows = FFN2_SLABS * FFN_ROWS
    row_spec = lambda w: pl.BlockSpec((rows, w), lambda i: (i, 0))
    return pl.pallas_call(
        _outproj_ffn2_kernel,
        grid=(n // rows,),
        in_specs=[row_spec(d), row_spec(o_gdn.shape[1]),
                  pl.BlockSpec((rows, GDN_WIDTH), lambda i: (i, 3)),
                  row_spec(ya.shape[1])] + [_resident()] * 6,
        out_specs=row_spec(d),
        out_shape=jax.ShapeDtypeStruct((n, d), F32),
        compiler_params=pltpu.CompilerParams(dimension_semantics=("arbitrary",),
                                             vmem_limit_bytes=V7X_VMEM_LIMIT_BYTES),
        name="outproj_ffn2",
    )(x1, o_gdn, pg, ya, ogain, wo, nrm, wg, wu, wd)


def _head_block_ones(width, head_dim):
    r = lax.broadcasted_iota(jnp.int32, (width, width), 0) // head_dim
    c = lax.broadcasted_iota(jnp.int32, (width, width), 1) // head_dim
    return jnp.where(r == c, 1.0, 0.0).astype(BF16)


def _head_rms(x, gain, block_ones, head_dim):
    sq = x * x
    hi = sq.astype(BF16)
    lo = (sq - hi.astype(F32)).astype(BF16)
    ss = _dot(hi, block_ones) + _dot(lo, block_ones)
    return x * lax.rsqrt(ss * (1.0 / head_dim) + NORM_EPS) * gain


def _values_with_ones(v_t):
    Dh = ATT_HEAD_DIM
    row = lax.broadcasted_iota(jnp.int32, (V7X_BF16_SUBLANES, v_t.shape[1]), 0)
    ones_block = jnp.where(row == 0, 1.0, 0.0)
    parts = []
    for h in range(ATT_HEADS):
        parts += [v_t[h * Dh:(h + 1) * Dh], ones_block]
    return jnp.concatenate(parts, axis=0).astype(BF16)


def _mem_kv_kernel(mem_ref, nrm_ref, wkv_ref, kg_ref, k_ref, vt_ref):
    h = _rms_rows(mem_ref[...], nrm_ref[...]).astype(BF16)
    kv = _dot(h, wkv_ref[...])
    k = _head_rms(kv[:, :ATT_WIDTH], kg_ref[...], _head_block_ones(ATT_WIDTH, ATT_HEAD_DIM), ATT_HEAD_DIM)
    k_ref[...] = k.astype(BF16)
    vt_ref[...] = _values_with_ones(kv[:, ATT_WIDTH:].T)


def _mem_kv(mem, nrm, wkv, kgain):
    b, m, d = mem.shape
    return pl.pallas_call(
        _mem_kv_kernel,
        grid=(b,),
        in_specs=[pl.BlockSpec((None, m, d), lambda i: (i, 0, 0)), _resident(), _resident(), _resident()],
        out_specs=[pl.BlockSpec((None, m, ATT_WIDTH), lambda i: (i, 0, 0)),
                   pl.BlockSpec((None, ATT_HEADS * ATT_VROWS, m), lambda i: (i, 0, 0))],
        out_shape=[jax.ShapeDtypeStruct((b, m, ATT_WIDTH), BF16),
                   jax.ShapeDtypeStruct((b, ATT_HEADS * ATT_VROWS, m), BF16)],
        compiler_params=pltpu.CompilerParams(dimension_semantics=("arbitrary",)),
        name="mem_kv",
    )(mem, nrm, wkv, kgain)


def _softplus(x):
    return jnp.maximum(x, 0.0) + jnp.log(1.0 + jnp.exp(-jnp.abs(x)))


def _unit_lower_inverses(a_list, eye):
    x16 = [(-a).astype(BF16) for a in a_list]
    t = [eye - a for a in a_list]
    for _ in range(5):
        x16 = [_dot(xb, xb).astype(BF16) for xb in x16]
        t = [ti + _dot(ti.astype(BF16), xb) for ti, xb in zip(t, x16)]
    t16 = [ti.astype(BF16) for ti in t]
    a_hi = [a.astype(BF16) for a in a_list]
    a_lo = [(a - hi.astype(F32)).astype(BF16) for a, hi in zip(a_list, a_hi)]
    resid = [(eye - tb.astype(F32) - _dot(hi, tb) - _dot(lo, tb)).astype(BF16)
             for tb, hi, lo in zip(t16, a_hi, a_lo)]
    return [tb.astype(F32) + _dot(tb, r) for tb, r in zip(t16, resid)]


def _gdn_kernel(qkv_ref, ab_ref, cw_ref, alog_ref, dtb_ref, o_ref, xbuf_ref, pbuf_ref, s_ref):
    n_seq, rows = o_ref.shape[0], o_ref.shape[1]
    H, Dh, C, W = GDN_HEADS, GDN_HEAD_DIM, GDN_CHUNK, GDN_WIDTH
    G = GDN_GROUP
    n_chunks = rows // C
    n_groups = rows // G

    @pl.when(pl.program_id(1) == 0)
    def _():
        xbuf_ref[:, 0:8, :] = jnp.zeros((n_seq, 8, 3 * W), F32)
        pbuf_ref[:, 0:8, :] = jnp.zeros((n_seq, 8, 3 * W), F32)
        s_ref[...] = jnp.zeros_like(s_ref)

    ri = lax.broadcasted_iota(jnp.int32, (G, G), 0)
    ci = lax.broadcasted_iota(jnp.int32, (G, G), 1)
    same_chunk = (ri // C) == (ci // C)
    lower = same_chunk & (ri >= ci)
    strict = same_chunk & (ri > ci)
    eye = (ri == ci).astype(F32)
    lower_f = lower.astype(F32)

    chains = [(b, h) for b in range(n_seq) for h in range(H)]
    q_c, k_c, v_c, gcol_c, beta_c, grow_c = [], [], [], [], [], []
    for b in range(n_seq):
        x_now = qkv_ref[b]
        xbuf_ref[b, 8:, :] = x_now
        x_prev = xbuf_ref[b, 7:7 + rows, :]
        pbuf_ref[b, 8:, :] = cw_ref[1:2, :] * x_now + cw_ref[0:1, :] * x_prev
        conv = cw_ref[3:4, :] * x_now + cw_ref[2:3, :] * x_prev + pbuf_ref[b, 6:6 + rows, :]
        xbuf_ref[b, 0:8, :] = xbuf_ref[b, rows:rows + 8, :]
        pbuf_ref[b, 0:8, :] = pbuf_ref[b, rows:rows + 8, :]
        act = _silu(conv)

        ab = ab_ref[b]
        log_decay = -jnp.exp(alog_ref[...]) * _softplus(ab + dtb_ref[...])
        beta_all = jax.nn.sigmoid(ab)
        gc_all = jnp.concatenate(
            [_dot(lower_f, log_decay[g * G:(g + 1) * G], HIGHEST) for g in range(n_groups)], axis=0)
        gc_rows = gc_all.T
        for h in range(H):
            qh = act[:, h * Dh:(h + 1) * Dh]
            kh = act[:, W + h * Dh:W + (h + 1) * Dh]
            v_c.append(act[:, 2 * W + h * Dh:2 * W + (h + 1) * Dh])
            q_c.append(qh * (lax.rsqrt(jnp.sum(qh * qh, axis=-1, keepdims=True) + NORM_EPS) * Dh ** -0.5))
            k_c.append(kh * lax.rsqrt(jnp.sum(kh * kh, axis=-1, keepdims=True) + NORM_EPS))
            gcol_c.append(jnp.broadcast_to(gc_all[:, h:h + 1], (rows, Dh)))
            beta_c.append(jnp.broadcast_to(beta_all[:, H + h:H + h + 1], (rows, Dh)))
            grow_c.append(gc_rows[h:h + 1, :])

    units = [(c, g) for c in range(len(chains)) for g in range(n_groups)]

    def unit_rows(x, g):
        return x[g * G:(g + 1) * G]

    k_u = [unit_rows(k_c[c], g) for c, g in units]
    gcol_u = [unit_rows(gcol_c[c], g) for c, g in units]
    beta_u = [unit_rows(beta_c[c], g) for c, g in units]
    kb_u = [k * beta for k, beta in zip(k_u, beta_u)]
    k16_u = [k.astype(BF16) for k in k_u]
    q_u = [unit_rows(q_c[c], g) for c, g in units]
    kk_u = [_dot_nt(kb.astype(BF16), k16) for kb, k16 in zip(kb_u, k16_u)]
    qk_u = [_dot_nt(q.astype(BF16), k16) for q, k16 in zip(q_u, k16_u)]
    decay_u = []
    for gcol, (c, g) in zip(gcol_u, units):
        diff = gcol - grow_c[c][:, g * G:(g + 1) * G]
        decay_u.append(jnp.where(lower, jnp.exp(jnp.where(lower, diff, 0.0)), 0.0))
    a_u = [jnp.where(strict, kk * decay, 0.0) for kk, decay in zip(kk_u, decay_u)]
    qk16_u = [(qk * decay).astype(BF16) for qk, decay in zip(qk_u, decay_u)]
    tinv_u = _unit_lower_inverses(a_u, eye)
    egc_u = [jnp.exp(gcol) for gcol in gcol_u]
    rhs_u = [jnp.concatenate([unit_rows(v_c[c], g) * beta, kb * egc], axis=1).astype(BF16)
             for beta, kb, egc, (c, g) in zip(beta_u, kb_u, egc_u, units)]
    uw_u = [_dot(tinv.astype(BF16), rhs) for tinv, rhs in zip(tinv_u, rhs_u)]
    w16_u = [uw[:, Dh:].astype(BF16) for uw in uw_u]
    qe16_u = [(q * egc).astype(BF16) for q, egc in zip(q_u, egc_u)]

    def chunk_operands(c, n):
        u = c * n_groups + (n * C) // G
        r = slice((n * C) % G, (n * C) % G + C)
        g_last = gcol_u[u][r.stop - 1:r.stop, :]
        return (uw_u[u][r, :Dh],
                jnp.concatenate([w16_u[u][r], qe16_u[u][r]], axis=0),
                qk16_u[u][r, r],
                (k_u[u][r] * jnp.exp(g_last - gcol_u[u][r])).T.astype(BF16),
                jnp.exp(g_last))

    n_chains = len(chains)
    ops = [[chunk_operands(c, n) for n in range(n_chunks)] for c in range(n_chains)]

    state = [s_ref[c] for c in range(n_chains)]
    o_l = [[] for _ in range(n_chains)]
    for n in range(n_chunks):
        ws = [_dot(ops[c][n][1], state[c].astype(BF16)) for c in range(n_chains)]
        v_new = [(ops[c][n][0] - ws[c][:C]).astype(BF16) for c in range(n_chains)]
        state = [state[c] * ops[c][n][4] + _dot(ops[c][n][3], v_new[c]) for c in range(n_chains)]
        for c in range(n_chains):
            o_l[c].append(ws[c][C:] + _dot(ops[c][n][2], v_new[c]))
    for c, (b, h) in enumerate(chains):
        s_ref[c] = state[c]
        o_ref[b, :, h * Dh:(h + 1) * Dh] = jnp.concatenate(o_l[c], axis=0).astype(BF16)


def _gdn(pg, pab, conv_w, alog_vec, dtb_vec, batch, seq):
    rows = GDN_ROWS
    n_seq = GDN_SEQS
    W = GDN_WIDTH
    pg3 = pg.reshape(batch, seq, pg.shape[1])
    pab3 = pab.reshape(batch, seq, pab.shape[1])
    y = pl.pallas_call(
        _gdn_kernel,
        grid=(batch // n_seq, seq // rows),
        in_specs=[pl.BlockSpec((n_seq, rows, 3 * W), lambda b, t: (b, t, 0)),
                  pl.BlockSpec((n_seq, rows, pab.shape[1]), lambda b, t: (b, t, 0)),
                  _resident(), _resident(), _resident()],
        out_specs=pl.BlockSpec((n_seq, rows, W), lambda b, t: (b, t, 0)),
        out_shape=jax.ShapeDtypeStruct((batch, seq, W), BF16),
        scratch_shapes=[pltpu.VMEM((n_seq, rows + 8, 3 * W), F32),
                        pltpu.VMEM((n_seq, rows + 8, 3 * W), F32),
                        pltpu.VMEM((n_seq * GDN_HEADS, GDN_HEAD_DIM, GDN_HEAD_DIM), F32)],
        compiler_params=pltpu.CompilerParams(dimension_semantics=("arbitrary", "arbitrary"),
                                             vmem_limit_bytes=V7X_VMEM_LIMIT_BYTES),
        name="gdn",
    )(pg3, pab3, conv_w, alog_vec, dtb_vec)
    return y.reshape(batch * seq, W)


def _alibi_slope(h):
    return 2.0 ** (-8.0 * (h + 1) / ATT_HEADS)


def _moba_kernel(p_ref, qg_ref, kg_ref, cg_ref, km_ref, vmt_ref, y_ref,
                 k_ref, vt_ref, kmean_ref, sel_ref, m_ref, acc_ref, s_ref, kaug_ref, rhs_ref):
    H, Dh, BLK, Wd = ATT_HEADS, ATT_HEAD_DIM, MOBA_BLOCK, ATT_WIDTH
    PAIR = 2 * Dh
    AUG = V7X_BF16_SUBLANES
    n_seq = p_ref.shape[0]
    nb = kmean_ref.shape[1]
    i = pl.program_id(1)
    pair_row_head = lax.broadcasted_iota(jnp.int32, (PAIR, BLK), 0) // Dh
    chains = [(b, h) for b in range(n_seq) for h in range(H)]

    @pl.when((pl.program_id(0) == 0) & (i == 0))
    def _():
        lane = lax.broadcasted_iota(jnp.int32, (BLK, V7X_LANES), 1)
        key_pos = lax.broadcasted_iota(jnp.int32, (BLK, V7X_LANES), 0).astype(F32)
        kaug_ref[...] = jnp.where(lane < BIAS_TERMS, key_pos,
                                  jnp.where(lane < 2 * BIAS_TERMS, 1.0, 0.0)).astype(BF16)
        rhs_ref[...] = jnp.zeros_like(rhs_ref)

    @pl.when(i == 0)
    def _():
        kmean_ref[...] = jnp.zeros_like(kmean_ref)

    def pair_lanes(h):
        return slice((h // 2) * PAIR, (h // 2 + 1) * PAIR)

    def value_rows(h):
        return slice(h * ATT_VROWS, (h + 1) * ATT_VROWS)

    def normalised(acc):
        return acc[0:Dh] / acc[Dh:Dh + 1]

    def head_queries(q_t, h):
        qp = q_t[pair_lanes(h), :]
        return jnp.where(pair_row_head == h % 2, qp, jnp.zeros_like(qp))

    block_ones = _head_block_ones(Wd, Dh)
    scale = Dh ** -0.5 * LOG2_E
    blk_row = lax.broadcasted_iota(jnp.int32, (nb, Wd), 0)
    lane_head = lax.broadcasted_iota(jnp.int32, (nb, Wd), 1) // Dh
    jdx_f = lax.broadcasted_iota(jnp.int32, (nb, BLK), 0).astype(F32)
    past = lax.broadcasted_iota(jnp.int32, (nb, BLK), 0) < i
    cq_t = []
    for b in range(n_seq):
        qn = _head_rms(p_ref[b, :, 0:Wd], qg_ref[...], block_ones, Dh)
        kn = _head_rms(p_ref[b, :, Wd:2 * Wd], kg_ref[...], block_ones, Dh)
        cqn = _head_rms(p_ref[b, :, 3 * Wd:4 * Wd], cg_ref[...], block_ones, Dh)
        q_t = (qn * scale).T.astype(BF16)
        cq_t.append((cqn * scale).T.astype(BF16))
        k_ref[b, pl.ds(pl.multiple_of(i * BLK, BLK), BLK), :] = kn.astype(BF16)
        vt_ref[b, i] = _values_with_ones(p_ref[b, :, 2 * Wd:3 * Wd].T)
        for h in range(H):
            rhs_ref[b * H + h, 0:PAIR, :] = head_queries(q_t, h)

        kmean = kmean_ref[b]
        kmean_heads = jnp.concatenate([jnp.where(lane_head == h, kmean, 0.0) for h in range(H)], axis=0)
        gate_all = _dot_nt(kmean_heads, qn, HIGHEST)
        for h in range(H):
            gate = jnp.where(past, gate_all[h * nb:(h + 1) * nb], -jnp.inf)
            chosen = jnp.zeros(gate.shape, jnp.bool_)
            for _ in range(MOBA_TOPK):
                top = jnp.max(gate, axis=0, keepdims=True)
                first = jnp.min(jnp.where(gate == top, jdx_f, float(nb)), axis=0, keepdims=True)
                pick = jdx_f == first
                chosen = chosen | pick
                gate = jnp.where(pick, -jnp.inf, gate)
            sel_ref[b * H + h] = jnp.where(chosen & past, 0.0, MASKED)
        kmean_ref[b] = jnp.where(blk_row == i, jnp.mean(kn, axis=0, keepdims=True), kmean)

    aug_row = lax.broadcasted_iota(jnp.int32, (AUG, BLK), 0)
    query_pos = lax.broadcasted_iota(jnp.int32, (1, BLK), 1).astype(F32)

    def scores(blk, own):
        r0 = pl.multiple_of(blk * BLK, BLK)
        distance = (i - blk).astype(F32) * BLK
        keys_aug = kaug_ref[...]
        out = []
        for c, (b, h) in enumerate(chains):
            slope = _alibi_slope(h) * LOG2_E
            row_bias = -slope * (query_pos + distance)
            if not own:
                row_bias = row_bias + sel_ref[c, pl.ds(blk, 1), :]
            rows = _bf16_terms_const(slope) + _bf16_terms(row_bias)
            aug = jnp.zeros((AUG, BLK), F32)
            for n, val in enumerate(rows):
                aug = jnp.where(aug_row == n, val, aug)
            rhs_ref[c, PAIR:PAIR + AUG, :] = aug.astype(BF16)
            keys = jnp.concatenate([k_ref[b, pl.ds(r0, BLK), pair_lanes(h)], keys_aug], axis=1)
            out.append(_dot(keys, rhs_ref[c]))
        return out

    own_scores = scores(i, True)
    mem_scores = [_dot(km_ref[b, :, pair_lanes(h)], head_queries(cq_t[b], h)) for b, h in chains]
    first_past = scores(0, False)
    causal = (lax.broadcasted_iota(jnp.int32, (BLK, BLK), 0) <= lax.broadcasted_iota(jnp.int32, (BLK, BLK), 1))
    own_p, mem_p = [], []
    for c in range(len(chains)):
        s_t = jnp.where(causal, own_scores[c], MASKED)
        m = jnp.max(s_t, axis=0, keepdims=True)
        p = jnp.exp2(s_t - m)
        m_ref[c] = m
        own_p.append(p.astype(BF16))
    for c in range(len(chains)):
        p = jnp.exp2(mem_scores[c] - jnp.max(mem_scores[c], axis=0, keepdims=True))
        mem_p.append(p.astype(BF16))
    for c, (b, h) in enumerate(chains):
        acc_ref[b * 2 * H + h] = _dot(vt_ref[b, i, value_rows(h), :], own_p[c])
        s_ref[0, c] = first_past[c]
    for c, (b, h) in enumerate(chains):
        acc_ref[b * 2 * H + H + h] = _dot(vmt_ref[b, value_rows(h), :], mem_p[c])

    def block_step(j, slot, nxt):
        next_scores = None if nxt is None else scores(nxt, False)
        for c, (b, h) in enumerate(chains):
            a = b * 2 * H + h
            m_old = m_ref[c]
            m_new = jnp.maximum(m_old, jnp.max(s_ref[slot, c], axis=0, keepdims=True))
            m_ref[c] = m_new
            p = jnp.exp2(s_ref[slot, c] - m_new)
            alpha = jnp.exp2(m_old - m_new)
            acc_ref[a] = alpha * acc_ref[a] + _dot(vt_ref[b, j, value_rows(h), :], p.astype(BF16))
        if next_scores is not None:
            for c in range(len(chains)):
                s_ref[1 - slot, c] = next_scores[c]

    last_past = jnp.maximum(i - 1, 0)

    def block_run(j0, count):
        for n in range(count):
            block_step(j0 + n, n % 2, jnp.minimum(j0 + n + 1, last_past))

    def full_run(jj, carry):
        block_run(MOBA_RUN * jj, MOBA_RUN)
        return carry

    lax.fori_loop(0, i // MOBA_RUN, full_run, 0)
    run = MOBA_RUN // 2
    while run >= 2:
        @pl.when(i % (2 * run) >= run)
        def _(run=run):
            block_run((2 * run) * (i // (2 * run)), run)
        run //= 2

    @pl.when(i % 2 == 1)
    def _():
        block_step(i - 1, 0, None)

    for b in range(n_seq):
        out_t = jnp.concatenate([normalised(acc_ref[b * 2 * H + a]) for a in range(2 * H)], axis=0)
        y_ref[b] = out_t.T.astype(BF16)


def _moba(patt, qg, kg, cg, kmem, vmem_t, batch, seq):
    nb = seq // MOBA_BLOCK
    Wd, H, BLK = ATT_WIDTH, ATT_HEADS, MOBA_BLOCK
    n_seq = MOBA_SEQS
    n_mem = kmem.shape[1]
    y = pl.pallas_call(
        _moba_kernel,
        grid=(batch // n_seq, nb),
        in_specs=[pl.BlockSpec((n_seq, BLK, 4 * Wd), lambda b, i: (b, i, 0)),
                  _resident(), _resident(), _resident(),
                  pl.BlockSpec((n_seq, n_mem, Wd), lambda b, i: (b, 0, 0)),
                  pl.BlockSpec((n_seq, H * ATT_VROWS, n_mem), lambda b, i: (b, 0, 0))],
        out_specs=pl.BlockSpec((n_seq, BLK, 2 * Wd), lambda b, i: (b, i, 0)),
        out_shape=jax.ShapeDtypeStruct((batch, seq, 2 * Wd), BF16),
        scratch_shapes=[pltpu.VMEM((n_seq, seq, Wd), BF16),
                        pltpu.VMEM((n_seq, nb, H * ATT_VROWS, BLK), BF16),
                        pltpu.VMEM((n_seq, nb, Wd), F32),
                        pltpu.VMEM((n_seq * H, nb, BLK), F32),
                        pltpu.VMEM((n_seq * H, 1, BLK), F32),
                        pltpu.VMEM((n_seq * 2 * H, ATT_VROWS, BLK), F32),
                        pltpu.VMEM((2, n_seq * H, BLK, BLK), F32),
                        pltpu.VMEM((BLK, V7X_LANES), BF16),
                        pltpu.VMEM((n_seq * H, 2 * ATT_HEAD_DIM + V7X_LANES, BLK), BF16)],
        compiler_params=pltpu.CompilerParams(dimension_semantics=("arbitrary", "arbitrary"),
                                             vmem_limit_bytes=V7X_VMEM_LIMIT_BYTES),
        name="moba",
    )(patt.reshape(batch, seq, 4 * Wd), qg, kg, cg, kmem, vmem_t)
    return y.reshape(batch * seq, 2 * Wd)


def _lane_vector(values, width):
    return jnp.zeros((1, width), F32).at[0, :values.shape[0]].set(values.astype(F32))


def _layer(x2d, mem, batch, seq, ffn1_norm, ffn1_w_gate, ffn1_w_up, ffn1_w_down, mix_norm, w_in,
           gdn_conv_w, gdn_a_log, gdn_dt_bias, gdn_out_norm, moba_q_norm, moba_k_norm,
           mem_norm, w_mem_kv, mem_q_norm, mem_k_norm, w_out, ffn2_norm, ffn2_w_gate, ffn2_w_up, ffn2_w_down):
    W, H = GDN_WIDTH, GDN_HEADS
    row = lambda v: v.reshape(1, -1).astype(F32)
    tile_heads = lambda v: jnp.tile(v.astype(F32), ATT_HEADS).reshape(1, -1)
    ab0 = 4 * W
    att0 = ab0 + 2 * H
    ab_width = V7X_LANES
    w_in16 = w_in.astype(BF16)
    x1, pg, pab, patt = _ffn1_inproj(
        x2d, row(ffn1_norm), ffn1_w_gate.astype(BF16), ffn1_w_up.astype(BF16), ffn1_w_down.astype(BF16),
        row(mix_norm), w_in16[:, :ab0 + ab_width], w_in16[:, att0:], ab0)

    o_gdn = _gdn(pg, pab, gdn_conv_w.astype(F32), _lane_vector(gdn_a_log, ab_width),
                 _lane_vector(gdn_dt_bias, ab_width), batch, seq)

    kmem, vmem_t = _mem_kv(mem, row(mem_norm), w_mem_kv.astype(BF16), tile_heads(mem_k_norm))
    y_att = _moba(patt, tile_heads(moba_q_norm), tile_heads(moba_k_norm), tile_heads(mem_q_norm),
                  kmem, vmem_t, batch, seq)

    return _outproj_ffn2(x1, o_gdn, pg, y_att, row(gdn_out_norm), w_out.astype(BF16), row(ffn2_norm),
                         ffn2_w_gate.astype(BF16), ffn2_w_up.astype(BF16), ffn2_w_down.astype(BF16))


def kernel(x, mem, ffn1_norm, ffn1_w_gate, ffn1_w_up, ffn1_w_down, mix_norm, w_in, gdn_conv_w, gdn_a_log,
           gdn_dt_bias, gdn_out_norm, moba_q_norm, moba_k_norm, mem_norm, w_mem_kv, mem_q_norm, mem_k_norm,
           w_out, ffn2_norm, ffn2_w_gate, ffn2_w_up, ffn2_w_down):
    batch, seq, d = x.shape
    assert seq % MOBA_BLOCK == 0 and seq % GDN_ROWS == 0 and batch % GDN_SEQS == 0 and batch % MOBA_SEQS == 0
    assert (batch * seq) % (FFN1_SLABS * FFN_ROWS) == 0 and (batch * seq) % (FFN2_SLABS * FFN_ROWS) == 0
    depth = w_in.shape[0]
    x2d = x.reshape(batch * seq, d)
    for l in range(depth):
        x2d = _layer(x2d, mem, batch, seq, ffn1_norm[l], ffn1_w_gate[l], ffn1_w_up[l], ffn1_w_down[l],
                     mix_norm[l], w_in[l], gdn_conv_w[l], gdn_a_log[l], gdn_dt_bias[l], gdn_out_norm[l],
                     moba_q_norm[l], moba_k_norm[l], mem_norm[l], w_mem_kv[l], mem_q_norm[l], mem_k_norm[l],
                     w_out[l], ffn2_norm[l], ffn2_w_gate[l], ffn2_w_up[l], ffn2_w_down[l])
    return x2d.reshape(batch, seq, d)
```

```python
import struct

import jax
import jax.numpy as jnp
from jax import lax
from jax.experimental import pallas as pl
from jax.experimental.pallas import tpu as pltpu

F32 = jnp.float32
BF16 = jnp.bfloat16
HIGHEST = lax.Precision.HIGHEST

NORM_EPS = 1e-6
GDN_HEADS = 4
GDN_HEAD_DIM = 128
GDN_WIDTH = GDN_HEADS * GDN_HEAD_DIM
GDN_CONV = 4
GDN_CHUNK = 64
GDN_GROUP = 2 * GDN_CHUNK
ATT_HEADS = 4
ATT_HEAD_DIM = 64
ATT_WIDTH = ATT_HEADS * ATT_HEAD_DIM
MOBA_BLOCK = 256
MOBA_TOPK = 3
MOBA_SEQS = 2
MOBA_RUN = 4
MASKED = -1e30
LOG2_E = 1.4426950408889634
BIAS_TERMS = 3

V7X_VMEM_LIMIT_BYTES = 56 * 1024 * 1024
V7X_LANES = 128
V7X_BF16_SUBLANES = 16
ATT_VROWS = ATT_HEAD_DIM + V7X_BF16_SUBLANES
FFN_ROWS = 256
FFN1_SLABS = 2
FFN2_SLABS = 4
GDN_ROWS = 256
GDN_SEQS = 4


def _dot(a, b, precision=None):
    return jnp.dot(a, b, preferred_element_type=F32, precision=precision)


def _dot_nt(a, b, precision=None):
    return lax.dot_general(a, b, (((1,), (1,)), ((), ())), preferred_element_type=F32,
                           precision=precision)


def _rms_rows(x, gain):
    return x * lax.rsqrt(jnp.mean(x * x, axis=-1, keepdims=True) + NORM_EPS) * gain


def _silu(x):
    half = 0.5 * x
    return half + half * jnp.tanh(half)


def _swiglu(x, gain, wg_ref, wu_ref, wd_ref):
    h = _rms_rows(x, gain).astype(BF16)
    g = _dot(h, wg_ref[...])
    u = _dot(h, wu_ref[...])
    return _dot((_silu(g) * u).astype(BF16), wd_ref[...])


def _resident():
    return pl.BlockSpec(memory_space=pltpu.VMEM)


def _bf16_terms(x):
    terms, rest = [], x
    for _ in range(BIAS_TERMS):
        t = rest.astype(BF16).astype(F32)
        terms.append(t)
        rest = rest - t
    return terms


def _bf16_terms_const(value):
    def f32(v):
        return struct.unpack("<f", struct.pack("<f", v))[0]

    def bf16_round(v):
        bits = struct.unpack("<I", struct.pack("<f", v))[0]
        bits = (bits + 0x7FFF + ((bits >> 16) & 1)) & 0xFFFF0000
        return struct.unpack("<f", struct.pack("<I", bits))[0]

    terms, rest = [], f32(value)
    for _ in range(BIAS_TERMS):
        t = bf16_round(rest)
        terms.append(t)
        rest = f32(rest - t)
    return terms


def _ffn1_inproj_kernel(x_ref, n1_ref, wg_ref, wu_ref, wd_ref, n2_ref, win_ref, watt_ref,
                        x1_ref, pg_ref, pab_ref, patt_ref):
    gdn_w, ab_w = pg_ref.shape[1], pab_ref.shape[1]
    for r0 in range(0, x_ref.shape[0], FFN_ROWS):
        rows = slice(r0, r0 + FFN_ROWS)
        x = x_ref[rows, :]
        x1 = x + 0.5 * _swiglu(x, n1_ref[...], wg_ref, wu_ref, wd_ref)
        x1_ref[rows, :] = x1
        h = _rms_rows(x1, n2_ref[...]).astype(BF16)
        pg_ref[rows, :] = _dot(h, win_ref[:, 0:gdn_w])
        pab_ref[rows, :] = _dot(h, win_ref[:, gdn_w:gdn_w + ab_w])
        patt_ref[rows, :] = _dot(h, watt_ref[...])


def _ffn1_inproj(x2d, n1, wg, wu, wd, n2, win, watt, gdn_w):
    n, d = x2d.shape
    ab_w = win.shape[1] - gdn_w
    att_w = watt.shape[1]
    rows = FFN1_SLABS * FFN_ROWS
    row_spec = lambda w: pl.BlockSpec((rows, w), lambda i: (i, 0))
    return pl.pallas_call(
        _ffn1_inproj_kernel,
        grid=(n // rows,),
        in_specs=[row_spec(d)] + [_resident()] * 7,
        out_specs=[row_spec(d), row_spec(gdn_w), row_spec(ab_w), row_spec(att_w)],
        out_shape=[jax.ShapeDtypeStruct((n, d), F32),
                   jax.ShapeDtypeStruct((n, gdn_w), F32),
                   jax.ShapeDtypeStruct((n, ab_w), F32),
                   jax.ShapeDtypeStruct((n, att_w), F32)],
        compiler_params=pltpu.CompilerParams(dimension_semantics=("arbitrary",),
                                             vmem_limit_bytes=V7X_VMEM_LIMIT_BYTES),
        name="ffn1_inproj",
    )(x2d, n1, wg, wu, wd, n2, win, watt)


def _outproj_ffn2_kernel(x1_ref, og_ref, z_ref, ya_ref, ogain_ref, wo_ref, n_ref, wg_ref, wu_ref, wd_ref,
                         o_ref):
    gdn_w = og_ref.shape[1]
    Dh = GDN_HEAD_DIM
    for r0 in range(0, x1_ref.shape[0], FFN_ROWS):
        rows = slice(r0, r0 + FFN_ROWS)
        o_gdn = og_ref[rows, :].astype(F32)
        z = z_ref[rows, :]
        y_gdn = jnp.concatenate(
            [_rms_rows(o_gdn[:, h * Dh:(h + 1) * Dh], ogain_ref[...]) * _silu(z[:, h * Dh:(h + 1) * Dh])
             for h in range(GDN_HEADS)], axis=1).astype(BF16)
        x2 = (x1_ref[rows, :] + _dot(y_gdn, wo_ref[0:gdn_w, :])
              + _dot(ya_ref[rows, :], wo_ref[gdn_w:, :]))
        o_ref[rows, :] = x2 + 0.5 * _swiglu(x2, n_ref[...], wg_ref, wu_ref, wd_ref)


def _outproj_ffn2(x1, o_gdn, pg, ya, ogain, wo, nrm, wg, wu, wd):
    n, d = x1.shape
    rows = FFN2_SLABS * FFN_ROWS
    row_spec = lambda w: pl.BlockSpec((rows, w), lambda i: (i, 0))
    return pl.pallas_call(
        _outproj_ffn2_kernel,
        grid=(n // rows,),
        in_specs=[row_spec(d), row_spec(o_gdn.shape[1]),
                  pl.BlockSpec((rows, GDN_WIDTH), lambda i: (i, 3)),
                  row_spec(ya.shape[1])] + [_resident()] * 6,
        out_specs=row_spec(d),
        out_shape=jax.ShapeDtypeStruct((n, d), F32),
        compiler_params=pltpu.CompilerParams(dimension_semantics=("arbitrary",),
                                             vmem_limit_bytes=V7X_VMEM_LIMIT_BYTES),
        name="outproj_ffn2",
    )(x1, o_gdn, pg, ya, ogain, wo, nrm, wg, wu, wd)


def _head_block_ones(width, head_dim):
    r = lax.broadcasted_iota(jnp.int32, (width, width), 0) // head_dim
    c = lax.broadcasted_iota(jnp.int32, (width, width), 1) // head_dim
    return jnp.where(r == c, 1.0, 0.0).astype(BF16)


def _head_rms(x, gain, block_ones, head_dim):
    sq = x * x
    hi = sq.astype(BF16)
    lo = (sq - hi.astype(F32)).astype(BF16)
    ss = _dot(hi, block_ones) + _dot(lo, block_ones)
    return x * lax.rsqrt(ss * (1.0 / head_dim) + NORM_EPS) * gain


def _values_with_ones(v_t):
    Dh = ATT_HEAD_DIM
    row = lax.broadcasted_iota(jnp.int32, (V7X_BF16_SUBLANES, v_t.shape[1]), 0)
    ones_block = jnp.where(row == 0, 1.0, 0.0)
    parts = []
    for h in range(ATT_HEADS):
        parts += [v_t[h * Dh:(h + 1) * Dh], ones_block]
    return jnp.concatenate(parts, axis=0).astype(BF16)


def _mem_kv_kernel(mem_ref, nrm_ref, wkv_ref, kg_ref, k_ref, vt_ref):
    h = _rms_rows(mem_ref[...], nrm_ref[...]).astype(BF16)
    kv = _dot(h, wkv_ref[...])
    k = _head_rms(kv[:, :ATT_WIDTH], kg_ref[...], _head_block_ones(ATT_WIDTH, ATT_HEAD_DIM), ATT_HEAD_DIM)
    k_ref[...] = k.astype(BF16)
    vt_ref[...] = _values_with_ones(kv[:, ATT_WIDTH:].T)


def _mem_kv(mem, nrm, wkv, kgain):
    b, m, d = mem.shape
    return pl.pallas_call(
        _mem_kv_kernel,
        grid=(b,),
        in_specs=[pl.BlockSpec((None, m, d), lambda i: (i, 0, 0)), _resident(), _resident(), _resident()],
        out_specs=[pl.BlockSpec((None, m, ATT_WIDTH), lambda i: (i, 0, 0)),
                   pl.BlockSpec((None, ATT_HEADS * ATT_VROWS, m), lambda i: (i, 0, 0))],
        out_shape=[jax.ShapeDtypeStruct((b, m, ATT_WIDTH), BF16),
                   jax.ShapeDtypeStruct((b, ATT_HEADS * ATT_VROWS, m), BF16)],
        compiler_params=pltpu.CompilerParams(dimension_semantics=("arbitrary",)),
        name="mem_kv",
    )(mem, nrm, wkv, kgain)


def _softplus(x):
    return jnp.maximum(x, 0.0) + jnp.log(1.0 + jnp.exp(-jnp.abs(x)))


def _unit_lower_inverses(a_list, eye):
    x16 = [(-a).astype(BF16) for a in a_list]
    t = [eye - a for a in a_list]
    for _ in range(5):
        x16 = [_dot(xb, xb).astype(BF16) for xb in x16]
        t = [ti + _dot(ti.astype(BF16), xb) for ti, xb in zip(t, x16)]
    t16 = [ti.astype(BF16) for ti in t]
    a_hi = [a.astype(BF16) for a in a_list]
    a_lo = [(a - hi.astype(F32)).astype(BF16) for a, hi in zip(a_list, a_hi)]
    resid = [(eye - tb.astype(F32) - _dot(hi, tb) - _dot(lo, tb)).astype(BF16)
             for tb, hi, lo in zip(t16, a_hi, a_lo)]
    return [tb.astype(F32) + _dot(tb, r) for tb, r in zip(t16, resid)]


def _gdn_kernel(qkv_ref, ab_ref, cw_ref, alog_ref, dtb_ref, o_ref, xbuf_ref, pbuf_ref, s_ref):
    n_seq, rows = o_ref.shape[0], o_ref.shape[1]
    H, Dh, C, W = GDN_HEADS, GDN_HEAD_DIM, GDN_CHUNK, GDN_WIDTH
    G = GDN_GROUP
    n_chunks = rows // C
    n_groups = rows // G

    @pl.when(pl.program_id(1) == 0)
    def _():
        xbuf_ref[:, 0:8, :] = jnp.zeros((n_seq, 8, 3 * W), F32)
        pbuf_ref[:, 0:8, :] = jnp.zeros((n_seq, 8, 3 * W), F32)
        s_ref[...] = jnp.zeros_like(s_ref)

    ri = lax.broadcasted_iota(jnp.int32, (G, G), 0)
    ci = lax.broadcasted_iota(jnp.int32, (G, G), 1)
    same_chunk = (ri // C) == (ci // C)
    lower = same_chunk & (ri >= ci)
    strict = same_chunk & (ri > ci)
    eye = (ri == ci).astype(F32)
    lower_f = lower.astype(F32)

    chains = [(b, h) for b in range(n_seq) for h in range(H)]
    q_c, k_c, v_c, gcol_c, beta_c, grow_c = [], [], [], [], [], []
    for b in range(n_seq):
        x_now = qkv_ref[b]
        xbuf_ref[b, 8:, :] = x_now
        x_prev = xbuf_ref[b, 7:7 + rows, :]
        pbuf_ref[b, 8:, :] = cw_ref[1:2, :] * x_now + cw_ref[0:1, :] * x_prev
        conv = cw_ref[3:4, :] * x_now + cw_ref[2:3, :] * x_prev + pbuf_ref[b, 6:6 + rows, :]
        xbuf_ref[b, 0:8, :] = xbuf_ref[b, rows:rows + 8, :]
        pbuf_ref[b, 0:8, :] = pbuf_ref[b, rows:rows + 8, :]
        act = _silu(conv)

        ab = ab_ref[b]
        log_decay = -jnp.exp(alog_ref[...]) * _softplus(ab + dtb_ref[...])
        beta_all = jax.nn.sigmoid(ab)
        gc_all = jnp.concatenate(
            [_dot(lower_f, log_decay[g * G:(g + 1) * G], HIGHEST) for g in range(n_groups)], axis=0)
        gc_rows = gc_all.T
        for h in range(H):
            qh = act[:, h * Dh:(h + 1) * Dh]
            kh = act[:, W + h * Dh:W + (h + 1) * Dh]
            v_c.append(act[:, 2 * W + h * Dh:2 * W + (h + 1) * Dh])
            q_c.append(qh * (lax.rsqrt(jnp.sum(qh * qh, axis=-1, keepdims=True) + NORM_EPS) * Dh ** -0.5))
            k_c.append(kh * lax.rsqrt(jnp.sum(kh * kh, axis=-1, keepdims=True) + NORM_EPS))
            gcol_c.append(jnp.broadcast_to(gc_all[:, h:h + 1], (rows, Dh)))
            beta_c.append(jnp.broadcast_to(beta_all[:, H + h:H + h + 1], (rows, Dh)))
            grow_c.append(gc_rows[h:h + 1, :])

    units = [(c, g) for c in range(len(chains)) for g in range(n_groups)]

    def unit_rows(x, g):
        return x[g * G:(g + 1) * G]

    def wy_stage(some_units):
        k_u = [unit_rows(k_c[c], g) for c, g in some_units]
        gcol_u = [unit_rows(gcol_c[c], g) for c, g in some_units]
        beta_u = [unit_rows(beta_c[c], g) for c, g in some_units]
        kb_u = [k * beta for k, beta in zip(k_u, beta_u)]
        k16_u = [k.astype(BF16) for k in k_u]
        q_u = [unit_rows(q_c[c], g) for c, g in some_units]
        kk_u = [_dot_nt(kb.astype(BF16), k16) for kb, k16 in zip(kb_u, k16_u)]
        qk_u = [_dot_nt(q.astype(BF16), k16) for q, k16 in zip(q_u, k16_u)]
        decay_u = []
        for gcol, (c, g) in zip(gcol_u, some_units):
            diff = gcol - grow_c[c][:, g * G:(g + 1) * G]
            decay_u.append(jnp.where(lower, jnp.exp(jnp.where(lower, diff, 0.0)), 0.0))
        a_u = [jnp.where(strict, kk * decay, 0.0) for kk, decay in zip(kk_u, decay_u)]
        qk16_u = [(qk * decay).astype(BF16) for qk, decay in zip(qk_u, decay_u)]
        tinv_u = _unit_lower_inverses(a_u, eye)
        egc_u = [jnp.exp(gcol) for gcol in gcol_u]
        rhs_u = [jnp.concatenate([unit_rows(v_c[c], g) * beta, kb * egc], axis=1).astype(BF16)
                 for beta, kb, egc, (c, g) in zip(beta_u, kb_u, egc_u, some_units)]
        uw_u = [_dot(tinv.astype(BF16), rhs) for tinv, rhs in zip(tinv_u, rhs_u)]
        w16_u = [uw[:, Dh:].astype(BF16) for uw in uw_u]
        qe16_u = [(q * egc).astype(BF16) for q, egc in zip(q_u, egc_u)]
        return k_u, gcol_u, qk16_u, uw_u, w16_u, qe16_u

    half = len(units) // 2
    k_u, gcol_u, qk16_u, uw_u, w16_u, qe16_u = [a + b for a, b in zip(wy_stage(units[:half]),
                                                                      wy_stage(units[half:]))]

    def chunk_operands(c, n):
        u = c * n_groups + (n * C) // G
        r = slice((n * C) % G, (n * C) % G + C)
        g_last = gcol_u[u][r.stop - 1:r.stop, :]
        return (uw_u[u][r, :Dh],
                jnp.concatenate([w16_u[u][r], qe16_u[u][r]], axis=0),
                qk16_u[u][r, r],
                (k_u[u][r] * jnp.exp(g_last - gcol_u[u][r])).T.astype(BF16),
                jnp.exp(g_last))

    n_chains = len(chains)
    ops = [[chunk_operands(c, n) for n in range(n_chunks)] for c in range(n_chains)]

    state = [s_ref[c] for c in range(n_chains)]
    o_l = [[] for _ in range(n_chains)]
    for n in range(n_chunks):
        ws = [_dot(ops[c][n][1], state[c].astype(BF16)) for c in range(n_chains)]
        v_new = [(ops[c][n][0] - ws[c][:C]).astype(BF16) for c in range(n_chains)]
        state = [state[c] * ops[c][n][4] + _dot(ops[c][n][3], v_new[c]) for c in range(n_chains)]
        for c in range(n_chains):
            o_l[c].append(ws[c][C:] + _dot(ops[c][n][2], v_new[c]))
    for c, (b, h) in enumerate(chains):
        s_ref[c] = state[c]
        o_ref[b, :, h * Dh:(h + 1) * Dh] = jnp.concatenate(o_l[c], axis=0).astype(BF16)


def _gdn(pg, pab, conv_w, alog_vec, dtb_vec, batch, seq):
    rows = GDN_ROWS
    n_seq = GDN_SEQS
    W = GDN_WIDTH
    pg3 = pg.reshape(batch, seq, pg.shape[1])
    pab3 = pab.reshape(batch, seq, pab.shape[1])
    y = pl.pallas_call(
        _gdn_kernel,
        grid=(batch // n_seq, seq // rows),
        in_specs=[pl.BlockSpec((n_seq, rows, 3 * W), lambda b, t: (b, t, 0)),
                  pl.BlockSpec((n_seq, rows, pab.shape[1]), lambda b, t: (b, t, 0)),
                  _resident(), _resident(), _resident()],
        out_specs=pl.BlockSpec((n_seq, rows, W), lambda b, t: (b, t, 0)),
        out_shape=jax.ShapeDtypeStruct((batch, seq, W), BF16),
        scratch_shapes=[pltpu.VMEM((n_seq, rows + 8, 3 * W), F32),
                        pltpu.VMEM((n_seq, rows + 8, 3 * W), F32),
                        pltpu.VMEM((n_seq * GDN_HEADS, GDN_HEAD_DIM, GDN_HEAD_DIM), F32)],
        compiler_params=pltpu.CompilerParams(dimension_semantics=("arbitrary", "arbitrary"),
                                             vmem_limit_bytes=V7X_VMEM_LIMIT_BYTES),
        name="gdn",
    )(pg3, pab3, conv_w, alog_vec, dtb_vec)
    return y.reshape(batch * seq, W)


def _alibi_slope(h):
    return 2.0 ** (-8.0 * (h + 1) / ATT_HEADS)


def _moba_kernel(p_ref, qg_ref, kg_ref, cg_ref, km_ref, vmt_ref, y_ref,
                 k_ref, vt_ref, kmean_ref, sel_ref, m_ref, acc_ref, s_ref, kaug_ref, rhs_ref):
    H, Dh, BLK, Wd = ATT_HEADS, ATT_HEAD_DIM, MOBA_BLOCK, ATT_WIDTH
    PAIR = 2 * Dh
    AUG = V7X_BF16_SUBLANES
    n_seq = p_ref.shape[0]
    nb = kmean_ref.shape[1]
    i = pl.program_id(1)
    pair_row_head = lax.broadcasted_iota(jnp.int32, (PAIR, BLK), 0) // Dh
    chains = [(b, h) for b in range(n_seq) for h in range(H)]

    @pl.when((pl.program_id(0) == 0) & (i == 0))
    def _():
        lane = lax.broadcasted_iota(jnp.int32, (BLK, V7X_LANES), 1)
        key_pos = lax.broadcasted_iota(jnp.int32, (BLK, V7X_LANES), 0).astype(F32)
        kaug_ref[...] = jnp.where(lane < BIAS_TERMS, key_pos,
                                  jnp.where(lane < 2 * BIAS_TERMS, 1.0, 0.0)).astype(BF16)
        rhs_ref[...] = jnp.zeros_like(rhs_ref)

    @pl.when(i == 0)
    def _():
        kmean_ref[...] = jnp.zeros_like(kmean_ref)

    def pair_lanes(h):
        return slice((h // 2) * PAIR, (h // 2 + 1) * PAIR)

    def value_rows(h):
        return slice(h * ATT_VROWS, (h + 1) * ATT_VROWS)

    def normalised(acc):
        return acc[0:Dh] / acc[Dh:Dh + 1]

    def head_queries(q_t, h):
        qp = q_t[pair_lanes(h), :]
        return jnp.where(pair_row_head == h % 2, qp, jnp.zeros_like(qp))

    block_ones = _head_block_ones(Wd, Dh)
    scale = Dh ** -0.5 * LOG2_E
    blk_row = lax.broadcasted_iota(jnp.int32, (nb, Wd), 0)
    lane_head = lax.broadcasted_iota(jnp.int32, (nb, Wd), 1) // Dh
    jdx_f = lax.broadcasted_iota(jnp.int32, (nb, BLK), 0).astype(F32)
    past = lax.broadcasted_iota(jnp.int32, (nb, BLK), 0) < i
    cq_t = []
    for b in range(n_seq):
        qn = _head_rms(p_ref[b, :, 0:Wd], qg_ref[...], block_ones, Dh)
        kn = _head_rms(p_ref[b, :, Wd:2 * Wd], kg_ref[...], block_ones, Dh)
        cqn = _head_rms(p_ref[b, :, 3 * Wd:4 * Wd], cg_ref[...], block_ones, Dh)
        q_t = (qn * scale).T.astype(BF16)
        cq_t.append((cqn * scale).T.astype(BF16))
        k_ref[b, pl.ds(pl.multiple_of(i * BLK, BLK), BLK), :] = kn.astype(BF16)
        vt_ref[b, i] = _values_with_ones(p_ref[b, :, 2 * Wd:3 * Wd].T)
        for h in range(H):
            rhs_ref[b * H + h, 0:PAIR, :] = head_queries(q_t, h)

        kmean = kmean_ref[b]
        kmean_heads = jnp.concatenate([jnp.where(lane_head == h, kmean, 0.0) for h in range(H)], axis=0)
        gate_all = _dot_nt(kmean_heads, qn, HIGHEST)
        for h in range(H):
            gate = jnp.where(past, gate_all[h * nb:(h + 1) * nb], -jnp.inf)
            chosen = jnp.zeros(gate.shape, jnp.bool_)
            for _ in range(MOBA_TOPK):
                top = jnp.max(gate, axis=0, keepdims=True)
                first = jnp.min(jnp.where(gate == top, jdx_f, float(nb)), axis=0, keepdims=True)
                pick = jdx_f == first
                chosen = chosen | pick
                gate = jnp.where(pick, -jnp.inf, gate)
            sel_ref[b * H + h] = jnp.where(chosen & past, 0.0, MASKED)
        kmean_ref[b] = jnp.where(blk_row == i, jnp.mean(kn, axis=0, keepdims=True), kmean)

    aug_row = lax.broadcasted_iota(jnp.int32, (AUG, BLK), 0)
    query_pos = lax.broadcasted_iota(jnp.int32, (1, BLK), 1).astype(F32)

    def scores(blk, own):
        r0 = pl.multiple_of(blk * BLK, BLK)
        distance = (i - blk).astype(F32) * BLK
        keys_aug = kaug_ref[...]
        out = []
        for c, (b, h) in enumerate(chains):
            slope = _alibi_slope(h) * LOG2_E
            row_bias = -slope * (query_pos + distance)
            if not own:
                row_bias = row_bias + sel_ref[c, pl.ds(blk, 1), :]
            rows = _bf16_terms_const(slope) + _bf16_terms(row_bias)
            aug = jnp.zeros((AUG, BLK), F32)
            for n, val in enumerate(rows):
                aug = jnp.where(aug_row == n, val, aug)
            rhs_ref[c, PAIR:PAIR + AUG, :] = aug.astype(BF16)
            keys = jnp.concatenate([k_ref[b, pl.ds(r0, BLK), pair_lanes(h)], keys_aug], axis=1)
            out.append(_dot(keys, rhs_ref[c]))
        return out

    own_scores = scores(i, True)
    mem_scores = [_dot(km_ref[b, :, pair_lanes(h)], head_queries(cq_t[b], h)) for b, h in chains]
    first_past = scores(0, False)
    causal = (lax.broadcasted_iota(jnp.int32, (BLK, BLK), 0) <= lax.broadcasted_iota(jnp.int32, (BLK, BLK), 1))
    own_p, mem_p = [], []
    for c in range(len(chains)):
        s_t = jnp.where(causal, own_scores[c], MASKED)
        m = jnp.max(s_t, axis=0, keepdims=True)
        p = jnp.exp2(s_t - m)
        m_ref[c] = m
        own_p.append(p.astype(BF16))
    for c in range(len(chains)):
        p = jnp.exp2(mem_scores[c] - jnp.max(mem_scores[c], axis=0, keepdims=True))
        mem_p.append(p.astype(BF16))
    for c, (b, h) in enumerate(chains):
        acc_ref[b * 2 * H + h] = _dot(vt_ref[b, i, value_rows(h), :], own_p[c])
        s_ref[0, c] = first_past[c]
    for c, (b, h) in enumerate(chains):
        acc_ref[b * 2 * H + H + h] = _dot(vmt_ref[b, value_rows(h), :], mem_p[c])

    def block_step(j, slot, nxt):
        next_scores = None if nxt is None else scores(nxt, False)
        for c, (b, h) in enumerate(chains):
            a = b * 2 * H + h
            m_old = m_ref[c]
            m_new = jnp.maximum(m_old, jnp.max(s_ref[slot, c], axis=0, keepdims=True))
            m_ref[c] = m_new
            p = jnp.exp2(s_ref[slot, c] - m_new)
            alpha = jnp.exp2(m_old - m_new)
            acc_ref[a] = alpha * acc_ref[a] + _dot(vt_ref[b, j, value_rows(h), :], p.astype(BF16))
        if next_scores is not None:
            for c in range(len(chains)):
                s_ref[1 - slot, c] = next_scores[c]

    last_past = jnp.maximum(i - 1, 0)

    def block_run(j0, count):
        for n in range(count):
            block_step(j0 + n, n % 2, jnp.minimum(j0 + n + 1, last_past))

    def full_run(jj, carry):
        block_run(MOBA_RUN * jj, MOBA_RUN)
        return carry

    lax.fori_loop(0, i // MOBA_RUN, full_run, 0)
    run = MOBA_RUN // 2
    while run >= 2:
        @pl.when(i % (2 * run) >= run)
        def _(run=run):
            block_run((2 * run) * (i // (2 * run)), run)
        run //= 2

    @pl.when(i % 2 == 1)
    def _():
        block_step(i - 1, 0, None)

    for b in range(n_seq):
        out_t = jnp.concatenate([normalised(acc_ref[b * 2 * H + a]) for a in range(2 * H)], axis=0)
        y_ref[b] = out_t.T.astype(BF16)


def _moba(patt, qg, kg, cg, kmem, vmem_t, batch, seq):
    nb = seq // MOBA_BLOCK
    Wd, H, BLK = ATT_WIDTH, ATT_HEADS, MOBA_BLOCK
    n_seq = MOBA_SEQS
    n_mem = kmem.shape[1]
    y = pl.pallas_call(
        _moba_kernel,
        grid=(batch // n_seq, nb),
        in_specs=[pl.BlockSpec((n_seq, BLK, 4 * Wd), lambda b, i: (b, i, 0)),
                  _resident(), _resident(), _resident(),
                  pl.BlockSpec((n_seq, n_mem, Wd), lambda b, i: (b, 0, 0)),
                  pl.BlockSpec((n_seq, H * ATT_VROWS, n_mem), lambda b, i: (b, 0, 0))],
        out_specs=pl.BlockSpec((n_seq, BLK, 2 * Wd), lambda b, i: (b, i, 0)),
        out_shape=jax.ShapeDtypeStruct((batch, seq, 2 * Wd), BF16),
        scratch_shapes=[pltpu.VMEM((n_seq, seq, Wd), BF16),
                        pltpu.VMEM((n_seq, nb, H * ATT_VROWS, BLK), BF16),
                        pltpu.VMEM((n_seq, nb, Wd), F32),
                        pltpu.VMEM((n_seq * H, nb, BLK), F32),
                        pltpu.VMEM((n_seq * H, 1, BLK), F32),
                        pltpu.VMEM((n_seq * 2 * H, ATT_VROWS, BLK), F32),
                        pltpu.VMEM((2, n_seq * H, BLK, BLK), F32),
                        pltpu.VMEM((BLK, V7X_LANES), BF16),
                        pltpu.VMEM((n_seq * H, 2 * ATT_HEAD_DIM + V7X_LANES, BLK), BF16)],
        compiler_params=pltpu.CompilerParams(dimension_semantics=("arbitrary", "arbitrary"),
                                             vmem_limit_bytes=V7X_VMEM_LIMIT_BYTES),
        name="moba",
    )(patt.reshape(batch, seq, 4 * Wd), qg, kg, cg, kmem, vmem_t)
    return y.reshape(batch * seq, 2 * Wd)


def _lane_vector(values, width):
    return jnp.zeros((1, width), F32).at[0, :values.shape[0]].set(values.astype(F32))


def _layer(x2d, mem, batch, seq, ffn1_norm, ffn1_w_gate, ffn1_w_up, ffn1_w_down, mix_norm, w_in,
           gdn_conv_w, gdn_a_log, gdn_dt_bias, gdn_out_norm, moba_q_norm, moba_k_norm,
           mem_norm, w_mem_kv, mem_q_norm, mem_k_norm, w_out, ffn2_norm, ffn2_w_gate, ffn2_w_up, ffn2_w_down):
    W, H = GDN_WIDTH, GDN_HEADS
    row = lambda v: v.reshape(1, -1).astype(F32)
    tile_heads = lambda v: jnp.tile(v.astype(F32), ATT_HEADS).reshape(1, -1)
    ab0 = 4 * W
    att0 = ab0 + 2 * H
    ab_width = V7X_LANES
    w_in16 = w_in.astype(BF16)
    x1, pg, pab, patt = _ffn1_inproj(
        x2d, row(ffn1_norm), ffn1_w_gate.astype(BF16), ffn1_w_up.astype(BF16), ffn1_w_down.astype(BF16),
        row(mix_norm), w_in16[:, :ab0 + ab_width], w_in16[:, att0:], ab0)

    o_gdn = _gdn(pg, pab, gdn_conv_w.astype(F32), _lane_vector(gdn_a_log, ab_width),
                 _lane_vector(gdn_dt_bias, ab_width), batch, seq)

    kmem, vmem_t = _mem_kv(mem, row(mem_norm), w_mem_kv.astype(BF16), tile_heads(mem_k_norm))
    y_att = _moba(patt, tile_heads(moba_q_norm), tile_heads(moba_k_norm), tile_heads(mem_q_norm),
                  kmem, vmem_t, batch, seq)

    return _outproj_ffn2(x1, o_gdn, pg, y_att, row(gdn_out_norm), w_out.astype(BF16), row(ffn2_norm),
                         ffn2_w_gate.astype(BF16), ffn2_w_up.astype(BF16), ffn2_w_down.astype(BF16))


def kernel(x, mem, ffn1_norm, ffn1_w_gate, ffn1_w_up, ffn1_w_down, mix_norm, w_in, gdn_conv_w, gdn_a_log,
           gdn_dt_bias, gdn_out_norm, moba_q_norm, moba_k_norm, mem_norm, w_mem_kv, mem_q_norm, mem_k_norm,
           w_out, ffn2_norm, ffn2_w_gate, ffn2_w_up, ffn2_w_down):
    batch, seq, d = x.shape
    assert seq % MOBA_BLOCK == 0 and seq % GDN_ROWS == 0 and batch % GDN_SEQS == 0 and batch % MOBA_SEQS == 0
    assert (batch * seq) % (FFN1_SLABS * FFN_ROWS) == 0 and (batch * seq) % (FFN2_SLABS * FFN_ROWS) == 0
    depth = w_in.shape[0]
    x2d = x.reshape(batch * seq, d)
    for l in range(depth):
        x2d = _layer(x2d, mem, batch, seq, ffn1_norm[l], ffn1_w_gate[l], ffn1_w_up[l], ffn1_w_down[l],
                     mix_norm[l], w_in[l], gdn_conv_w[l], gdn_a_log[l], gdn_dt_bias[l], gdn_out_norm[l],
                     moba_q_norm[l], moba_k_norm[l], mem_norm[l], w_mem_kv[l], mem_q_norm[l], mem_k_norm[l],
                     w_out[l], ffn2_norm[l], ffn2_w_gate[l], ffn2_w_up[l], ffn2_w_down[l])
    return x2d.reshape(batch, seq, d)
```
